```python
import math
import jax, jax.numpy as jnp
from jax import lax
import numpy as np

D_MODEL = 2048
BATCH = 1
SEQ = 8192
DEPTH = 1

N_HEADS = 16
QK_NOPE_DIM = 128
ROPE_DIM = 64
V_HEAD_DIM = 128
Q_LORA_RANK = 512
KV_LORA_RANK = 512
MLA_WIDTH = N_HEADS * V_HEAD_DIM
CONV_WIDTH = D_MODEL
CONV_K = 3
ROPE_THETA = 10000.0
RMS_EPS = 1e-6
BLOCK_Q = 128

IN_SPLIT = [
    Q_LORA_RANK,
    KV_LORA_RANK,
    ROPE_DIM,
    MLA_WIDTH,
    CONV_WIDTH,
    CONV_WIDTH,
    CONV_WIDTH,
    CONV_WIDTH,
    D_MODEL,
    D_MODEL,
]
IN_TOTAL = sum(IN_SPLIT)

kernel_name = "hybrid_mla_shortconv_gated_merge"


def _rmsnorm(x, g):
    xf = x.astype(jnp.float32)
    r = lax.rsqrt(jnp.mean(xf * xf, axis=-1, keepdims=True) + RMS_EPS)
    return (xf * r * g.astype(jnp.float32)).astype(x.dtype)


def _rope_tables(positions):
    inv_freq = ROPE_THETA ** (-jnp.arange(0, ROPE_DIM, 2, dtype=jnp.float32) / ROPE_DIM)
    ang = positions.astype(jnp.float32)[..., None] * inv_freq
    return jnp.cos(ang), jnp.sin(ang)


def _apply_rope(x, cos, sin):
    xf = x.astype(jnp.float32)
    x1, x2 = jnp.split(xf, 2, axis=-1)
    out = jnp.concatenate([x1 * cos - x2 * sin, x1 * sin + x2 * cos], axis=-1)
    return out.astype(x.dtype)


def _causal_mla_attention(q_nope, q_rope, k_nope, k_rope, v):
    b, s, h, _ = q_nope.shape
    nb = s // BLOCK_Q
    scale = 1.0 / math.sqrt(QK_NOPE_DIM + ROPE_DIM)
    qn_blocks = q_nope.reshape(b, nb, BLOCK_Q, h, QK_NOPE_DIM).transpose(1, 0, 2, 3, 4)
    qr_blocks = q_rope.reshape(b, nb, BLOCK_Q, h, ROPE_DIM).transpose(1, 0, 2, 3, 4)
    key_pos = jnp.arange(s, dtype=jnp.int32)

    def one_block(args):
        qn, qr, blk = args
        sc = (jnp.einsum('bqhd,bkhd->bhqk', qn, k_nope)
              + jnp.einsum('bqhr,bkr->bhqk', qr, k_rope)).astype(jnp.float32) * scale
        q_pos = blk * BLOCK_Q + jnp.arange(BLOCK_Q, dtype=jnp.int32)
        mask = key_pos[None, :] <= q_pos[:, None]
        sc = jnp.where(mask[None, None], sc, -jnp.inf)
        p = jax.nn.softmax(sc, axis=-1).astype(v.dtype)
        return jnp.einsum('bhqk,bkhd->bqhd', p, v)

    out = lax.map(one_block, (qn_blocks, qr_blocks, jnp.arange(nb, dtype=jnp.int32)))
    return out.transpose(1, 0, 2, 3, 4).reshape(b, s, h, V_HEAD_DIM)


def setup_inputs(seed: int = 0) -> dict:
    key = jax.random.key(seed)
    ks = jax.random.split(key, 16)
    f32 = jnp.float32

    def w(k, shape, fan_in):
        return jax.random.normal(k, shape, f32) * (fan_in ** -0.5)

    def gain(k, n):
        return 1.0 + 0.02 * jax.random.normal(k, (n,), f32)

    x = jax.random.normal(ks[0], (BATCH, SEQ, D_MODEL), f32)
    start = jax.random.randint(ks[1], (BATCH, 1), 0, 1024, dtype=jnp.int32)
    positions = start + jnp.arange(SEQ, dtype=jnp.int32)[None, :]
    return {
        "x": x,
        "positions": positions,
        "pre_norm_g": gain(ks[2], D_MODEL),
        "w_in": w(ks[3], (D_MODEL, IN_TOTAL), D_MODEL),
        "q_a_norm_g": gain(ks[4], Q_LORA_RANK),
        "w_q_b": w(ks[5], (Q_LORA_RANK, N_HEADS * (QK_NOPE_DIM + ROPE_DIM)), Q_LORA_RANK),
        "kv_a_norm_g": gain(ks[6], KV_LORA_RANK),
        "w_kv_b": w(ks[7], (KV_LORA_RANK, N_HEADS * (QK_NOPE_DIM + V_HEAD_DIM)), KV_LORA_RANK),
        "conv_w": w(ks[8], (CONV_K, CONV_WIDTH), CONV_K),
        "w_o_mla": w(ks[9], (MLA_WIDTH, D_MODEL), MLA_WIDTH),
        "w_o_conv": w(ks[10], (CONV_WIDTH, D_MODEL), CONV_WIDTH),
        "w_out": w(ks[11], (D_MODEL, D_MODEL), D_MODEL),
        "post_norm_g": gain(ks[12], D_MODEL),
    }


def reference(x, positions, pre_norm_g, w_in, q_a_norm_g, w_q_b, kv_a_norm_g, w_kv_b,
              conv_w, w_o_mla, w_o_conv, w_out, post_norm_g):
    b, s, _ = x.shape
    cos, sin = _rope_tables(positions)
    for _layer in range(DEPTH):
        h = _rmsnorm(x, pre_norm_g)
        proj = jnp.einsum('bsd,de->bse', h, w_in)
        cuts = [int(c) for c in np.cumsum(IN_SPLIT)[:-1]]
        (q_a, c_kv, k_rope, z_mla, c_in, b_gate, c_gate, z_conv,
         g_mla, g_conv) = jnp.split(proj, cuts, axis=-1)

        q = jnp.einsum('bsr,re->bse', _rmsnorm(q_a, q_a_norm_g), w_q_b)
        q = q.reshape(b, s, N_HEADS, QK_NOPE_DIM + ROPE_DIM)
        q_nope, q_rope = q[..., :QK_NOPE_DIM], q[..., QK_NOPE_DIM:]
        kv = jnp.einsum('bsr,re->bse', _rmsnorm(c_kv, kv_a_norm_g), w_kv_b)
        kv = kv.reshape(b, s, N_HEADS, QK_NOPE_DIM + V_HEAD_DIM)
        k_nope, v = kv[..., :QK_NOPE_DIM], kv[..., QK_NOPE_DIM:]
        q_rope = _apply_rope(q_rope, cos[:, :, None, :], sin[:, :, None, :])
        k_rope = _apply_rope(k_rope, cos, sin)
        attn = _causal_mla_attention(q_nope, q_rope, k_nope, k_rope, v)
        attn = attn.reshape(b, s, MLA_WIDTH) * jax.nn.silu(z_mla)
        y_mla = jnp.einsum('bse,ed->bsd', attn, w_o_mla)

        u = c_gate * c_in
        u_pad = jnp.pad(u, ((0, 0), (CONV_K - 1, 0), (0, 0)))
        conv = sum(conv_w[k] * u_pad[:, k:k + s] for k in range(CONV_K))
        y_conv = jnp.einsum('bse,ed->bsd', b_gate * conv * jax.nn.silu(z_conv), w_o_conv)

        merged = jax.nn.sigmoid(g_mla) * y_mla + jax.nn.sigmoid(g_conv) * y_conv
        out = jnp.einsum('bsd,de->bse', merged, w_out)
        x = x + _rmsnorm(out, post_norm_g)
    return x
```

```python
import functools
import math

import jax
import jax.numpy as jnp
from jax import lax
from jax.experimental import pallas as pl
from jax.experimental.pallas import tpu as pltpu

D_MODEL = 2048
N_HEADS = 16
QK_NOPE_DIM = 128
ROPE_DIM = 64
V_HEAD_DIM = 128
Q_LORA_RANK = 512
KV_LORA_RANK = 512
MLA_WIDTH = N_HEADS * V_HEAD_DIM
CONV_WIDTH = D_MODEL
CONV_K = 3
ROPE_THETA = 10000.0
RMS_EPS = 1e-6

LANES = 128
SUBLANES = 8
QK_PAD = QK_NOPE_DIM + LANES
LAT_WIDTH = Q_LORA_RANK + KV_LORA_RANK + LANES
VMEM_LIMIT = 56 * 1024 * 1024

BF16 = jnp.bfloat16
F32 = jnp.float32


def _params(semantics):
    return pltpu.CompilerParams(dimension_semantics=semantics, vmem_limit_bytes=VMEM_LIMIT)


def _resident(shape):
    return pl.BlockSpec(shape, lambda *_: (0,) * len(shape), pipeline_mode=pl.Buffered(1))


def _rms(xf, g):
    r = lax.rsqrt(jnp.mean(xf * xf, axis=-1, keepdims=True) + RMS_EPS)
    return xf * r * g


def _dot(a, b):
    return jnp.dot(a, b, preferred_element_type=F32)


def _prenorm_kernel(x_ref, g_ref, h_ref):
    h_ref[...] = _rms(x_ref[...], g_ref[...]).astype(BF16)


def _prenorm(x, g, tm=512):
    s, d = x.shape
    return pl.pallas_call(
        _prenorm_kernel,
        grid=(s // tm,),
        in_specs=[pl.BlockSpec((tm, d), lambda i: (i, 0)), _resident((1, d))],
        out_specs=pl.BlockSpec((tm, d), lambda i: (i, 0)),
        out_shape=jax.ShapeDtypeStruct((s, d), BF16),
        compiler_params=_params(("parallel",)),
        name="prenorm",
    )(x, g)


def _rope128(g2, cos_t, sin_t):
    return g2 * cos_t + pltpu.roll(g2, ROPE_DIM, axis=1) * sin_t


def _latent_kernel(h_ref, wlat_ref, gq_ref, gkv_ref, wq_ref, wkv_ref, cos_ref, sin_ref,
                   q_ref, k_ref, v_ref, *, scale):
    lat = _dot(h_ref[...], wlat_ref[...])
    qa = _rms(lat[:, :Q_LORA_RANK], gq_ref[...]).astype(BF16)
    ckv = _rms(lat[:, Q_LORA_RANK:Q_LORA_RANK + KV_LORA_RANK], gkv_ref[...]).astype(BF16)
    cos_t = cos_ref[...]
    sin_t = sin_ref[...]
    k_rope = _rope128(lat[:, Q_LORA_RANK + KV_LORA_RANK:], cos_t, sin_t).astype(BF16)
    q = _dot(qa, wq_ref[...]) * scale
    kv = _dot(ckv, wkv_ref[...])
    for hd in range(N_HEADS):
        base = hd * QK_PAD
        q_ref[hd, :, :QK_NOPE_DIM] = q[:, base:base + QK_NOPE_DIM].astype(BF16)
        q_ref[hd, :, QK_NOPE_DIM:] = _rope128(q[:, base + QK_NOPE_DIM:base + QK_PAD],
                                              cos_t, sin_t).astype(BF16)
        k_ref[hd, :, :QK_NOPE_DIM] = kv[:, hd * QK_NOPE_DIM:(hd + 1) * QK_NOPE_DIM].astype(BF16)
        k_ref[hd, :, QK_NOPE_DIM:] = k_rope
        v_ref[hd] = kv[:, MLA_WIDTH + hd * V_HEAD_DIM:MLA_WIDTH + (hd + 1) * V_HEAD_DIM].astype(BF16)


def _latent(h, wlat, gq, gkv, wq, wkv, cos_t, sin_t, tm=256):
    s, d = h.shape
    scale = 1.0 / math.sqrt(QK_NOPE_DIM + ROPE_DIM)
    return pl.pallas_call(
        functools.partial(_latent_kernel, scale=scale),
        grid=(s // tm,),
        in_specs=[
            pl.BlockSpec((tm, d), lambda i: (i, 0)),
            _resident(wlat.shape), _resident(gq.shape), _resident(gkv.shape),
            _resident(wq.shape), _resident(wkv.shape),
            pl.BlockSpec((tm, LANES), lambda i: (i, 0)),
            pl.BlockSpec((tm, LANES), lambda i: (i, 0)),
        ],
        out_specs=[
            pl.BlockSpec((N_HEADS, tm, QK_PAD), lambda i: (0, i, 0)),
            pl.BlockSpec((N_HEADS, tm, QK_PAD), lambda i: (0, i, 0)),
            pl.BlockSpec((N_HEADS, tm, V_HEAD_DIM), lambda i: (0, i, 0)),
        ],
        out_shape=[
            jax.ShapeDtypeStruct((N_HEADS, s, QK_PAD), BF16),
            jax.ShapeDtypeStruct((N_HEADS, s, QK_PAD), BF16),
            jax.ShapeDtypeStruct((N_HEADS, s, V_HEAD_DIM), BF16),
        ],
        compiler_params=_params(("parallel",)),
        name="latent",
    )(h, wlat, gq, gkv, wq, wkv, cos_t, sin_t)


def _gates_kernel(h_ref, w_ref, o_ref, *, n_silu_blocks):
    r = _dot(h_ref[...], w_ref[...])
    sg = jax.nn.sigmoid(r)
    is_silu = pl.program_id(0) < n_silu_blocks
    o_ref[...] = jnp.where(is_silu, r * sg, sg).astype(BF16)


def _gates(h, w, n_silu_cols, tm=1024, tn=1024):
    s, d = h.shape
    n = w.shape[1]
    return pl.pallas_call(
        functools.partial(_gates_kernel, n_silu_blocks=n_silu_cols // tn),
        grid=(n // tn, s // tm),
        in_specs=[pl.BlockSpec((tm, d), lambda j, i: (i, 0)),
                  pl.BlockSpec((d, tn), lambda j, i: (0, j))],
        out_specs=pl.BlockSpec((tm, tn), lambda j, i: (i, j)),
        out_shape=jax.ShapeDtypeStruct((s, n), BF16),
        compiler_params=_params(("parallel", "parallel")),
        name="gates",
    )(h, w)


def _conv_kernel(h_ref, w_ref, cw_ref, o_ref, carry_ref, *, tn):
    i = pl.program_id(1)

    @pl.when(i == 0)
    def _():
        carry_ref[...] = jnp.zeros_like(carry_ref)

    r = _dot(h_ref[...], w_ref[...])
    c_in, b_gate, c_gate, z_conv = (r[:, k * tn:(k + 1) * tn] for k in range(4))
    u = c_gate * c_in
    prev = carry_ref[...]
    row = lax.broadcasted_iota(jnp.int32, u.shape, 0)
    u1 = jnp.where(row == 0, prev[SUBLANES - 1:SUBLANES, :], pltpu.roll(u, 1, axis=0))
    u2 = jnp.where(row == 0, prev[SUBLANES - 2:SUBLANES - 1, :],
                   jnp.where(row == 1, prev[SUBLANES - 1:SUBLANES, :], pltpu.roll(u, 2, axis=0)))
    cw = cw_ref[...]
    conv = cw[0:1, :] * u2 + cw[1:2, :] * u1 + cw[2:3, :] * u
    o_ref[...] = (b_gate * conv * (z_conv * jax.nn.sigmoid(z_conv))).astype(BF16)
    carry_ref[...] = u[u.shape[0] - SUBLANES:, :]


def _conv(h, w4, conv_w, tm=1024, tn=512):
    s, d = h.shape
    return pl.pallas_call(
        functools.partial(_conv_kernel, tn=tn),
        grid=(CONV_WIDTH // tn, s // tm),
        in_specs=[pl.BlockSpec((tm, d), lambda j, i: (i, 0)),
                  pl.BlockSpec((d, 4 * tn), lambda j, i: (0, j)),
                  pl.BlockSpec((CONV_K, tn), lambda j, i: (0, j))],
        out_specs=pl.BlockSpec((tm, tn), lambda j, i: (i, j)),
        out_shape=jax.ShapeDtypeStruct((s, CONV_WIDTH), BF16),
        scratch_shapes=[pltpu.VMEM((SUBLANES, tn), F32)],
        compiler_params=_params(("arbitrary", "arbitrary")),
        name="conv",
    )(h, w4, conv_w)


def _attn_kernel(q_ref, k_ref, v_ref, o_ref, acc_ref, *, tq):
    i = pl.program_id(1)
    q = q_ref[0]

    def step(j, carry, masked):
        m_prev, l_prev = carry
        start = pl.multiple_of(j * tq, tq)
        k = k_ref[0, pl.ds(start, tq), :]
        v = v_ref[0, pl.ds(start, tq), :]
        s = lax.dot_general(q, k, (((1,), (1,)), ((), ())), preferred_element_type=F32)
        if masked:
            row = lax.broadcasted_iota(jnp.int32, s.shape, 0)
            col = lax.broadcasted_iota(jnp.int32, s.shape, 1)
            s = jnp.where(col <= row, s, -jnp.inf)
        m_new = jnp.maximum(m_prev, jnp.max(s, axis=1, keepdims=True))
        alpha = jnp.exp(m_prev - m_new)
        p = jnp.exp(s - m_new)
        l_new = alpha * l_prev + jnp.sum(p, axis=1, keepdims=True)
        acc_ref[...] = alpha * acc_ref[...] + _dot(p.astype(BF16), v)
        return m_new, l_new

    acc_ref[...] = jnp.zeros_like(acc_ref)
    carry = (jnp.full((tq, 1), -jnp.inf, F32), jnp.zeros((tq, 1), F32))
    carry = lax.fori_loop(0, i, functools.partial(step, masked=False), carry)
    _, l_fin = step(i, carry, masked=True)
    o_ref[...] = (acc_ref[...] / l_fin).astype(BF16)


def _attention(q, k, v, tq=512):
    nh, s, _ = q.shape
    return pl.pallas_call(
        functools.partial(_attn_kernel, tq=tq),
        grid=(nh, s // tq),
        in_specs=[pl.BlockSpec((1, tq, QK_PAD), lambda h, i: (h, i, 0)),
                  pl.BlockSpec((1, s, QK_PAD), lambda h, i: (h, 0, 0)),
                  pl.BlockSpec((1, s, V_HEAD_DIM), lambda h, i: (h, 0, 0))],
        out_specs=pl.BlockSpec((tq, V_HEAD_DIM), lambda h, i: (i, h)),
        out_shape=jax.ShapeDtypeStruct((s, nh * V_HEAD_DIM), BF16),
        scratch_shapes=[pltpu.VMEM((tq, V_HEAD_DIM), F32)],
        compiler_params=_params(("parallel", "arbitrary")),
        name="attention",
    )(q, k, v)


def _output_kernel(x_ref, attn_ref, sz_ref, sgm_ref, co_ref, sgc_ref,
                   womla_ref, woconv_ref, wout_ref, g_ref, o_ref):
    a = (attn_ref[...].astype(F32) * sz_ref[...].astype(F32)).astype(BF16)
    y_mla = _dot(a, womla_ref[...])
    y_conv = _dot(co_ref[...], woconv_ref[...])
    merged = sgm_ref[...].astype(F32) * y_mla + sgc_ref[...].astype(F32) * y_conv
    out = _dot(merged.astype(BF16), wout_ref[...])
    o_ref[...] = x_ref[...] + _rms(out, g_ref[...])


def _output(x, attn, gates, co, womla, woconv, wout, g, tm=256):
    s, d = x.shape
    row = lambda c: pl.BlockSpec((tm, d), lambda i: (i, c))
    return pl.pallas_call(
        _output_kernel,
        grid=(s // tm,),
        in_specs=[row(0), row(0), row(0), row(1), row(0), row(2),
                  _resident(womla.shape), _resident(woconv.shape), _resident(wout.shape),
                  _resident(g.shape)],
        out_specs=row(0),
        out_shape=jax.ShapeDtypeStruct((s, d), F32),
        compiler_params=_params(("parallel",)),
        name="output",
    )(x, attn, gates, gates, co, gates, womla, woconv, wout, g)


def _swap_halves(r):
    half = r.shape[-1] // 2
    return jnp.concatenate([r[..., half:], r[..., :half]], axis=-1)


def _split_w_in(w_in):
    cuts, acc = [], 0
    for width in (Q_LORA_RANK, KV_LORA_RANK, ROPE_DIM, MLA_WIDTH, CONV_WIDTH, CONV_WIDTH,
                  CONV_WIDTH, CONV_WIDTH, D_MODEL, D_MODEL):
        cuts.append(w_in[:, acc:acc + width])
        acc += width
    return cuts


def kernel(x, positions, pre_norm_g, w_in, q_a_norm_g, w_q_b, kv_a_norm_g, w_kv_b, conv_w,
           w_o_mla, w_o_conv, w_out, post_norm_g):
    b, s, d = x.shape
    assert b == 1 and d == D_MODEL
    x2 = x[0]
    conv_tn = 512

    (w_qa, w_ckv, w_kr, w_zm, w_ci, w_bg, w_cg, w_zc, w_gm, w_gc) = _split_w_in(w_in)
    wlat = jnp.concatenate([w_qa, w_ckv, w_kr, _swap_halves(w_kr)], axis=1).astype(BF16)
    wq3 = w_q_b.reshape(Q_LORA_RANK, N_HEADS, QK_NOPE_DIM + ROPE_DIM)
    wq_rope = wq3[..., QK_NOPE_DIM:]
    wq = jnp.concatenate([wq3[..., :QK_NOPE_DIM], wq_rope, _swap_halves(wq_rope)], axis=-1)
    wq = wq.reshape(Q_LORA_RANK, N_HEADS * QK_PAD).astype(BF16)
    wkv3 = w_kv_b.reshape(KV_LORA_RANK, N_HEADS, QK_NOPE_DIM + V_HEAD_DIM)
    wkv = jnp.concatenate([wkv3[..., :QK_NOPE_DIM].reshape(KV_LORA_RANK, -1),
                           wkv3[..., QK_NOPE_DIM:].reshape(KV_LORA_RANK, -1)], axis=1).astype(BF16)
    w_gates = jnp.concatenate([w_zm, w_gm, w_gc], axis=1).astype(BF16)
    nj = CONV_WIDTH // conv_tn
    w4 = jnp.stack([w.reshape(d, nj, conv_tn) for w in (w_ci, w_bg, w_cg, w_zc)], axis=2)
    w4 = w4.reshape(d, nj * 4 * conv_tn).astype(BF16)

    inv_freq = ROPE_THETA ** (-jnp.arange(0, ROPE_DIM, 2, dtype=F32) / ROPE_DIM)
    ang = positions[0].astype(F32)[:, None] * inv_freq
    cos, sin = jnp.cos(ang), jnp.sin(ang)
    zeros = jnp.zeros((s, LANES - ROPE_DIM), F32)
    cos_t = jnp.concatenate([cos, cos, zeros], axis=1)
    sin_t = jnp.concatenate([-sin, sin, zeros], axis=1)

    h = _prenorm(x2, pre_norm_g.reshape(1, d))
    q, k, v = _latent(h, wlat, q_a_norm_g.reshape(1, -1), kv_a_norm_g.reshape(1, -1), wq, wkv,
                      cos_t, sin_t)
    gates = _gates(h, w_gates, MLA_WIDTH)
    co = _conv(h, w4, conv_w, tn=conv_tn)
    attn = _attention(q, k, v)
    out = _output(x2, attn, gates, co, w_o_mla.astype(BF16), w_o_conv.astype(BF16),
                  w_out.astype(BF16), post_norm_g.reshape(1, d))
    return out[None]
```

```python
import functools
import math

import jax
import jax.numpy as jnp
from jax import lax
from jax.experimental import pallas as pl
from jax.experimental.pallas import tpu as pltpu

D_MODEL = 2048
N_HEADS = 16
QK_NOPE_DIM = 128
ROPE_DIM = 64
V_HEAD_DIM = 128
Q_LORA_RANK = 512
KV_LORA_RANK = 512
MLA_WIDTH = N_HEADS * V_HEAD_DIM
CONV_WIDTH = D_MODEL
CONV_K = 3
ROPE_THETA = 10000.0
RMS_EPS = 1e-6

LANES = 128
SUBLANES = 8
QK_PAD = QK_NOPE_DIM + LANES
V_PAD = V_HEAD_DIM + LANES
LAT_WIDTH = Q_LORA_RANK + KV_LORA_RANK + LANES
LAT_IN = Q_LORA_RANK + KV_LORA_RANK + ROPE_DIM
MAIN_Z_MLA = 0
MAIN_C_IN = MAIN_Z_MLA + MLA_WIDTH
MAIN_B_GATE = MAIN_C_IN + CONV_WIDTH
MAIN_C_GATE = MAIN_B_GATE + CONV_WIDTH
MAIN_Z_CONV = MAIN_C_GATE + CONV_WIDTH
MAIN_G_MLA = MAIN_Z_CONV + CONV_WIDTH
MAIN_G_CONV = MAIN_G_MLA + D_MODEL
VMEM_LIMIT = 56 * 1024 * 1024

BF16 = jnp.bfloat16
F32 = jnp.float32


def _params(semantics):
    return pltpu.CompilerParams(dimension_semantics=semantics, vmem_limit_bytes=VMEM_LIMIT)


def _resident(shape):
    return pl.BlockSpec(shape, lambda *_: (0,) * len(shape), pipeline_mode=pl.Buffered(1))


def _rms(xf, g):
    r = lax.rsqrt(jnp.mean(xf * xf, axis=-1, keepdims=True) + RMS_EPS)
    return xf * r * g


def _dot(a, b):
    return jnp.dot(a, b, preferred_element_type=F32)


def _prenorm_kernel(x_ref, g_ref, h_ref):
    h_ref[...] = _rms(x_ref[...], g_ref[...]).astype(BF16)


def _prenorm(x, g, tm=512):
    s, d = x.shape
    return pl.pallas_call(
        _prenorm_kernel,
        grid=(s // tm,),
        in_specs=[pl.BlockSpec((tm, d), lambda i: (i, 0)), _resident((1, d))],
        out_specs=pl.BlockSpec((tm, d), lambda i: (i, 0)),
        out_shape=jax.ShapeDtypeStruct((s, d), BF16),
        compiler_params=_params(("parallel",)),
        name="prenorm",
    )(x, g)


def _rope128(g2, cos_t, sin_t):
    return g2 * cos_t + pltpu.roll(g2, ROPE_DIM, axis=1) * sin_t


def _latent_kernel(h_ref, wlat_ref, gq_ref, gkv_ref, wq_ref, wkv_ref, cos_ref, sin_ref,
                   q_ref, k_ref, v_ref, *, scale):
    lat = _dot(h_ref[...], wlat_ref[...])
    qa = _rms(lat[:, :Q_LORA_RANK], gq_ref[...]).astype(BF16)
    ckv = _rms(lat[:, Q_LORA_RANK:Q_LORA_RANK + KV_LORA_RANK], gkv_ref[...]).astype(BF16)
    cos_t = cos_ref[...]
    sin_t = sin_ref[...]
    k_rope = _rope128(lat[:, Q_LORA_RANK + KV_LORA_RANK:], cos_t, sin_t).astype(BF16)
    q = _dot(qa, wq_ref[...]) * scale
    kv = _dot(ckv, wkv_ref[...])
    for hd in range(N_HEADS):
        base = hd * QK_PAD
        q_ref[hd, :, :QK_NOPE_DIM] = q[:, base:base + QK_NOPE_DIM].astype(BF16)
        q_ref[hd, :, QK_NOPE_DIM:] = _rope128(q[:, base + QK_NOPE_DIM:base + QK_PAD],
                                              cos_t, sin_t).astype(BF16)
        k_ref[hd, :, :QK_NOPE_DIM] = kv[:, hd * QK_NOPE_DIM:(hd + 1) * QK_NOPE_DIM].astype(BF16)
        k_ref[hd, :, QK_NOPE_DIM:] = k_rope
        v_ref[hd, :, :V_HEAD_DIM] = kv[:, MLA_WIDTH + hd * V_HEAD_DIM:
                                       MLA_WIDTH + (hd + 1) * V_HEAD_DIM].astype(BF16)
        v_ref[hd, :, V_HEAD_DIM:] = jnp.ones((kv.shape[0], LANES), BF16)


def _latent(h, wlat, gq, gkv, wq, wkv, cos_t, sin_t, tm=256):
    s, d = h.shape
    scale = math.log2(math.e) / math.sqrt(QK_NOPE_DIM + ROPE_DIM)
    return pl.pallas_call(
        functools.partial(_latent_kernel, scale=scale),
        grid=(s // tm,),
        in_specs=[
            pl.BlockSpec((tm, d), lambda i: (i, 0)),
            _resident(wlat.shape), _resident(gq.shape), _resident(gkv.shape),
            _resident(wq.shape), _resident(wkv.shape),
            pl.BlockSpec((tm, LANES), lambda i: (i, 0)),
            pl.BlockSpec((tm, LANES), lambda i: (i, 0)),
        ],
        out_specs=[
            pl.BlockSpec((N_HEADS, tm, QK_PAD), lambda i: (0, i, 0)),
            pl.BlockSpec((N_HEADS, tm, QK_PAD), lambda i: (0, i, 0)),
            pl.BlockSpec((N_HEADS, tm, V_PAD), lambda i: (0, i, 0)),
        ],
        out_shape=[
            jax.ShapeDtypeStruct((N_HEADS, s, QK_PAD), BF16),
            jax.ShapeDtypeStruct((N_HEADS, s, QK_PAD), BF16),
            jax.ShapeDtypeStruct((N_HEADS, s, V_PAD), BF16),
        ],
        compiler_params=_params(("parallel",)),
        name="latent",
    )(h, wlat, gq, gkv, wq, wkv, cos_t, sin_t)


def _gates_kernel(h_ref, w_ref, o_ref, *, n_silu_blocks):
    r = _dot(h_ref[...], w_ref[...])
    sg = jax.nn.sigmoid(r)
    is_silu = pl.program_id(0) < n_silu_blocks
    o_ref[...] = jnp.where(is_silu, r * sg, sg).astype(BF16)


def _gates(h, w_main, tm=1024, tn=1024):
    s, d = h.shape
    n_silu_blocks = MLA_WIDTH // tn
    skip_blocks = (MAIN_G_MLA - MLA_WIDTH) // tn
    return pl.pallas_call(
        functools.partial(_gates_kernel, n_silu_blocks=n_silu_blocks),
        grid=(3 * D_MODEL // tn, s // tm),
        in_specs=[pl.BlockSpec((tm, d), lambda j, i: (i, 0)),
                  pl.BlockSpec((d, tn), lambda j, i: (0, jnp.where(j < n_silu_blocks, j, j + skip_blocks)))],
        out_specs=pl.BlockSpec((tm, tn), lambda j, i: (i, j)),
        out_shape=jax.ShapeDtypeStruct((s, 3 * D_MODEL), BF16),
        compiler_params=_params(("parallel", "parallel")),
        name="gates",
    )(h, w_main)


def _conv_kernel(h_ref, wci_ref, wbg_ref, wcg_ref, wzc_ref, cw_ref, o_ref, carry_ref):
    i = pl.program_id(1)

    @pl.when(i == 0)
    def _():
        carry_ref[...] = jnp.zeros_like(carry_ref)

    h = h_ref[...]
    u = _dot(h, wcg_ref[...]) * _dot(h, wci_ref[...])
    b_gate = _dot(h, wbg_ref[...])
    z_conv = _dot(h, wzc_ref[...])
    prev = carry_ref[...]
    row = lax.broadcasted_iota(jnp.int32, u.shape, 0)
    u1 = jnp.where(row == 0, prev[SUBLANES - 1:SUBLANES, :], pltpu.roll(u, 1, axis=0))
    u2 = jnp.where(row == 0, prev[SUBLANES - 2:SUBLANES - 1, :],
                   jnp.where(row == 1, prev[SUBLANES - 1:SUBLANES, :], pltpu.roll(u, 2, axis=0)))
    cw = cw_ref[...]
    conv = cw[0:1, :] * u2 + cw[1:2, :] * u1 + cw[2:3, :] * u
    o_ref[...] = (b_gate * conv * (z_conv * jax.nn.sigmoid(z_conv))).astype(BF16)
    carry_ref[...] = u[u.shape[0] - SUBLANES:, :]


def _conv(h, w_main, conv_w, tm=1024, tn=512):
    s, d = h.shape

    def wcol(offset):
        return pl.BlockSpec((d, tn), lambda j, i: (0, offset // tn + j))

    return pl.pallas_call(
        _conv_kernel,
        grid=(CONV_WIDTH // tn, s // tm),
        in_specs=[pl.BlockSpec((tm, d), lambda j, i: (i, 0)),
                  wcol(MAIN_C_IN), wcol(MAIN_B_GATE), wcol(MAIN_C_GATE), wcol(MAIN_Z_CONV),
                  pl.BlockSpec((CONV_K, tn), lambda j, i: (0, j))],
        out_specs=pl.BlockSpec((tm, tn), lambda j, i: (i, j)),
        out_shape=jax.ShapeDtypeStruct((s, CONV_WIDTH), BF16),
        scratch_shapes=[pltpu.VMEM((SUBLANES, tn), F32)],
        compiler_params=_params(("arbitrary", "arbitrary")),
        name="conv",
    )(h, w_main, w_main, w_main, w_main, conv_w)


def _attn_kernel(q_ref, k_ref, v_ref, o_ref, s0_ref, s1_ref, m_ref, acc_ref, *, tq):
    i = pl.program_id(1)
    q = q_ref[0]
    n_rep = tq // LANES

    def qk(j, s_ref):
        k = k_ref[0, pl.ds(pl.multiple_of(j * tq, tq), tq), :]
        s_ref[...] = lax.dot_general(q, k, (((1,), (1,)), ((), ())), preferred_element_type=F32)

    def softmax_pv(j, s_ref, masked):
        s = s_ref[...]
        if masked:
            row = lax.broadcasted_iota(jnp.int32, s.shape, 0)
            col = lax.broadcasted_iota(jnp.int32, s.shape, 1)
            s = jnp.where(col <= row, s, -jnp.inf)
        m_prev = m_ref[...]
        m_new = jnp.maximum(m_prev, jnp.max(s, axis=1, keepdims=True))
        alpha = jnp.exp2(m_prev - m_new)
        p = jnp.exp2(s - jnp.concatenate([m_new] * n_rep, axis=1)).astype(BF16)
        v = v_ref[0, pl.ds(pl.multiple_of(j * tq, tq), tq), :]
        acc_ref[...] = jnp.concatenate([alpha, alpha], axis=1) * acc_ref[...] + _dot(p, v)
        m_ref[...] = m_new

    m_ref[...] = jnp.full_like(m_ref, -jnp.inf)
    acc_ref[...] = jnp.zeros_like(acc_ref)
    qk(0, s0_ref)

    def pair(jj, carry):
        j0 = 2 * jj
        qk(j0 + 1, s1_ref)
        softmax_pv(j0, s0_ref, masked=False)
        qk(j0 + 2, s0_ref)
        softmax_pv(j0 + 1, s1_ref, masked=False)
        return carry

    lax.fori_loop(0, i // 2, pair, 0)

    @pl.when(i % 2 == 1)
    def _():
        qk(i, s1_ref)
        softmax_pv(i - 1, s0_ref, masked=False)
        softmax_pv(i, s1_ref, masked=True)

    @pl.when(i % 2 == 0)
    def _():
        softmax_pv(i, s0_ref, masked=True)

    acc = acc_ref[...]
    o_ref[...] = (acc[:, :V_HEAD_DIM] / acc[:, V_HEAD_DIM:]).astype(BF16)


def _attention(q, k, v, tq=512):
    nh, s, _ = q.shape
    return pl.pallas_call(
        functools.partial(_attn_kernel, tq=tq),
        grid=(nh, s // tq),
        in_specs=[pl.BlockSpec((1, tq, QK_PAD), lambda h, i: (h, i, 0)),
                  pl.BlockSpec((1, s, QK_PAD), lambda h, i: (h, 0, 0)),
                  pl.BlockSpec((1, s, V_PAD), lambda h, i: (h, 0, 0))],
        out_specs=pl.BlockSpec((tq, V_HEAD_DIM), lambda h, i: (i, h)),
        out_shape=jax.ShapeDtypeStruct((s, nh * V_HEAD_DIM), BF16),
        scratch_shapes=[pltpu.VMEM((tq, tq), F32), pltpu.VMEM((tq, tq), F32),
                        pltpu.VMEM((tq, LANES), F32), pltpu.VMEM((tq, V_PAD), F32)],
        compiler_params=_params(("parallel", "arbitrary")),
        name="attention",
    )(q, k, v)


def _output_kernel(x_ref, attn_ref, sz_ref, sgm_ref, co_ref, sgc_ref,
                   womla_ref, woconv_ref, wout_ref, g_ref, o_ref):
    a = (attn_ref[...].astype(F32) * sz_ref[...].astype(F32)).astype(BF16)
    y_mla = _dot(a, womla_ref[...])
    y_conv = _dot(co_ref[...], woconv_ref[...])
    merged = sgm_ref[...].astype(F32) * y_mla + sgc_ref[...].astype(F32) * y_conv
    out = _dot(merged.astype(BF16), wout_ref[...])
    o_ref[...] = x_ref[...] + _rms(out, g_ref[...])


def _output(x, attn, gates, co, womla, woconv, wout, g, tm=256):
    s, d = x.shape
    row = lambda c: pl.BlockSpec((tm, d), lambda i: (i, c))
    return pl.pallas_call(
        _output_kernel,
        grid=(s // tm,),
        in_specs=[row(0), row(0), row(0), row(1), row(0), row(2),
                  _resident(womla.shape), _resident(woconv.shape), _resident(wout.shape),
                  _resident(g.shape)],
        out_specs=row(0),
        out_shape=jax.ShapeDtypeStruct((s, d), F32),
        compiler_params=_params(("parallel",)),
        name="output",
    )(x, attn, gates, gates, co, gates, womla, woconv, wout, g)


def _swap_halves(r):
    half = r.shape[-1] // 2
    return jnp.concatenate([r[..., half:], r[..., :half]], axis=-1)


def kernel(x, positions, pre_norm_g, w_in, q_a_norm_g, w_q_b, kv_a_norm_g, w_kv_b, conv_w,
           w_o_mla, w_o_conv, w_out, post_norm_g):
    b, s, d = x.shape
    assert b == 1 and d == D_MODEL
    x2 = x[0]

    w_kr = w_in[:, LAT_IN - ROPE_DIM:LAT_IN]
    wlat = jnp.concatenate([w_in[:, :LAT_IN], _swap_halves(w_kr)], axis=1).astype(BF16)
    w_main = w_in[:, LAT_IN:].astype(BF16)
    wq3 = w_q_b.reshape(Q_LORA_RANK, N_HEADS, QK_NOPE_DIM + ROPE_DIM)
    wq_rope = wq3[..., QK_NOPE_DIM:]
    wq = jnp.concatenate([wq3[..., :QK_NOPE_DIM], wq_rope, _swap_halves(wq_rope)], axis=-1)
    wq = wq.reshape(Q_LORA_RANK, N_HEADS * QK_PAD).astype(BF16)
    wkv3 = w_kv_b.reshape(KV_LORA_RANK, N_HEADS, QK_NOPE_DIM + V_HEAD_DIM)
    wkv = jnp.concatenate([wkv3[..., :QK_NOPE_DIM].reshape(KV_LORA_RANK, -1),
                           wkv3[..., QK_NOPE_DIM:].reshape(KV_LORA_RANK, -1)], axis=1).astype(BF16)

    inv_freq = ROPE_THETA ** (-jnp.arange(0, ROPE_DIM, 2, dtype=F32) / ROPE_DIM)
    ang = positions[0].astype(F32)[:, None] * inv_freq
    cos, sin = jnp.cos(ang), jnp.sin(ang)
    zeros = jnp.zeros((s, LANES - ROPE_DIM), F32)
    cos_t = jnp.concatenate([cos, cos, zeros], axis=1)
    sin_t = jnp.concatenate([-sin, sin, zeros], axis=1)

    h = _prenorm(x2, pre_norm_g.reshape(1, d))
    q, k, v = _latent(h, wlat, q_a_norm_g.reshape(1, -1), kv_a_norm_g.reshape(1, -1), wq, wkv,
                      cos_t, sin_t)
    gates = _gates(h, w_main)
    co = _conv(h, w_main, conv_w)
    attn = _attention(q, k, v)
    out = _output(x2, attn, gates, co, w_o_mla.astype(BF16), w_o_conv.astype(BF16),
                  w_out.astype(BF16), post_norm_g.reshape(1, d))
    return out[None]
```

```python
import functools
import math

import jax
import jax.numpy as jnp
from jax import lax
from jax.experimental import pallas as pl
from jax.experimental.pallas import tpu as pltpu

D_MODEL = 2048
N_HEADS = 16
QK_NOPE_DIM = 128
ROPE_DIM = 64
V_HEAD_DIM = 128
Q_LORA_RANK = 512
KV_LORA_RANK = 512
MLA_WIDTH = N_HEADS * V_HEAD_DIM
CONV_WIDTH = D_MODEL
CONV_K = 3
ROPE_THETA = 10000.0
RMS_EPS = 1e-6

LANES = 128
SUBLANES = 8
QK_PAD = QK_NOPE_DIM + LANES
V_PAD = V_HEAD_DIM + LANES
LAT_WIDTH = Q_LORA_RANK + KV_LORA_RANK + LANES
LAT_IN = Q_LORA_RANK + KV_LORA_RANK + ROPE_DIM
MAIN_Z_MLA = 0
MAIN_C_IN = MAIN_Z_MLA + MLA_WIDTH
MAIN_B_GATE = MAIN_C_IN + CONV_WIDTH
MAIN_C_GATE = MAIN_B_GATE + CONV_WIDTH
MAIN_Z_CONV = MAIN_C_GATE + CONV_WIDTH
MAIN_G_MLA = MAIN_Z_CONV + CONV_WIDTH
MAIN_G_CONV = MAIN_G_MLA + D_MODEL
VMEM_LIMIT = 56 * 1024 * 1024

BF16 = jnp.bfloat16
F32 = jnp.float32


def _params(semantics):
    return pltpu.CompilerParams(dimension_semantics=semantics, vmem_limit_bytes=VMEM_LIMIT)


def _resident(shape):
    return pl.BlockSpec(shape, lambda *_: (0,) * len(shape), pipeline_mode=pl.Buffered(1))


def _rms(xf, g):
    r = lax.rsqrt(jnp.mean(xf * xf, axis=-1, keepdims=True) + RMS_EPS)
    return xf * r * g


def _dot(a, b):
    return jnp.dot(a, b, preferred_element_type=F32)


def _prenorm_kernel(x_ref, g_ref, h_ref):
    h_ref[...] = _rms(x_ref[...], g_ref[...]).astype(BF16)


def _prenorm(x, g, tm=512):
    s, d = x.shape
    return pl.pallas_call(
        _prenorm_kernel,
        grid=(s // tm,),
        in_specs=[pl.BlockSpec((tm, d), lambda i: (i, 0)), _resident((1, d))],
        out_specs=pl.BlockSpec((tm, d), lambda i: (i, 0)),
        out_shape=jax.ShapeDtypeStruct((s, d), BF16),
        compiler_params=_params(("parallel",)),
        name="prenorm",
    )(x, g)


def _rope128(g2, cos_t, sin_t):
    return g2 * cos_t + pltpu.roll(g2, ROPE_DIM, axis=1) * sin_t


def _latent_kernel(h_ref, wlat_ref, gq_ref, gkv_ref, wq_ref, wkv_ref, cos_ref, sin_ref,
                   q_ref, k_ref, v_ref, *, scale):
    lat = _dot(h_ref[...], wlat_ref[...])
    qa = _rms(lat[:, :Q_LORA_RANK], gq_ref[...]).astype(BF16)
    ckv = _rms(lat[:, Q_LORA_RANK:Q_LORA_RANK + KV_LORA_RANK], gkv_ref[...]).astype(BF16)
    cos_t = cos_ref[...]
    sin_t = sin_ref[...]
    k_rope = _rope128(lat[:, Q_LORA_RANK + KV_LORA_RANK:], cos_t, sin_t).astype(BF16)
    q = _dot(qa, wq_ref[...]) * scale
    kv = _dot(ckv, wkv_ref[...])
    for hd in range(N_HEADS):
        base = hd * QK_PAD
        q_ref[hd, :, :QK_NOPE_DIM] = q[:, base:base + QK_NOPE_DIM].astype(BF16)
        q_ref[hd, :, QK_NOPE_DIM:] = _rope128(q[:, base + QK_NOPE_DIM:base + QK_PAD],
                                              cos_t, sin_t).astype(BF16)
        k_ref[hd, :, :QK_NOPE_DIM] = kv[:, hd * QK_NOPE_DIM:(hd + 1) * QK_NOPE_DIM].astype(BF16)
        k_ref[hd, :, QK_NOPE_DIM:] = k_rope
        v_ref[hd, :, :V_HEAD_DIM] = kv[:, MLA_WIDTH + hd * V_HEAD_DIM:
                                       MLA_WIDTH + (hd + 1) * V_HEAD_DIM].astype(BF16)
        v_ref[hd, :, V_HEAD_DIM:] = jnp.ones((kv.shape[0], LANES), BF16)


def _latent(h, wlat, gq, gkv, wq, wkv, cos_t, sin_t, tm=256):
    s, d = h.shape
    scale = math.log2(math.e) / math.sqrt(QK_NOPE_DIM + ROPE_DIM)
    return pl.pallas_call(
        functools.partial(_latent_kernel, scale=scale),
        grid=(s // tm,),
        in_specs=[
            pl.BlockSpec((tm, d), lambda i: (i, 0)),
            _resident(wlat.shape), _resident(gq.shape), _resident(gkv.shape),
            _resident(wq.shape), _resident(wkv.shape),
            pl.BlockSpec((tm, LANES), lambda i: (i, 0)),
            pl.BlockSpec((tm, LANES), lambda i: (i, 0)),
        ],
        out_specs=[
            pl.BlockSpec((N_HEADS, tm, QK_PAD), lambda i: (0, i, 0)),
            pl.BlockSpec((N_HEADS, tm, QK_PAD), lambda i: (0, i, 0)),
            pl.BlockSpec((N_HEADS, tm, V_PAD), lambda i: (0, i, 0)),
        ],
        out_shape=[
            jax.ShapeDtypeStruct((N_HEADS, s, QK_PAD), BF16),
            jax.ShapeDtypeStruct((N_HEADS, s, QK_PAD), BF16),
            jax.ShapeDtypeStruct((N_HEADS, s, V_PAD), BF16),
        ],
        compiler_params=_params(("parallel",)),
        name="latent",
    )(h, wlat, gq, gkv, wq, wkv, cos_t, sin_t)


def _gates_kernel(h_ref, w_ref, o_ref, *, n_silu_blocks):
    r = _dot(h_ref[...], w_ref[...])
    sg = jax.nn.sigmoid(r)
    is_silu = pl.program_id(0) < n_silu_blocks
    o_ref[...] = jnp.where(is_silu, r * sg, sg).astype(BF16)


def _gates(h, w_main, tm=1024, tn=1024):
    s, d = h.shape
    n_silu_blocks = MLA_WIDTH // tn
    skip_blocks = (MAIN_G_MLA - MLA_WIDTH) // tn
    return pl.pallas_call(
        functools.partial(_gates_kernel, n_silu_blocks=n_silu_blocks),
        grid=(3 * D_MODEL // tn, s // tm),
        in_specs=[pl.BlockSpec((tm, d), lambda j, i: (i, 0)),
                  pl.BlockSpec((d, tn), lambda j, i: (0, jnp.where(j < n_silu_blocks, j, j + skip_blocks)))],
        out_specs=pl.BlockSpec((tm, tn), lambda j, i: (i, j)),
        out_shape=jax.ShapeDtypeStruct((s, 3 * D_MODEL), BF16),
        compiler_params=_params(("parallel", "parallel")),
        name="gates",
    )(h, w_main)


def _conv_kernel(h_ref, wci_ref, wbg_ref, wcg_ref, wzc_ref, cw_ref, o_ref, carry_ref):
    i = pl.program_id(1)

    @pl.when(i == 0)
    def _():
        carry_ref[...] = jnp.zeros_like(carry_ref)

    h = h_ref[...]
    u = _dot(h, wcg_ref[...]) * _dot(h, wci_ref[...])
    b_gate = _dot(h, wbg_ref[...])
    z_conv = _dot(h, wzc_ref[...])
    prev = carry_ref[...]
    row = lax.broadcasted_iota(jnp.int32, u.shape, 0)
    u1 = jnp.where(row == 0, prev[SUBLANES - 1:SUBLANES, :], pltpu.roll(u, 1, axis=0))
    u2 = jnp.where(row == 0, prev[SUBLANES - 2:SUBLANES - 1, :],
                   jnp.where(row == 1, prev[SUBLANES - 1:SUBLANES, :], pltpu.roll(u, 2, axis=0)))
    cw = cw_ref[...]
    conv = cw[0:1, :] * u2 + cw[1:2, :] * u1 + cw[2:3, :] * u
    o_ref[...] = (b_gate * conv * (z_conv * jax.nn.sigmoid(z_conv))).astype(BF16)
    carry_ref[...] = u[u.shape[0] - SUBLANES:, :]


def _conv(h, w_main, conv_w, tm=1024, tn=512):
    s, d = h.shape

    def wcol(offset):
        return pl.BlockSpec((d, tn), lambda j, i: (0, offset // tn + j))

    return pl.pallas_call(
        _conv_kernel,
        grid=(CONV_WIDTH // tn, s // tm),
        in_specs=[pl.BlockSpec((tm, d), lambda j, i: (i, 0)),
                  wcol(MAIN_C_IN), wcol(MAIN_B_GATE), wcol(MAIN_C_GATE), wcol(MAIN_Z_CONV),
                  pl.BlockSpec((CONV_K, tn), lambda j, i: (0, j))],
        out_specs=pl.BlockSpec((tm, tn), lambda j, i: (i, j)),
        out_shape=jax.ShapeDtypeStruct((s, CONV_WIDTH), BF16),
        scratch_shapes=[pltpu.VMEM((SUBLANES, tn), F32)],
        compiler_params=_params(("arbitrary", "arbitrary")),
        name="conv",
    )(h, w_main, w_main, w_main, w_main, conv_w)


def _attn_kernel(q_ref, k_ref, v_ref, o_ref, s0_ref, s1_ref, m_ref, acc_ref, *, tq):
    i = pl.program_id(1)
    tk = tq // 2
    n_rep = tk // LANES
    all_rows = slice(0, tq)
    low_rows = slice(tk, tq)

    def qk(j, s_ref, rows):
        k = k_ref[0, pl.ds(pl.multiple_of(j * tk, tk), tk), :]
        s_ref[rows, :] = lax.dot_general(q_ref[0, rows, :], k, (((1,), (1,)), ((), ())),
                                         preferred_element_type=F32)

    def softmax_pv(j, s_ref, rows, masked):
        s = s_ref[rows, :]
        if masked:
            row = lax.broadcasted_iota(jnp.int32, s.shape, 0)
            col = lax.broadcasted_iota(jnp.int32, s.shape, 1)
            s = jnp.where(col <= row, s, -jnp.inf)
        m_prev = m_ref[rows, :]
        m_new = jnp.maximum(m_prev, jnp.max(s, axis=1, keepdims=True))
        alpha = jnp.exp2(m_prev - m_new)
        p = jnp.exp2(s - jnp.concatenate([m_new] * n_rep, axis=1)).astype(BF16)
        v = v_ref[0, pl.ds(pl.multiple_of(j * tk, tk), tk), :]
        acc_ref[rows, :] = jnp.concatenate([alpha, alpha], axis=1) * acc_ref[rows, :] + _dot(p, v)
        m_ref[rows, :] = m_new

    m_ref[...] = jnp.full_like(m_ref, -jnp.inf)
    acc_ref[...] = jnp.zeros_like(acc_ref)
    qk(0, s0_ref, all_rows)

    def pair(jj, carry):
        j0 = 2 * jj
        qk(j0 + 1, s1_ref, all_rows)
        softmax_pv(j0, s0_ref, all_rows, masked=False)
        qk(j0 + 2, s0_ref, all_rows)
        softmax_pv(j0 + 1, s1_ref, all_rows, masked=False)
        return carry

    lax.fori_loop(0, i, pair, 0)
    qk(2 * i + 1, s1_ref, low_rows)
    softmax_pv(2 * i, s0_ref, all_rows, masked=True)
    softmax_pv(2 * i + 1, s1_ref, low_rows, masked=True)

    acc = acc_ref[...]
    o_ref[...] = (acc[:, :V_HEAD_DIM] / acc[:, V_HEAD_DIM:]).astype(BF16)


def _attention(q, k, v, tq=1024):
    nh, s, _ = q.shape
    return pl.pallas_call(
        functools.partial(_attn_kernel, tq=tq),
        grid=(nh, s // tq),
        in_specs=[pl.BlockSpec((1, tq, QK_PAD), lambda h, i: (h, i, 0)),
                  pl.BlockSpec((1, s, QK_PAD), lambda h, i: (h, 0, 0)),
                  pl.BlockSpec((1, s, V_PAD), lambda h, i: (h, 0, 0))],
        out_specs=pl.BlockSpec((tq, V_HEAD_DIM), lambda h, i: (i, h)),
        out_shape=jax.ShapeDtypeStruct((s, nh * V_HEAD_DIM), BF16),
        scratch_shapes=[pltpu.VMEM((tq, tq // 2), F32), pltpu.VMEM((tq, tq // 2), F32),
                        pltpu.VMEM((tq, LANES), F32), pltpu.VMEM((tq, V_PAD), F32)],
        compiler_params=_params(("parallel", "arbitrary")),
        name="attention",
    )(q, k, v)


def _output_kernel(x_ref, attn_ref, sz_ref, sgm_ref, co_ref, sgc_ref,
                   womla_ref, woconv_ref, wout_ref, g_ref, o_ref):
    a = (attn_ref[...].astype(F32) * sz_ref[...].astype(F32)).astype(BF16)
    y_mla = _dot(a, womla_ref[...])
    y_conv = _dot(co_ref[...], woconv_ref[...])
    merged = sgm_ref[...].astype(F32) * y_mla + sgc_ref[...].astype(F32) * y_conv
    out = _dot(merged.astype(BF16), wout_ref[...])
    o_ref[...] = x_ref[...] + _rms(out, g_ref[...])


def _output(x, attn, gates, co, womla, woconv, wout, g, tm=256):
    s, d = x.shape
    row = lambda c: pl.BlockSpec((tm, d), lambda i: (i, c))
    return pl.pallas_call(
        _output_kernel,
        grid=(s // tm,),
        in_specs=[row(0), row(0), row(0), row(1), row(0), row(2),
                  _resident(womla.shape), _resident(woconv.shape), _resident(wout.shape),
                  _resident(g.shape)],
        out_specs=row(0),
        out_shape=jax.ShapeDtypeStruct((s, d), F32),
        compiler_params=_params(("parallel",)),
        name="output",
    )(x, attn, gates, gates, co, gates, womla, woconv, wout, g)


def _swap_halves(r):
    half = r.shape[-1] // 2
    return jnp.concatenate([r[..., half:], r[..., :half]], axis=-1)


def kernel(x, positions, pre_norm_g, w_in, q_a_norm_g, w_q_b, kv_a_norm_g, w_kv_b, conv_w,
           w_o_mla, w_o_conv, w_out, post_norm_g):
    b, s, d = x.shape
    assert b == 1 and d == D_MODEL
    x2 = x[0]

    w_kr = w_in[:, LAT_IN - ROPE_DIM:LAT_IN]
    wlat = jnp.concatenate([w_in[:, :LAT_IN], _swap_halves(w_kr)], axis=1).astype(BF16)
    w_main = w_in[:, LAT_IN:].astype(BF16)
    wq3 = w_q_b.reshape(Q_LORA_RANK, N_HEADS, QK_NOPE_DIM + ROPE_DIM)
    wq_rope = wq3[..., QK_NOPE_DIM:]
    wq = jnp.concatenate([wq3[..., :QK_NOPE_DIM], wq_rope, _swap_halves(wq_rope)], axis=-1)
    wq = wq.reshape(Q_LORA_RANK, N_HEADS * QK_PAD).astype(BF16)
    wkv3 = w_kv_b.reshape(KV_LORA_RANK, N_HEADS, QK_NOPE_DIM + V_HEAD_DIM)
    wkv = jnp.concatenate([wkv3[..., :QK_NOPE_DIM].reshape(KV_LORA_RANK, -1),
                           wkv3[..., QK_NOPE_DIM:].reshape(KV_LORA_RANK, -1)], axis=1).astype(BF16)

    inv_freq = ROPE_THETA ** (-jnp.arange(0, ROPE_DIM, 2, dtype=F32) / ROPE_DIM)
    ang = positions[0].astype(F32)[:, None] * inv_freq
    cos, sin = jnp.cos(ang), jnp.sin(ang)
    zeros = jnp.zeros((s, LANES - ROPE_DIM), F32)
    cos_t = jnp.concatenate([cos, cos, zeros], axis=1)
    sin_t = jnp.concatenate([-sin, sin, zeros], axis=1)

    h = _prenorm(x2, pre_norm_g.reshape(1, d))
    q, k, v = _latent(h, wlat, q_a_norm_g.reshape(1, -1), kv_a_norm_g.reshape(1, -1), wq, wkv,
                      cos_t, sin_t)
    gates = _gates(h, w_main)
    co = _conv(h, w_main, conv_w)
    attn = _attention(q, k, v)
    out = _output(x2, attn, gates, co, w_o_mla.astype(BF16), w_o_conv.astype(BF16),
                  w_out.astype(BF16), post_norm_g.reshape(1, d))
    return out[None]
```

```python
import functools
import math

import jax
import jax.numpy as jnp
from jax import lax
from jax.experimental import pallas as pl
from jax.experimental.pallas import tpu as pltpu

D_MODEL = 2048
N_HEADS = 16
QK_NOPE_DIM = 128
ROPE_DIM = 64
V_HEAD_DIM = 128
Q_LORA_RANK = 512
KV_LORA_RANK = 512
MLA_WIDTH = N_HEADS * V_HEAD_DIM
CONV_WIDTH = D_MODEL
CONV_K = 3
ROPE_THETA = 10000.0
RMS_EPS = 1e-6

LANES = 128
SUBLANES = 8
QK_PAD = QK_NOPE_DIM + LANES
BF16_SUBLANES = 16
VT_PAD = V_HEAD_DIM + BF16_SUBLANES
ATTN_TK = 512
ATTN_TQ = 2 * ATTN_TK
HEAD_GROUP = 4
LAT_WIDTH = Q_LORA_RANK + KV_LORA_RANK + LANES
LAT_IN = Q_LORA_RANK + KV_LORA_RANK + ROPE_DIM
MAIN_Z_MLA = 0
MAIN_C_IN = MAIN_Z_MLA + MLA_WIDTH
MAIN_B_GATE = MAIN_C_IN + CONV_WIDTH
MAIN_C_GATE = MAIN_B_GATE + CONV_WIDTH
MAIN_Z_CONV = MAIN_C_GATE + CONV_WIDTH
MAIN_G_MLA = MAIN_Z_CONV + CONV_WIDTH
MAIN_G_CONV = MAIN_G_MLA + D_MODEL
VMEM_LIMIT = 56 * 1024 * 1024

BF16 = jnp.bfloat16
F32 = jnp.float32


def _params(semantics):
    return pltpu.CompilerParams(dimension_semantics=semantics, vmem_limit_bytes=VMEM_LIMIT)


def _resident(shape):
    return pl.BlockSpec(shape, lambda *_: (0,) * len(shape), pipeline_mode=pl.Buffered(1))


def _rms(xf, g):
    r = lax.rsqrt(jnp.mean(xf * xf, axis=-1, keepdims=True) + RMS_EPS)
    return xf * r * g


def _dot(a, b):
    return jnp.dot(a, b, preferred_element_type=F32)


def _prenorm_kernel(x_ref, g_ref, h_ref):
    h_ref[...] = _rms(x_ref[...], g_ref[...]).astype(BF16)


def _prenorm(x, g, tm=512):
    s, d = x.shape
    return pl.pallas_call(
        _prenorm_kernel,
        grid=(s // tm,),
        in_specs=[pl.BlockSpec((tm, d), lambda i: (i, 0)), _resident((1, d))],
        out_specs=pl.BlockSpec((tm, d), lambda i: (i, 0)),
        out_shape=jax.ShapeDtypeStruct((s, d), BF16),
        compiler_params=_params(("parallel",)),
        name="prenorm",
    )(x, g)


def _rope128(g2, cos_t, sin_t):
    return g2 * cos_t + pltpu.roll(g2, ROPE_DIM, axis=1) * sin_t


def _latent_kernel(h_ref, wlat_ref, gq_ref, gkv_ref, wq_ref, wk_ref, wvt_ref, cos_ref, sin_ref,
                   q_ref, k_ref, vt_ref, *, scale):
    lat = _dot(h_ref[...], wlat_ref[...])
    qa = _rms(lat[:, :Q_LORA_RANK], gq_ref[...]).astype(BF16)
    ckv = _rms(lat[:, Q_LORA_RANK:Q_LORA_RANK + KV_LORA_RANK], gkv_ref[...]).astype(BF16)
    cos_t = cos_ref[...]
    sin_t = sin_ref[...]
    k_rope = _rope128(lat[:, Q_LORA_RANK + KV_LORA_RANK:], cos_t, sin_t).astype(BF16)
    tm = qa.shape[0]
    vt = lax.dot_general(wvt_ref[...], ckv, (((1,), (1,)), ((), ())), preferred_element_type=F32)
    ones_rows = jnp.ones((VT_PAD - V_HEAD_DIM, tm), BF16)
    for grp in range(N_HEADS // HEAD_GROUP):
        q = _dot(qa, wq_ref[:, grp * HEAD_GROUP * QK_PAD:(grp + 1) * HEAD_GROUP * QK_PAD]) * scale
        kn = _dot(ckv, wk_ref[:, grp * HEAD_GROUP * QK_NOPE_DIM:(grp + 1) * HEAD_GROUP * QK_NOPE_DIM])
        for sub in range(HEAD_GROUP):
            hd = grp * HEAD_GROUP + sub
            base = sub * QK_PAD
            q_ref[hd, :, :QK_NOPE_DIM] = q[:, base:base + QK_NOPE_DIM].astype(BF16)
            q_ref[hd, :, QK_NOPE_DIM:] = _rope128(q[:, base + QK_NOPE_DIM:base + QK_PAD],
                                                  cos_t, sin_t).astype(BF16)
            k_ref[hd, :, :QK_NOPE_DIM] = kn[:, sub * QK_NOPE_DIM:(sub + 1) * QK_NOPE_DIM].astype(BF16)
            k_ref[hd, :, QK_NOPE_DIM:] = k_rope
            vt_ref[hd, 0, :V_HEAD_DIM, :] = vt[hd * V_HEAD_DIM:(hd + 1) * V_HEAD_DIM, :].astype(BF16)
            vt_ref[hd, 0, V_HEAD_DIM:, :] = ones_rows


def _latent(h, wlat, gq, gkv, wq, wk, wvt, cos_t, sin_t, tm=ATTN_TK):
    s, d = h.shape
    scale = math.log2(math.e) / math.sqrt(QK_NOPE_DIM + ROPE_DIM)
    return pl.pallas_call(
        functools.partial(_latent_kernel, scale=scale),
        grid=(s // tm,),
        in_specs=[
            pl.BlockSpec((tm, d), lambda i: (i, 0)),
            _resident(wlat.shape), _resident(gq.shape), _resident(gkv.shape),
            _resident(wq.shape), _resident(wk.shape), _resident(wvt.shape),
            pl.BlockSpec((tm, LANES), lambda i: (i, 0)),
            pl.BlockSpec((tm, LANES), lambda i: (i, 0)),
        ],
        out_specs=[
            pl.BlockSpec((N_HEADS, tm, QK_PAD), lambda i: (0, i, 0)),
            pl.BlockSpec((N_HEADS, tm, QK_PAD), lambda i: (0, i, 0)),
            pl.BlockSpec((N_HEADS, 1, VT_PAD, tm), lambda i: (0, i, 0, 0)),
        ],
        out_shape=[
            jax.ShapeDtypeStruct((N_HEADS, s, QK_PAD), BF16),
            jax.ShapeDtypeStruct((N_HEADS, s, QK_PAD), BF16),
            jax.ShapeDtypeStruct((N_HEADS, s // tm, VT_PAD, tm), BF16),
        ],
        compiler_params=_params(("parallel",)),
        name="latent",
    )(h, wlat, gq, gkv, wq, wk, wvt, cos_t, sin_t)


def _gates_kernel(h_ref, w_ref, o_ref, *, n_silu_blocks):
    r = _dot(h_ref[...], w_ref[...])
    sg = jax.nn.sigmoid(r)
    is_silu = pl.program_id(0) < n_silu_blocks
    o_ref[...] = jnp.where(is_silu, r * sg, sg).astype(BF16)


def _gates(h, w_main, tm=1024, tn=1024):
    s, d = h.shape
    n_silu_blocks = MLA_WIDTH // tn
    skip_blocks = (MAIN_G_MLA - MLA_WIDTH) // tn
    return pl.pallas_call(
        functools.partial(_gates_kernel, n_silu_blocks=n_silu_blocks),
        grid=(3 * D_MODEL // tn, s // tm),
        in_specs=[pl.BlockSpec((tm, d), lambda j, i: (i, 0)),
                  pl.BlockSpec((d, tn), lambda j, i: (0, jnp.where(j < n_silu_blocks, j, j + skip_blocks)))],
        out_specs=pl.BlockSpec((tm, tn), lambda j, i: (i, j)),
        out_shape=jax.ShapeDtypeStruct((s, 3 * D_MODEL), BF16),
        compiler_params=_params(("parallel", "parallel")),
        name="gates",
    )(h, w_main)


def _conv_kernel(h_ref, wci_ref, wbg_ref, wcg_ref, wzc_ref, cw_ref, o_ref, carry_ref):
    i = pl.program_id(1)

    @pl.when(i == 0)
    def _():
        carry_ref[...] = jnp.zeros_like(carry_ref)

    h = h_ref[...]
    u = _dot(h, wcg_ref[...]) * _dot(h, wci_ref[...])
    b_gate = _dot(h, wbg_ref[...])
    z_conv = _dot(h, wzc_ref[...])
    prev = carry_ref[...]
    row = lax.broadcasted_iota(jnp.int32, u.shape, 0)
    u1 = jnp.where(row == 0, prev[SUBLANES - 1:SUBLANES, :], pltpu.roll(u, 1, axis=0))
    u2 = jnp.where(row == 0, prev[SUBLANES - 2:SUBLANES - 1, :],
                   jnp.where(row == 1, prev[SUBLANES - 1:SUBLANES, :], pltpu.roll(u, 2, axis=0)))
    cw = cw_ref[...]
    conv = cw[0:1, :] * u2 + cw[1:2, :] * u1 + cw[2:3, :] * u
    o_ref[...] = (b_gate * conv * (z_conv * jax.nn.sigmoid(z_conv))).astype(BF16)
    carry_ref[...] = u[u.shape[0] - SUBLANES:, :]


def _conv(h, w_main, conv_w, tm=1024, tn=512):
    s, d = h.shape

    def wcol(offset):
        return pl.BlockSpec((d, tn), lambda j, i: (0, offset // tn + j))

    return pl.pallas_call(
        _conv_kernel,
        grid=(CONV_WIDTH // tn, s // tm),
        in_specs=[pl.BlockSpec((tm, d), lambda j, i: (i, 0)),
                  wcol(MAIN_C_IN), wcol(MAIN_B_GATE), wcol(MAIN_C_GATE), wcol(MAIN_Z_CONV),
                  pl.BlockSpec((CONV_K, tn), lambda j, i: (0, j))],
        out_specs=pl.BlockSpec((tm, tn), lambda j, i: (i, j)),
        out_shape=jax.ShapeDtypeStruct((s, CONV_WIDTH), BF16),
        scratch_shapes=[pltpu.VMEM((SUBLANES, tn), F32)],
        compiler_params=_params(("arbitrary", "arbitrary")),
        name="conv",
    )(h, w_main, w_main, w_main, w_main, conv_w)


def _attn_kernel(q_ref, k_ref, vt_ref, o_ref, s0_ref, s1_ref, m_ref, acc_ref):
    i = pl.program_id(1)
    tk, tq = ATTN_TK, ATTN_TQ
    all_q = slice(0, tq)
    late_q = slice(tk, tq)

    def qk(j, s_ref, qs):
        k = k_ref[0, pl.ds(pl.multiple_of(j * tk, tk), tk), :]
        s_ref[:, qs] = lax.dot_general(k, q_ref[0, qs, :], (((1,), (1,)), ((), ())),
                                       preferred_element_type=F32)

    def softmax_pv(j, s_ref, qs, masked):
        s = s_ref[:, qs]
        if masked:
            key = lax.broadcasted_iota(jnp.int32, s.shape, 0)
            qry = lax.broadcasted_iota(jnp.int32, s.shape, 1)
            s = jnp.where(key <= qry, s, -jnp.inf)
        m_prev = m_ref[:, qs]
        m_new = jnp.maximum(m_prev, jnp.max(s, axis=0, keepdims=True))
        alpha = jnp.exp2(m_prev - m_new)
        p = jnp.exp2(s - m_new).astype(BF16)
        acc_ref[:, qs] = alpha * acc_ref[:, qs] + _dot(vt_ref[0, j], p)
        m_ref[:, qs] = m_new

    m_ref[...] = jnp.full_like(m_ref, -jnp.inf)
    acc_ref[...] = jnp.zeros_like(acc_ref)
    qk(0, s0_ref, all_q)

    def pair(jj, carry):
        j0 = 2 * jj
        qk(j0 + 1, s1_ref, all_q)
        softmax_pv(j0, s0_ref, all_q, masked=False)
        qk(j0 + 2, s0_ref, all_q)
        softmax_pv(j0 + 1, s1_ref, all_q, masked=False)
        return carry

    lax.fori_loop(0, i, pair, 0)
    qk(2 * i + 1, s1_ref, late_q)
    softmax_pv(2 * i, s0_ref, all_q, masked=True)
    softmax_pv(2 * i + 1, s1_ref, late_q, masked=True)

    acc = acc_ref[...]
    out_t = acc[:V_HEAD_DIM, :] / acc[V_HEAD_DIM:V_HEAD_DIM + 1, :]
    o_ref[...] = out_t.T.astype(BF16)


def _attention(q, k, vt):
    nh, s, _ = q.shape
    tk, tq = ATTN_TK, ATTN_TQ
    return pl.pallas_call(
        _attn_kernel,
        grid=(nh, s // tq),
        in_specs=[pl.BlockSpec((1, tq, QK_PAD), lambda h, i: (h, i, 0)),
                  pl.BlockSpec((1, s, QK_PAD), lambda h, i: (h, 0, 0)),
                  pl.BlockSpec((1, s // tk, VT_PAD, tk), lambda h, i: (h, 0, 0, 0))],
        out_specs=pl.BlockSpec((tq, V_HEAD_DIM), lambda h, i: (i, h)),
        out_shape=jax.ShapeDtypeStruct((s, nh * V_HEAD_DIM), BF16),
        scratch_shapes=[pltpu.VMEM((tk, tq), F32), pltpu.VMEM((tk, tq), F32),
                        pltpu.VMEM((1, tq), F32), pltpu.VMEM((VT_PAD, tq), F32)],
        compiler_params=_params(("parallel", "arbitrary")),
        name="attention",
    )(q, k, vt)


def _output_kernel(x_ref, attn_ref, sz_ref, sgm_ref, co_ref, sgc_ref,
                   womla_ref, woconv_ref, wout_ref, g_ref, o_ref):
    a = (attn_ref[...].astype(F32) * sz_ref[...].astype(F32)).astype(BF16)
    y_mla = _dot(a, womla_ref[...])
    y_conv = _dot(co_ref[...], woconv_ref[...])
    merged = sgm_ref[...].astype(F32) * y_mla + sgc_ref[...].astype(F32) * y_conv
    out = _dot(merged.astype(BF16), wout_ref[...])
    o_ref[...] = x_ref[...] + _rms(out, g_ref[...])


def _output(x, attn, gates, co, womla, woconv, wout, g, tm=256):
    s, d = x.shape
    row = lambda c: pl.BlockSpec((tm, d), lambda i: (i, c))
    return pl.pallas_call(
        _output_kernel,
        grid=(s // tm,),
        in_specs=[row(0), row(0), row(0), row(1), row(0), row(2),
                  _resident(womla.shape), _resident(woconv.shape), _resident(wout.shape),
                  _resident(g.shape)],
        out_specs=row(0),
        out_shape=jax.ShapeDtypeStruct((s, d), F32),
        compiler_params=_params(("parallel",)),
        name="output",
    )(x, attn, gates, gates, co, gates, womla, woconv, wout, g)


def _swap_halves(r):
    half = r.shape[-1] // 2
    return jnp.concatenate([r[..., half:], r[..., :half]], axis=-1)


def kernel(x, positions, pre_norm_g, w_in, q_a_norm_g, w_q_b, kv_a_norm_g, w_kv_b, conv_w,
           w_o_mla, w_o_conv, w_out, post_norm_g):
    b, s, d = x.shape
    assert b == 1 and d == D_MODEL
    x2 = x[0]

    w_kr = w_in[:, LAT_IN - ROPE_DIM:LAT_IN]
    wlat = jnp.concatenate([w_in[:, :LAT_IN], _swap_halves(w_kr)], axis=1).astype(BF16)
    w_main = w_in[:, LAT_IN:].astype(BF16)
    wq3 = w_q_b.reshape(Q_LORA_RANK, N_HEADS, QK_NOPE_DIM + ROPE_DIM)
    wq_rope = wq3[..., QK_NOPE_DIM:]
    wq = jnp.concatenate([wq3[..., :QK_NOPE_DIM], wq_rope, _swap_halves(wq_rope)], axis=-1)
    wq = wq.reshape(Q_LORA_RANK, N_HEADS * QK_PAD).astype(BF16)
    wkv3 = w_kv_b.reshape(KV_LORA_RANK, N_HEADS, QK_NOPE_DIM + V_HEAD_DIM)
    wk = wkv3[..., :QK_NOPE_DIM].reshape(KV_LORA_RANK, -1).astype(BF16)
    wvt = wkv3[..., QK_NOPE_DIM:].reshape(KV_LORA_RANK, -1).T.astype(BF16)

    inv_freq = ROPE_THETA ** (-jnp.arange(0, ROPE_DIM, 2, dtype=F32) / ROPE_DIM)
    ang = positions[0].astype(F32)[:, None] * inv_freq
    cos, sin = jnp.cos(ang), jnp.sin(ang)
    zeros = jnp.zeros((s, LANES - ROPE_DIM), F32)
    cos_t = jnp.concatenate([cos, cos, zeros], axis=1)
    sin_t = jnp.concatenate([-sin, sin, zeros], axis=1)

    h = _prenorm(x2, pre_norm_g.reshape(1, d))
    q, k, vt = _latent(h, wlat, q_a_norm_g.reshape(1, -1), kv_a_norm_g.reshape(1, -1), wq, wk, wvt,
                       cos_t, sin_t)
    gates = _gates(h, w_main)
    co = _conv(h, w_main, conv_w)
    attn = _attention(q, k, vt)
    out = _output(x2, attn, gates, co, w_o_mla.astype(BF16), w_o_conv.astype(BF16),
                  w_out.astype(BF16), post_norm_g.reshape(1, d))
    return out[None]
```

```python
import functools
import math

import jax
import jax.numpy as jnp
from jax import lax
from jax.experimental import pallas as pl
from jax.experimental.pallas import tpu as pltpu

D_MODEL = 2048
N_HEADS = 16
QK_NOPE_DIM = 128
ROPE_DIM = 64
V_HEAD_DIM = 128
Q_LORA_RANK = 512
KV_LORA_RANK = 512
MLA_WIDTH = N_HEADS * V_HEAD_DIM
CONV_WIDTH = D_MODEL
CONV_K = 3
ROPE_THETA = 10000.0
RMS_EPS = 1e-6

LANES = 128
SUBLANES = 8
QK_PAD = QK_NOPE_DIM + LANES
V_PAD = V_HEAD_DIM + LANES
LAT_WIDTH = Q_LORA_RANK + KV_LORA_RANK + LANES
LAT_IN = Q_LORA_RANK + KV_LORA_RANK + ROPE_DIM
IN_Z_MLA = LAT_IN
IN_C_IN = IN_Z_MLA + MLA_WIDTH
IN_B_GATE = IN_C_IN + CONV_WIDTH
IN_C_GATE = IN_B_GATE + CONV_WIDTH
IN_Z_CONV = IN_C_GATE + CONV_WIDTH
IN_G_MLA = IN_Z_CONV + CONV_WIDTH
IN_G_CONV = IN_G_MLA + D_MODEL
SEG_SHIFT = LAT_IN % LANES
REALIGN_ROWS = 256
VMEM_LIMIT = 56 * 1024 * 1024

BF16 = jnp.bfloat16
F32 = jnp.float32


def _params(semantics):
    return pltpu.CompilerParams(dimension_semantics=semantics, vmem_limit_bytes=VMEM_LIMIT)


def _resident(shape):
    return pl.BlockSpec(shape, lambda *_: (0,) * len(shape), pipeline_mode=pl.Buffered(1))


def _rms(xf, g):
    r = lax.rsqrt(jnp.mean(xf * xf, axis=-1, keepdims=True) + RMS_EPS)
    return xf * r * g


def _dot(a, b):
    return jnp.dot(a, b, preferred_element_type=F32)


def _segment_specs(d, tn, block_of_step):
    wide = pl.BlockSpec((d, tn), lambda j, i: (0, block_of_step(j)))
    tail = pl.BlockSpec((d, LANES), lambda j, i: (0, (block_of_step(j) + 1) * (tn // LANES)))
    return [wide, tail]


def _realign_into(w_sc, wide_ref, tail_ref):
    d, tn = w_sc.shape

    def chunk(c, carry):
        rows = pl.ds(pl.multiple_of(c * REALIGN_ROWS, REALIGN_ROWS), REALIGN_ROWS)
        both = jnp.concatenate([wide_ref[rows, :], tail_ref[rows, :]], axis=1)
        w_sc[rows, :] = pltpu.roll(both, tn + LANES - SEG_SHIFT, axis=1)[:, :tn].astype(BF16)
        return carry

    lax.fori_loop(0, d // REALIGN_ROWS, chunk, 0)


def _prenorm_kernel(x_ref, g_ref, h_ref):
    h_ref[...] = _rms(x_ref[...], g_ref[...]).astype(BF16)


def _prenorm(x, g, tm=512):
    s, d = x.shape
    return pl.pallas_call(
        _prenorm_kernel,
        grid=(s // tm,),
        in_specs=[pl.BlockSpec((tm, d), lambda i: (i, 0)), _resident((1, d))],
        out_specs=pl.BlockSpec((tm, d), lambda i: (i, 0)),
        out_shape=jax.ShapeDtypeStruct((s, d), BF16),
        compiler_params=_params(("parallel",)),
        name="prenorm",
    )(x, g)


def _rope128(g2, cos_t, sin_t):
    return g2 * cos_t + pltpu.roll(g2, ROPE_DIM, axis=1) * sin_t


def _latent_kernel(h_ref, wlat_ref, gq_ref, gkv_ref, wq_ref, wkv_ref, cos_ref, sin_ref,
                   q_ref, k_ref, v_ref, *, scale):
    lat = _dot(h_ref[...], wlat_ref[...])
    qa = _rms(lat[:, :Q_LORA_RANK], gq_ref[...]).astype(BF16)
    ckv = _rms(lat[:, Q_LORA_RANK:Q_LORA_RANK + KV_LORA_RANK], gkv_ref[...]).astype(BF16)
    cos_t = cos_ref[...]
    sin_t = sin_ref[...]
    k_rope = _rope128(lat[:, Q_LORA_RANK + KV_LORA_RANK:], cos_t, sin_t).astype(BF16)
    q = _dot(qa, wq_ref[...]) * scale
    kv = _dot(ckv, wkv_ref[...])
    for hd in range(N_HEADS):
        base = hd * QK_PAD
        q_ref[hd, :, :QK_NOPE_DIM] = q[:, base:base + QK_NOPE_DIM].astype(BF16)
        q_ref[hd, :, QK_NOPE_DIM:] = _rope128(q[:, base + QK_NOPE_DIM:base + QK_PAD],
                                              cos_t, sin_t).astype(BF16)
        k_ref[hd, :, :QK_NOPE_DIM] = kv[:, hd * QK_NOPE_DIM:(hd + 1) * QK_NOPE_DIM].astype(BF16)
        k_ref[hd, :, QK_NOPE_DIM:] = k_rope
        v_ref[hd, :, :V_HEAD_DIM] = kv[:, MLA_WIDTH + hd * V_HEAD_DIM:
                                       MLA_WIDTH + (hd + 1) * V_HEAD_DIM].astype(BF16)
        v_ref[hd, :, V_HEAD_DIM:] = jnp.ones((kv.shape[0], LANES), BF16)


def _latent(h, wlat, gq, gkv, wq, wkv, cos_t, sin_t, tm=256):
    s, d = h.shape
    scale = math.log2(math.e) / math.sqrt(QK_NOPE_DIM + ROPE_DIM)
    return pl.pallas_call(
        functools.partial(_latent_kernel, scale=scale),
        grid=(s // tm,),
        in_specs=[
            pl.BlockSpec((tm, d), lambda i: (i, 0)),
            _resident(wlat.shape), _resident(gq.shape), _resident(gkv.shape),
            _resident(wq.shape), _resident(wkv.shape),
            pl.BlockSpec((tm, LANES), lambda i: (i, 0)),
            pl.BlockSpec((tm, LANES), lambda i: (i, 0)),
        ],
        out_specs=[
            pl.BlockSpec((N_HEADS, tm, QK_PAD), lambda i: (0, i, 0)),
            pl.BlockSpec((N_HEADS, tm, QK_PAD), lambda i: (0, i, 0)),
            pl.BlockSpec((N_HEADS, tm, V_PAD), lambda i: (0, i, 0)),
        ],
        out_shape=[
            jax.ShapeDtypeStruct((N_HEADS, s, QK_PAD), BF16),
            jax.ShapeDtypeStruct((N_HEADS, s, QK_PAD), BF16),
            jax.ShapeDtypeStruct((N_HEADS, s, V_PAD), BF16),
        ],
        compiler_params=_params(("parallel",)),
        name="latent",
    )(h, wlat, gq, gkv, wq, wkv, cos_t, sin_t)


def _gates_kernel(h_ref, wide_ref, tail_ref, o_ref, w_sc, *, n_silu_blocks):
    @pl.when(pl.program_id(1) == 0)
    def _():
        _realign_into(w_sc, wide_ref, tail_ref)

    r = _dot(h_ref[...], w_sc[...])
    sg = jax.nn.sigmoid(r)
    is_silu = pl.program_id(0) < n_silu_blocks
    o_ref[...] = jnp.where(is_silu, r * sg, sg).astype(BF16)


def _gates(h, w_in, tm=1024, tn=1024):
    s, d = h.shape
    n_silu_blocks = MLA_WIDTH // tn
    z_block = (IN_Z_MLA - SEG_SHIFT) // tn
    g_block = (IN_G_MLA - SEG_SHIFT) // tn - n_silu_blocks
    return pl.pallas_call(
        functools.partial(_gates_kernel, n_silu_blocks=n_silu_blocks),
        grid=(3 * D_MODEL // tn, s // tm),
        in_specs=[pl.BlockSpec((tm, d), lambda j, i: (i, 0))]
        + _segment_specs(d, tn, lambda j: j + jnp.where(j < n_silu_blocks, z_block, g_block)),
        out_specs=pl.BlockSpec((tm, tn), lambda j, i: (i, j)),
        out_shape=jax.ShapeDtypeStruct((s, 3 * D_MODEL), BF16),
        scratch_shapes=[pltpu.VMEM((d, tn), BF16)],
        compiler_params=_params(("arbitrary", "arbitrary")),
        name="gates",
    )(h, w_in, w_in)


def _conv_kernel(h_ref, ci_wide, ci_tail, bg_wide, bg_tail, cg_wide, cg_tail, zc_wide, zc_tail,
                 cw_ref, o_ref, wci_sc, wbg_sc, wcg_sc, wzc_sc, carry_ref):
    i = pl.program_id(1)

    @pl.when(i == 0)
    def _():
        carry_ref[...] = jnp.zeros_like(carry_ref)
        _realign_into(wci_sc, ci_wide, ci_tail)
        _realign_into(wbg_sc, bg_wide, bg_tail)
        _realign_into(wcg_sc, cg_wide, cg_tail)
        _realign_into(wzc_sc, zc_wide, zc_tail)

    h = h_ref[...]
    u = _dot(h, wcg_sc[...]) * _dot(h, wci_sc[...])
    b_gate = _dot(h, wbg_sc[...])
    z_conv = _dot(h, wzc_sc[...])
    prev = carry_ref[...]
    row = lax.broadcasted_iota(jnp.int32, u.shape, 0)
    u1 = jnp.where(row == 0, prev[SUBLANES - 1:SUBLANES, :], pltpu.roll(u, 1, axis=0))
    u2 = jnp.where(row == 0, prev[SUBLANES - 2:SUBLANES - 1, :],
                   jnp.where(row == 1, prev[SUBLANES - 1:SUBLANES, :], pltpu.roll(u, 2, axis=0)))
    cw = cw_ref[...]
    conv = cw[0:1, :] * u2 + cw[1:2, :] * u1 + cw[2:3, :] * u
    o_ref[...] = (b_gate * conv * (z_conv * jax.nn.sigmoid(z_conv))).astype(BF16)
    carry_ref[...] = u[u.shape[0] - SUBLANES:, :]


def _conv(h, w_in, conv_w, tm=1024, tn=256):
    s, d = h.shape
    seg_specs = []
    for start in (IN_C_IN, IN_B_GATE, IN_C_GATE, IN_Z_CONV):
        first = (start - SEG_SHIFT) // tn
        seg_specs += _segment_specs(d, tn, lambda j, first=first: first + j)
    return pl.pallas_call(
        _conv_kernel,
        grid=(CONV_WIDTH // tn, s // tm),
        in_specs=[pl.BlockSpec((tm, d), lambda j, i: (i, 0))] + seg_specs
        + [pl.BlockSpec((CONV_K, tn), lambda j, i: (0, j))],
        out_specs=pl.BlockSpec((tm, tn), lambda j, i: (i, j)),
        out_shape=jax.ShapeDtypeStruct((s, CONV_WIDTH), BF16),
        scratch_shapes=[pltpu.VMEM((d, tn), BF16)] * 4 + [pltpu.VMEM((SUBLANES, tn), F32)],
        compiler_params=_params(("arbitrary", "arbitrary")),
        name="conv",
    )(h, *([w_in] * 8), conv_w)


def _attn_kernel(q_ref, k_ref, v_ref, o_ref, s0_ref, s1_ref, m_ref, acc_ref, *, tq):
    i = pl.program_id(1)
    tk = tq // 2
    n_rep = tk // LANES
    all_rows = slice(0, tq)
    low_rows = slice(tk, tq)

    def qk(j, s_ref, rows):
        k = k_ref[0, pl.ds(pl.multiple_of(j * tk, tk), tk), :]
        s_ref[rows, :] = lax.dot_general(q_ref[0, rows, :], k, (((1,), (1,)), ((), ())),
                                         preferred_element_type=F32)

    def softmax_pv(j, s_ref, rows, masked):
        s = s_ref[rows, :]
        if masked:
            row = lax.broadcasted_iota(jnp.int32, s.shape, 0)
            col = lax.broadcasted_iota(jnp.int32, s.shape, 1)
            s = jnp.where(col <= row, s, -jnp.inf)
        m_prev = m_ref[rows, :]
        m_new = jnp.maximum(m_prev, jnp.max(s, axis=1, keepdims=True))
        alpha = jnp.exp2(m_prev - m_new)
        p = jnp.exp2(s - jnp.concatenate([m_new] * n_rep, axis=1)).astype(BF16)
        v = v_ref[0, pl.ds(pl.multiple_of(j * tk, tk), tk), :]
        acc_ref[rows, :] = jnp.concatenate([alpha, alpha], axis=1) * acc_ref[rows, :] + _dot(p, v)
        m_ref[rows, :] = m_new

    m_ref[...] = jnp.full_like(m_ref, -jnp.inf)
    acc_ref[...] = jnp.zeros_like(acc_ref)
    qk(0, s0_ref, all_rows)

    def pair(jj, carry):
        j0 = 2 * jj
        qk(j0 + 1, s1_ref, all_rows)
        softmax_pv(j0, s0_ref, all_rows, masked=False)
        qk(j0 + 2, s0_ref, all_rows)
        softmax_pv(j0 + 1, s1_ref, all_rows, masked=False)
        return carry

    lax.fori_loop(0, i, pair, 0)
    qk(2 * i + 1, s1_ref, low_rows)
    softmax_pv(2 * i, s0_ref, all_rows, masked=True)
    softmax_pv(2 * i + 1, s1_ref, low_rows, masked=True)

    acc = acc_ref[...]
    o_ref[...] = (acc[:, :V_HEAD_DIM] / acc[:, V_HEAD_DIM:]).astype(BF16)


def _attention(q, k, v, tq=1024):
    nh, s, _ = q.shape
    return pl.pallas_call(
        functools.partial(_attn_kernel, tq=tq),
        grid=(nh, s // tq),
        in_specs=[pl.BlockSpec((1, tq, QK_PAD), lambda h, i: (h, i, 0)),
                  pl.BlockSpec((1, s, QK_PAD), lambda h, i: (h, 0, 0)),
                  pl.BlockSpec((1, s, V_PAD), lambda h, i: (h, 0, 0))],
        out_specs=pl.BlockSpec((tq, V_HEAD_DIM), lambda h, i: (i, h)),
        out_shape=jax.ShapeDtypeStruct((s, nh * V_HEAD_DIM), BF16),
        scratch_shapes=[pltpu.VMEM((tq, tq // 2), F32), pltpu.VMEM((tq, tq // 2), F32),
                        pltpu.VMEM((tq, LANES), F32), pltpu.VMEM((tq, V_PAD), F32)],
        compiler_params=_params(("parallel", "arbitrary")),
        name="attention",
    )(q, k, v)


def _output_kernel(x_ref, attn_ref, sz_ref, sgm_ref, co_ref, sgc_ref,
                   womla_ref, woconv_ref, wout_ref, g_ref, o_ref):
    a = (attn_ref[...].astype(F32) * sz_ref[...].astype(F32)).astype(BF16)
    y_mla = _dot(a, womla_ref[...])
    y_conv = _dot(co_ref[...], woconv_ref[...])
    merged = sgm_ref[...].astype(F32) * y_mla + sgc_ref[...].astype(F32) * y_conv
    out = _dot(merged.astype(BF16), wout_ref[...])
    o_ref[...] = x_ref[...] + _rms(out, g_ref[...])


def _output(x, attn, gates, co, womla, woconv, wout, g, tm=256):
    s, d = x.shape
    row = lambda c: pl.BlockSpec((tm, d), lambda i: (i, c))
    return pl.pallas_call(
        _output_kernel,
        grid=(s // tm,),
        in_specs=[row(0), row(0), row(0), row(1), row(0), row(2),
                  _resident(womla.shape), _resident(woconv.shape), _resident(wout.shape),
                  _resident(g.shape)],
        out_specs=row(0),
        out_shape=jax.ShapeDtypeStruct((s, d), F32),
        compiler_params=_params(("parallel",)),
        name="output",
    )(x, attn, gates, gates, co, gates, womla, woconv, wout, g)


def _swap_halves(r):
    half = r.shape[-1] // 2
    return jnp.concatenate([r[..., half:], r[..., :half]], axis=-1)


def kernel(x, positions, pre_norm_g, w_in, q_a_norm_g, w_q_b, kv_a_norm_g, w_kv_b, conv_w,
           w_o_mla, w_o_conv, w_out, post_norm_g):
    b, s, d = x.shape
    assert b == 1 and d == D_MODEL
    x2 = x[0]

    w_kr = w_in[:, LAT_IN - ROPE_DIM:LAT_IN]
    wlat = jnp.concatenate([w_in[:, :LAT_IN], _swap_halves(w_kr)], axis=1).astype(BF16)
    wq3 = w_q_b.reshape(Q_LORA_RANK, N_HEADS, QK_NOPE_DIM + ROPE_DIM)
    wq_rope = wq3[..., QK_NOPE_DIM:]
    wq = jnp.concatenate([wq3[..., :QK_NOPE_DIM], wq_rope, _swap_halves(wq_rope)], axis=-1)
    wq = wq.reshape(Q_LORA_RANK, N_HEADS * QK_PAD).astype(BF16)
    wkv3 = w_kv_b.reshape(KV_LORA_RANK, N_HEADS, QK_NOPE_DIM + V_HEAD_DIM)
    wkv = jnp.concatenate([wkv3[..., :QK_NOPE_DIM].reshape(KV_LORA_RANK, -1),
                           wkv3[..., QK_NOPE_DIM:].reshape(KV_LORA_RANK, -1)], axis=1).astype(BF16)

    inv_freq = ROPE_THETA ** (-jnp.arange(0, ROPE_DIM, 2, dtype=F32) / ROPE_DIM)
    ang = positions[0].astype(F32)[:, None] * inv_freq
    cos, sin = jnp.cos(ang), jnp.sin(ang)
    zeros = jnp.zeros((s, LANES - ROPE_DIM), F32)
    cos_t = jnp.concatenate([cos, cos, zeros], axis=1)
    sin_t = jnp.concatenate([-sin, sin, zeros], axis=1)

    h = _prenorm(x2, pre_norm_g.reshape(1, d))
    q, k, v = _latent(h, wlat, q_a_norm_g.reshape(1, -1), kv_a_norm_g.reshape(1, -1), wq, wkv,
                      cos_t, sin_t)
    gates = _gates(h, w_in)
    co = _conv(h, w_in, conv_w)
    attn = _attention(q, k, v)
    out = _output(x2, attn, gates, co, w_o_mla.astype(BF16), w_o_conv.astype(BF16),
                  w_out.astype(BF16), post_norm_g.reshape(1, d))
    return out[None]
```

```python
import functools
import math

import jax
import jax.numpy as jnp
from jax import lax
from jax.experimental import pallas as pl
from jax.experimental.pallas import tpu as pltpu

D_MODEL = 2048
N_HEADS = 16
QK_NOPE_DIM = 128
ROPE_DIM = 64
V_HEAD_DIM = 128
Q_LORA_RANK = 512
KV_LORA_RANK = 512
MLA_WIDTH = N_HEADS * V_HEAD_DIM
CONV_WIDTH = D_MODEL
CONV_K = 3
ROPE_THETA = 10000.0
RMS_EPS = 1e-6

LANES = 128
SUBLANES = 8
QK_PAD = QK_NOPE_DIM + LANES
V_PAD = V_HEAD_DIM + LANES
LAT_WIDTH = Q_LORA_RANK + KV_LORA_RANK + LANES
LAT_IN = Q_LORA_RANK + KV_LORA_RANK + ROPE_DIM
IN_Z_MLA = LAT_IN
IN_C_IN = IN_Z_MLA + MLA_WIDTH
IN_B_GATE = IN_C_IN + CONV_WIDTH
IN_C_GATE = IN_B_GATE + CONV_WIDTH
IN_Z_CONV = IN_C_GATE + CONV_WIDTH
IN_G_MLA = IN_Z_CONV + CONV_WIDTH
IN_G_CONV = IN_G_MLA + D_MODEL
VMEM_LIMIT = 56 * 1024 * 1024

BF16 = jnp.bfloat16
F32 = jnp.float32


def _params(semantics):
    return pltpu.CompilerParams(dimension_semantics=semantics, vmem_limit_bytes=VMEM_LIMIT)


def _resident(shape):
    return pl.BlockSpec(shape, lambda *_: (0,) * len(shape), pipeline_mode=pl.Buffered(1))


def _rms(xf, g):
    r = lax.rsqrt(jnp.mean(xf * xf, axis=-1, keepdims=True) + RMS_EPS)
    return xf * r * g


def _dot(a, b):
    return jnp.dot(a, b, preferred_element_type=F32)


def _dot_nt(a, b):
    return lax.dot_general(a, b, (((1,), (1,)), ((), ())), preferred_element_type=F32)


def _wt_rows(tn, d, first_row):
    return pl.BlockSpec((pl.Element(tn), pl.Element(d)),
                        lambda j, i: (pl.multiple_of(first_row(j), SUBLANES), 0))


def _prenorm_kernel(x_ref, g_ref, h_ref):
    h_ref[...] = _rms(x_ref[...], g_ref[...]).astype(BF16)


def _prenorm(x, g, tm=512):
    s, d = x.shape
    return pl.pallas_call(
        _prenorm_kernel,
        grid=(s // tm,),
        in_specs=[pl.BlockSpec((tm, d), lambda i: (i, 0)), _resident((1, d))],
        out_specs=pl.BlockSpec((tm, d), lambda i: (i, 0)),
        out_shape=jax.ShapeDtypeStruct((s, d), BF16),
        compiler_params=_params(("parallel",)),
        name="prenorm",
    )(x, g)


def _rope128(g2, cos_t, sin_t):
    return g2 * cos_t + pltpu.roll(g2, ROPE_DIM, axis=1) * sin_t


def _latent_kernel(h_ref, win_ref, gq_ref, gkv_ref, wq_ref, wkv_ref, cos_ref, sin_ref,
                   q_ref, k_ref, v_ref, wlat_sc, *, scale):
    @pl.when(pl.program_id(0) == 0)
    def _():
        half = ROPE_DIM // 2
        wlat_sc[:LAT_IN, :] = win_ref[...].astype(BF16)
        wlat_sc[LAT_IN:LAT_IN + half, :] = win_ref[LAT_IN - half:LAT_IN, :].astype(BF16)
        wlat_sc[LAT_IN + half:, :] = win_ref[LAT_IN - ROPE_DIM:LAT_IN - half, :].astype(BF16)

    lat = _dot_nt(h_ref[...], wlat_sc[...])
    qa = _rms(lat[:, :Q_LORA_RANK], gq_ref[...]).astype(BF16)
    ckv = _rms(lat[:, Q_LORA_RANK:Q_LORA_RANK + KV_LORA_RANK], gkv_ref[...]).astype(BF16)
    cos_t = cos_ref[...]
    sin_t = sin_ref[...]
    k_rope = _rope128(lat[:, Q_LORA_RANK + KV_LORA_RANK:], cos_t, sin_t).astype(BF16)
    q = _dot(qa, wq_ref[...]) * scale
    kv = _dot(ckv, wkv_ref[...])
    for hd in range(N_HEADS):
        base = hd * QK_PAD
        q_ref[hd, :, :QK_NOPE_DIM] = q[:, base:base + QK_NOPE_DIM].astype(BF16)
        q_ref[hd, :, QK_NOPE_DIM:] = _rope128(q[:, base + QK_NOPE_DIM:base + QK_PAD],
                                              cos_t, sin_t).astype(BF16)
        k_ref[hd, :, :QK_NOPE_DIM] = kv[:, hd * QK_NOPE_DIM:(hd + 1) * QK_NOPE_DIM].astype(BF16)
        k_ref[hd, :, QK_NOPE_DIM:] = k_rope
        v_ref[hd, :, :V_HEAD_DIM] = kv[:, MLA_WIDTH + hd * V_HEAD_DIM:
                                       MLA_WIDTH + (hd + 1) * V_HEAD_DIM].astype(BF16)
        v_ref[hd, :, V_HEAD_DIM:] = jnp.ones((kv.shape[0], LANES), BF16)


def _latent(h, w_in_t, gq, gkv, wq, wkv, cos_t, sin_t, tm=256):
    s, d = h.shape
    win_spec = pl.BlockSpec((pl.Element(LAT_IN), pl.Element(d)), lambda i: (0, 0),
                            pipeline_mode=pl.Buffered(1))
    scale = math.log2(math.e) / math.sqrt(QK_NOPE_DIM + ROPE_DIM)
    return pl.pallas_call(
        functools.partial(_latent_kernel, scale=scale),
        grid=(s // tm,),
        in_specs=[
            pl.BlockSpec((tm, d), lambda i: (i, 0)),
            win_spec, _resident(gq.shape), _resident(gkv.shape),
            _resident(wq.shape), _resident(wkv.shape),
            pl.BlockSpec((tm, LANES), lambda i: (i, 0)),
            pl.BlockSpec((tm, LANES), lambda i: (i, 0)),
        ],
        out_specs=[
            pl.BlockSpec((N_HEADS, tm, QK_PAD), lambda i: (0, i, 0)),
            pl.BlockSpec((N_HEADS, tm, QK_PAD), lambda i: (0, i, 0)),
            pl.BlockSpec((N_HEADS, tm, V_PAD), lambda i: (0, i, 0)),
        ],
        out_shape=[
            jax.ShapeDtypeStruct((N_HEADS, s, QK_PAD), BF16),
            jax.ShapeDtypeStruct((N_HEADS, s, QK_PAD), BF16),
            jax.ShapeDtypeStruct((N_HEADS, s, V_PAD), BF16),
        ],
        scratch_shapes=[pltpu.VMEM((LAT_WIDTH, d), BF16)],
        compiler_params=_params(("arbitrary",)),
        name="latent",
    )(h, w_in_t, gq, gkv, wq, wkv, cos_t, sin_t)


def _gates_kernel(h_ref, w_ref, o_ref, w_sc, *, n_silu_blocks):
    @pl.when(pl.program_id(1) == 0)
    def _():
        w_sc[...] = w_ref[...].astype(BF16)

    r = _dot_nt(h_ref[...], w_sc[...])
    sg = jax.nn.sigmoid(r)
    is_silu = pl.program_id(0) < n_silu_blocks
    o_ref[...] = jnp.where(is_silu, r * sg, sg).astype(BF16)


def _gates(h, w_in_t, tm=1024, tn=1024):
    s, d = h.shape
    n_silu_blocks = MLA_WIDTH // tn
    gap = IN_G_MLA - IN_Z_MLA - MLA_WIDTH
    return pl.pallas_call(
        functools.partial(_gates_kernel, n_silu_blocks=n_silu_blocks),
        grid=(3 * D_MODEL // tn, s // tm),
        in_specs=[pl.BlockSpec((tm, d), lambda j, i: (i, 0)),
                  _wt_rows(tn, d, lambda j: IN_Z_MLA + j * tn + jnp.where(j < n_silu_blocks, 0, gap))],
        out_specs=pl.BlockSpec((tm, tn), lambda j, i: (i, j)),
        out_shape=jax.ShapeDtypeStruct((s, 3 * D_MODEL), BF16),
        scratch_shapes=[pltpu.VMEM((tn, d), BF16)],
        compiler_params=_params(("arbitrary", "arbitrary")),
        name="gates",
    )(h, w_in_t)


def _conv_kernel(h_ref, wci_ref, wbg_ref, wcg_ref, wzc_ref, cw_ref, o_ref,
                 wci_sc, wbg_sc, wcg_sc, wzc_sc, carry_ref):
    i = pl.program_id(1)

    @pl.when(i == 0)
    def _():
        carry_ref[...] = jnp.zeros_like(carry_ref)
        wci_sc[...] = wci_ref[...].astype(BF16)
        wbg_sc[...] = wbg_ref[...].astype(BF16)
        wcg_sc[...] = wcg_ref[...].astype(BF16)
        wzc_sc[...] = wzc_ref[...].astype(BF16)

    h = h_ref[...]
    u = _dot_nt(h, wcg_sc[...]) * _dot_nt(h, wci_sc[...])
    b_gate = _dot_nt(h, wbg_sc[...])
    z_conv = _dot_nt(h, wzc_sc[...])
    prev = carry_ref[...]
    row = lax.broadcasted_iota(jnp.int32, u.shape, 0)
    u1 = jnp.where(row == 0, prev[SUBLANES - 1:SUBLANES, :], pltpu.roll(u, 1, axis=0))
    u2 = jnp.where(row == 0, prev[SUBLANES - 2:SUBLANES - 1, :],
                   jnp.where(row == 1, prev[SUBLANES - 1:SUBLANES, :], pltpu.roll(u, 2, axis=0)))
    cw = cw_ref[...]
    conv = cw[0:1, :] * u2 + cw[1:2, :] * u1 + cw[2:3, :] * u
    o_ref[...] = (b_gate * conv * (z_conv * jax.nn.sigmoid(z_conv))).astype(BF16)
    carry_ref[...] = u[u.shape[0] - SUBLANES:, :]


def _conv(h, w_in_t, conv_w, tm=1024, tn=256):
    s, d = h.shape
    seg_specs = [_wt_rows(tn, d, lambda j, start=start: start + j * tn)
                 for start in (IN_C_IN, IN_B_GATE, IN_C_GATE, IN_Z_CONV)]
    return pl.pallas_call(
        _conv_kernel,
        grid=(CONV_WIDTH // tn, s // tm),
        in_specs=[pl.BlockSpec((tm, d), lambda j, i: (i, 0))] + seg_specs
        + [pl.BlockSpec((CONV_K, tn), lambda j, i: (0, j))],
        out_specs=pl.BlockSpec((tm, tn), lambda j, i: (i, j)),
        out_shape=jax.ShapeDtypeStruct((s, CONV_WIDTH), BF16),
        scratch_shapes=[pltpu.VMEM((tn, d), BF16)] * 4 + [pltpu.VMEM((SUBLANES, tn), F32)],
        compiler_params=_params(("arbitrary", "arbitrary")),
        name="conv",
    )(h, *([w_in_t] * 4), conv_w)


def _attn_kernel(q_ref, k_ref, v_ref, o_ref, s0_ref, s1_ref, m_ref, acc_ref, *, tq):
    i = pl.program_id(1)
    tk = tq // 2
    n_rep = tk // LANES
    all_rows = slice(0, tq)
    low_rows = slice(tk, tq)

    def qk(j, s_ref, rows):
        k = k_ref[0, pl.ds(pl.multiple_of(j * tk, tk), tk), :]
        s_ref[rows, :] = lax.dot_general(q_ref[0, rows, :], k, (((1,), (1,)), ((), ())),
                                         preferred_element_type=F32)

    def softmax_pv(j, s_ref, rows, masked):
        s = s_ref[rows, :]
        if masked:
            row = lax.broadcasted_iota(jnp.int32, s.shape, 0)
            col = lax.broadcasted_iota(jnp.int32, s.shape, 1)
            s = jnp.where(col <= row, s, -jnp.inf)
        m_prev = m_ref[rows, :]
        m_new = jnp.maximum(m_prev, jnp.max(s, axis=1, keepdims=True))
        alpha = jnp.exp2(m_prev - m_new)
        p = jnp.exp2(s - jnp.concatenate([m_new] * n_rep, axis=1)).astype(BF16)
        v = v_ref[0, pl.ds(pl.multiple_of(j * tk, tk), tk), :]
        acc_ref[rows, :] = jnp.concatenate([alpha, alpha], axis=1) * acc_ref[rows, :] + _dot(p, v)
        m_ref[rows, :] = m_new

    m_ref[...] = jnp.full_like(m_ref, -jnp.inf)
    acc_ref[...] = jnp.zeros_like(acc_ref)
    qk(0, s0_ref, all_rows)

    def pair(jj, carry):
        j0 = 2 * jj
        qk(j0 + 1, s1_ref, all_rows)
        softmax_pv(j0, s0_ref, all_rows, masked=False)
        qk(j0 + 2, s0_ref, all_rows)
        softmax_pv(j0 + 1, s1_ref, all_rows, masked=False)
        return carry

    lax.fori_loop(0, i, pair, 0)
    qk(2 * i + 1, s1_ref, low_rows)
    softmax_pv(2 * i, s0_ref, all_rows, masked=True)
    softmax_pv(2 * i + 1, s1_ref, low_rows, masked=True)

    acc = acc_ref[...]
    o_ref[...] = (acc[:, :V_HEAD_DIM] / acc[:, V_HEAD_DIM:]).astype(BF16)


def _attention(q, k, v, tq=1024):
    nh, s, _ = q.shape
    return pl.pallas_call(
        functools.partial(_attn_kernel, tq=tq),
        grid=(nh, s // tq),
        in_specs=[pl.BlockSpec((1, tq, QK_PAD), lambda h, i: (h, i, 0)),
                  pl.BlockSpec((1, s, QK_PAD), lambda h, i: (h, 0, 0)),
                  pl.BlockSpec((1, s, V_PAD), lambda h, i: (h, 0, 0))],
        out_specs=pl.BlockSpec((tq, V_HEAD_DIM), lambda h, i: (i, h)),
        out_shape=jax.ShapeDtypeStruct((s, nh * V_HEAD_DIM), BF16),
        scratch_shapes=[pltpu.VMEM((tq, tq // 2), F32), pltpu.VMEM((tq, tq // 2), F32),
                        pltpu.VMEM((tq, LANES), F32), pltpu.VMEM((tq, V_PAD), F32)],
        compiler_params=_params(("parallel", "arbitrary")),
        name="attention",
    )(q, k, v)


def _output_kernel(x_ref, attn_ref, sz_ref, sgm_ref, co_ref, sgc_ref,
                   womla_ref, woconv_ref, wout_ref, g_ref, o_ref):
    a = (attn_ref[...].astype(F32) * sz_ref[...].astype(F32)).astype(BF16)
    y_mla = _dot(a, womla_ref[...])
    y_conv = _dot(co_ref[...], woconv_ref[...])
    merged = sgm_ref[...].astype(F32) * y_mla + sgc_ref[...].astype(F32) * y_conv
    out = _dot(merged.astype(BF16), wout_ref[...])
    o_ref[...] = x_ref[...] + _rms(out, g_ref[...])


def _output(x, attn, gates, co, womla, woconv, wout, g, tm=256):
    s, d = x.shape
    row = lambda c: pl.BlockSpec((tm, d), lambda i: (i, c))
    return pl.pallas_call(
        _output_kernel,
        grid=(s // tm,),
        in_specs=[row(0), row(0), row(0), row(1), row(0), row(2),
                  _resident(womla.shape), _resident(woconv.shape), _resident(wout.shape),
                  _resident(g.shape)],
        out_specs=row(0),
        out_shape=jax.ShapeDtypeStruct((s, d), F32),
        compiler_params=_params(("parallel",)),
        name="output",
    )(x, attn, gates, gates, co, gates, womla, woconv, wout, g)


def _swap_halves(r):
    half = r.shape[-1] // 2
    return jnp.concatenate([r[..., half:], r[..., :half]], axis=-1)


def kernel(x, positions, pre_norm_g, w_in, q_a_norm_g, w_q_b, kv_a_norm_g, w_kv_b, conv_w,
           w_o_mla, w_o_conv, w_out, post_norm_g):
    b, s, d = x.shape
    assert b == 1 and d == D_MODEL
    x2 = x[0]

    w_in_t = w_in.T
    wq3 = w_q_b.reshape(Q_LORA_RANK, N_HEADS, QK_NOPE_DIM + ROPE_DIM)
    wq_rope = wq3[..., QK_NOPE_DIM:]
    wq = jnp.concatenate([wq3[..., :QK_NOPE_DIM], wq_rope, _swap_halves(wq_rope)], axis=-1)
    wq = wq.reshape(Q_LORA_RANK, N_HEADS * QK_PAD).astype(BF16)
    wkv3 = w_kv_b.reshape(KV_LORA_RANK, N_HEADS, QK_NOPE_DIM + V_HEAD_DIM)
    wkv = jnp.concatenate([wkv3[..., :QK_NOPE_DIM].reshape(KV_LORA_RANK, -1),
                           wkv3[..., QK_NOPE_DIM:].reshape(KV_LORA_RANK, -1)], axis=1).astype(BF16)

    inv_freq = ROPE_THETA ** (-jnp.arange(0, ROPE_DIM, 2, dtype=F32) / ROPE_DIM)
    ang = positions[0].astype(F32)[:, None] * inv_freq
    cos, sin = jnp.cos(ang), jnp.sin(ang)
    zeros = jnp.zeros((s, LANES - ROPE_DIM), F32)
    cos_t = jnp.concatenate([cos, cos, zeros], axis=1)
    sin_t = jnp.concatenate([-sin, sin, zeros], axis=1)

    h = _prenorm(x2, pre_norm_g.reshape(1, d))
    q, k, v = _latent(h, w_in_t, q_a_norm_g.reshape(1, -1), kv_a_norm_g.reshape(1, -1), wq, wkv,
                      cos_t, sin_t)
    gates = _gates(h, w_in_t)
    co = _conv(h, w_in_t, conv_w)
    attn = _attention(q, k, v)
    out = _output(x2, attn, gates, co, w_o_mla.astype(BF16), w_o_conv.astype(BF16),
                  w_out.astype(BF16), post_norm_g.reshape(1, d))
    return out[None]
```

```python
import functools
import math

import jax
import jax.numpy as jnp
from jax import lax
from jax.experimental import pallas as pl
from jax.experimental.pallas import tpu as pltpu

D_MODEL = 2048
N_HEADS = 16
QK_NOPE_DIM = 128
ROPE_DIM = 64
V_HEAD_DIM = 128
Q_LORA_RANK = 512
KV_LORA_RANK = 512
MLA_WIDTH = N_HEADS * V_HEAD_DIM
CONV_WIDTH = D_MODEL
CONV_K = 3
ROPE_THETA = 10000.0
RMS_EPS = 1e-6

LANES = 128
SUBLANES = 8
QK_PAD = QK_NOPE_DIM + LANES
BF16_SUBLANES = 16
VT_PAD = V_HEAD_DIM + BF16_SUBLANES
ATTN_TK = 512
ATTN_TQ = 2 * ATTN_TK
ATTN_HEADS = 2
HEAD_GROUP = 4
LAT_WIDTH = Q_LORA_RANK + KV_LORA_RANK + LANES
LAT_IN = Q_LORA_RANK + KV_LORA_RANK + ROPE_DIM
IN_Z_MLA = LAT_IN
IN_C_IN = IN_Z_MLA + MLA_WIDTH
IN_B_GATE = IN_C_IN + CONV_WIDTH
IN_C_GATE = IN_B_GATE + CONV_WIDTH
IN_Z_CONV = IN_C_GATE + CONV_WIDTH
IN_G_MLA = IN_Z_CONV + CONV_WIDTH
IN_G_CONV = IN_G_MLA + D_MODEL
VMEM_LIMIT = 56 * 1024 * 1024

BF16 = jnp.bfloat16
F32 = jnp.float32


def _params(semantics):
    return pltpu.CompilerParams(dimension_semantics=semantics, vmem_limit_bytes=VMEM_LIMIT)


def _resident(shape):
    return pl.BlockSpec(shape, lambda *_: (0,) * len(shape), pipeline_mode=pl.Buffered(1))


def _rms(xf, g):
    r = lax.rsqrt(jnp.mean(xf * xf, axis=-1, keepdims=True) + RMS_EPS)
    return xf * r * g


def _dot(a, b):
    return jnp.dot(a, b, preferred_element_type=F32)


def _dot_nt(a, b):
    return lax.dot_general(a, b, (((1,), (1,)), ((), ())), preferred_element_type=F32)


def _wt_rows(tn, d, first_row):
    return pl.BlockSpec((pl.Element(tn), pl.Element(d)),
                        lambda j, i: (pl.multiple_of(first_row(j), SUBLANES), 0))


def _prenorm_kernel(x_ref, g_ref, h_ref):
    h_ref[...] = _rms(x_ref[...], g_ref[...]).astype(BF16)


def _prenorm(x, g, tm=512):
    s, d = x.shape
    return pl.pallas_call(
        _prenorm_kernel,
        grid=(s // tm,),
        in_specs=[pl.BlockSpec((tm, d), lambda i: (i, 0)), _resident((1, d))],
        out_specs=pl.BlockSpec((tm, d), lambda i: (i, 0)),
        out_shape=jax.ShapeDtypeStruct((s, d), BF16),
        compiler_params=_params(("parallel",)),
        name="prenorm",
    )(x, g)


def _rope128(g2, cos_t, sin_t):
    return g2 * cos_t + pltpu.roll(g2, ROPE_DIM, axis=1) * sin_t


def _latent_kernel(h_ref, win_ref, gq_ref, gkv_ref, wq_ref, wk_ref, wvt_ref, cos_ref, sin_ref,
                   q_ref, k_ref, vt_ref, wlat_sc, *, scale):
    @pl.when(pl.program_id(0) == 0)
    def _():
        half = ROPE_DIM // 2
        wlat_sc[:LAT_IN, :] = win_ref[...].astype(BF16)
        wlat_sc[LAT_IN:LAT_IN + half, :] = win_ref[LAT_IN - half:LAT_IN, :].astype(BF16)
        wlat_sc[LAT_IN + half:, :] = win_ref[LAT_IN - ROPE_DIM:LAT_IN - half, :].astype(BF16)

    lat = _dot_nt(h_ref[...], wlat_sc[...])
    qa = _rms(lat[:, :Q_LORA_RANK], gq_ref[...]).astype(BF16)
    ckv = _rms(lat[:, Q_LORA_RANK:Q_LORA_RANK + KV_LORA_RANK], gkv_ref[...]).astype(BF16)
    cos_t = cos_ref[...]
    sin_t = sin_ref[...]
    k_rope = _rope128(lat[:, Q_LORA_RANK + KV_LORA_RANK:], cos_t, sin_t).astype(BF16)
    tm = qa.shape[0]
    vt = _dot_nt(wvt_ref[...], ckv)
    ones_rows = jnp.ones((VT_PAD - V_HEAD_DIM, tm), BF16)
    for grp in range(N_HEADS // HEAD_GROUP):
        q = _dot(qa, wq_ref[:, grp * HEAD_GROUP * QK_PAD:(grp + 1) * HEAD_GROUP * QK_PAD]) * scale
        kn = _dot(ckv, wk_ref[:, grp * HEAD_GROUP * QK_NOPE_DIM:(grp + 1) * HEAD_GROUP * QK_NOPE_DIM])
        for sub in range(HEAD_GROUP):
            hd = grp * HEAD_GROUP + sub
            base = sub * QK_PAD
            q_ref[hd, :, :QK_NOPE_DIM] = q[:, base:base + QK_NOPE_DIM].astype(BF16)
            q_ref[hd, :, QK_NOPE_DIM:] = _rope128(q[:, base + QK_NOPE_DIM:base + QK_PAD],
                                                  cos_t, sin_t).astype(BF16)
            k_ref[hd, :, :QK_NOPE_DIM] = kn[:, sub * QK_NOPE_DIM:(sub + 1) * QK_NOPE_DIM].astype(BF16)
            k_ref[hd, :, QK_NOPE_DIM:] = k_rope
            vt_ref[hd, 0, :V_HEAD_DIM, :] = vt[hd * V_HEAD_DIM:(hd + 1) * V_HEAD_DIM, :].astype(BF16)
            vt_ref[hd, 0, V_HEAD_DIM:, :] = ones_rows


def _latent(h, w_in_t, gq, gkv, wq, wk, wvt, cos_t, sin_t, tm=256):
    s, d = h.shape
    per_blk = ATTN_TK // tm
    win_spec = pl.BlockSpec((pl.Element(LAT_IN), pl.Element(d)), lambda i: (0, 0),
                            pipeline_mode=pl.Buffered(1))
    scale = math.log2(math.e) / math.sqrt(QK_NOPE_DIM + ROPE_DIM)
    return pl.pallas_call(
        functools.partial(_latent_kernel, scale=scale),
        grid=(s // tm,),
        in_specs=[
            pl.BlockSpec((tm, d), lambda i: (i, 0)),
            win_spec, _resident(gq.shape), _resident(gkv.shape),
            _resident(wq.shape), _resident(wk.shape), _resident(wvt.shape),
            pl.BlockSpec((tm, LANES), lambda i: (i, 0)),
            pl.BlockSpec((tm, LANES), lambda i: (i, 0)),
        ],
        out_specs=[
            pl.BlockSpec((N_HEADS, tm, QK_PAD), lambda i: (0, i, 0)),
            pl.BlockSpec((N_HEADS, tm, QK_PAD), lambda i: (0, i, 0)),
            pl.BlockSpec((N_HEADS, 1, VT_PAD, tm), lambda i: (0, i // per_blk, 0, i % per_blk)),
        ],
        out_shape=[
            jax.ShapeDtypeStruct((N_HEADS, s, QK_PAD), BF16),
            jax.ShapeDtypeStruct((N_HEADS, s, QK_PAD), BF16),
            jax.ShapeDtypeStruct((N_HEADS, s // ATTN_TK, VT_PAD, ATTN_TK), BF16),
        ],
        scratch_shapes=[pltpu.VMEM((LAT_WIDTH, d), BF16)],
        compiler_params=_params(("arbitrary",)),
        name="latent",
    )(h, w_in_t, gq, gkv, wq, wk, wvt, cos_t, sin_t)


def _gates_kernel(h_ref, w_ref, o_ref, w_sc, *, n_silu_blocks):
    @pl.when(pl.program_id(1) == 0)
    def _():
        w_sc[...] = w_ref[...].astype(BF16)

    r = _dot_nt(h_ref[...], w_sc[...])
    sg = jax.nn.sigmoid(r)
    is_silu = pl.program_id(0) < n_silu_blocks
    o_ref[...] = jnp.where(is_silu, r * sg, sg).astype(BF16)


def _gates(h, w_in_t, tm=1024, tn=1024):
    s, d = h.shape
    n_silu_blocks = MLA_WIDTH // tn
    gap = IN_G_MLA - IN_Z_MLA - MLA_WIDTH
    return pl.pallas_call(
        functools.partial(_gates_kernel, n_silu_blocks=n_silu_blocks),
        grid=(3 * D_MODEL // tn, s // tm),
        in_specs=[pl.BlockSpec((tm, d), lambda j, i: (i, 0)),
                  _wt_rows(tn, d, lambda j: IN_Z_MLA + j * tn + jnp.where(j < n_silu_blocks, 0, gap))],
        out_specs=pl.BlockSpec((tm, tn), lambda j, i: (i, j)),
        out_shape=jax.ShapeDtypeStruct((s, 3 * D_MODEL), BF16),
        scratch_shapes=[pltpu.VMEM((tn, d), BF16)],
        compiler_params=_params(("arbitrary", "arbitrary")),
        name="gates",
    )(h, w_in_t)


def _conv_kernel(h_ref, wci_ref, wbg_ref, wcg_ref, wzc_ref, cw_ref, o_ref,
                 wci_sc, wbg_sc, wcg_sc, wzc_sc, carry_ref):
    i = pl.program_id(1)

    @pl.when(i == 0)
    def _():
        carry_ref[...] = jnp.zeros_like(carry_ref)
        wci_sc[...] = wci_ref[...].astype(BF16)
        wbg_sc[...] = wbg_ref[...].astype(BF16)
        wcg_sc[...] = wcg_ref[...].astype(BF16)
        wzc_sc[...] = wzc_ref[...].astype(BF16)

    h = h_ref[...]
    u = _dot_nt(h, wcg_sc[...]) * _dot_nt(h, wci_sc[...])
    b_gate = _dot_nt(h, wbg_sc[...])
    z_conv = _dot_nt(h, wzc_sc[...])
    prev = carry_ref[...]
    row = lax.broadcasted_iota(jnp.int32, u.shape, 0)
    u1 = jnp.where(row == 0, prev[SUBLANES - 1:SUBLANES, :], pltpu.roll(u, 1, axis=0))
    u2 = jnp.where(row == 0, prev[SUBLANES - 2:SUBLANES - 1, :],
                   jnp.where(row == 1, prev[SUBLANES - 1:SUBLANES, :], pltpu.roll(u, 2, axis=0)))
    cw = cw_ref[...]
    conv = cw[0:1, :] * u2 + cw[1:2, :] * u1 + cw[2:3, :] * u
    o_ref[...] = (b_gate * conv * (z_conv * jax.nn.sigmoid(z_conv))).astype(BF16)
    carry_ref[...] = u[u.shape[0] - SUBLANES:, :]


def _conv(h, w_in_t, conv_w, tm=1024, tn=256):
    s, d = h.shape
    seg_specs = [_wt_rows(tn, d, lambda j, start=start: start + j * tn)
                 for start in (IN_C_IN, IN_B_GATE, IN_C_GATE, IN_Z_CONV)]
    return pl.pallas_call(
        _conv_kernel,
        grid=(CONV_WIDTH // tn, s // tm),
        in_specs=[pl.BlockSpec((tm, d), lambda j, i: (i, 0))] + seg_specs
        + [pl.BlockSpec((CONV_K, tn), lambda j, i: (0, j))],
        out_specs=pl.BlockSpec((tm, tn), lambda j, i: (i, j)),
        out_shape=jax.ShapeDtypeStruct((s, CONV_WIDTH), BF16),
        scratch_shapes=[pltpu.VMEM((tn, d), BF16)] * 4 + [pltpu.VMEM((SUBLANES, tn), F32)],
        compiler_params=_params(("arbitrary", "arbitrary")),
        name="conv",
    )(h, *([w_in_t] * 4), conv_w)


def _attn_kernel(q_ref, k_ref, vt_ref, o_ref, s0_ref, s1_ref, m_ref, acc_ref):
    i = pl.program_id(1)
    tk, tq = ATTN_TK, ATTN_TQ
    all_q = slice(0, tq)
    late_q = slice(tk, tq)
    heads = range(ATTN_HEADS)

    def qk(hd, j, s_ref, qs):
        k = k_ref[hd, pl.ds(pl.multiple_of(j * tk, tk), tk), :]
        s_ref[hd, :, qs] = _dot_nt(k, q_ref[hd, qs, :])

    def softmax_pv(hd, j, s_ref, qs, masked):
        s = s_ref[hd, :, qs]
        if masked:
            key = lax.broadcasted_iota(jnp.int32, s.shape, 0)
            qry = lax.broadcasted_iota(jnp.int32, s.shape, 1)
            s = jnp.where(key <= qry, s, -jnp.inf)
        m_prev = m_ref[hd, :, qs]
        m_new = jnp.maximum(m_prev, jnp.max(s, axis=0, keepdims=True))
        alpha = jnp.exp2(m_prev - m_new)
        p = jnp.exp2(s - m_new).astype(BF16)
        acc_ref[hd, :, qs] = alpha * acc_ref[hd, :, qs] + _dot(vt_ref[hd, j], p)
        m_ref[hd, :, qs] = m_new

    m_ref[...] = jnp.full_like(m_ref, -jnp.inf)
    acc_ref[...] = jnp.zeros_like(acc_ref)
    for hd in heads:
        qk(hd, 0, s0_ref, all_q)

    def pair(jj, carry):
        j0 = 2 * jj
        for hd in heads:
            qk(hd, j0 + 1, s1_ref, all_q)
            softmax_pv(hd, j0, s0_ref, all_q, masked=False)
        for hd in heads:
            qk(hd, j0 + 2, s0_ref, all_q)
            softmax_pv(hd, j0 + 1, s1_ref, all_q, masked=False)
        return carry

    lax.fori_loop(0, i, pair, 0)
    for hd in heads:
        qk(hd, 2 * i + 1, s1_ref, late_q)
        softmax_pv(hd, 2 * i, s0_ref, all_q, masked=True)
    for hd in heads:
        softmax_pv(hd, 2 * i + 1, s1_ref, late_q, masked=True)

    for hd in heads:
        acc = acc_ref[hd]
        out_t = acc[:V_HEAD_DIM, :] / acc[V_HEAD_DIM:V_HEAD_DIM + 1, :]
        o_ref[:, hd * V_HEAD_DIM:(hd + 1) * V_HEAD_DIM] = out_t.T.astype(BF16)


def _attention(q, k, vt):
    nh, s, _ = q.shape
    tk, tq, hpb = ATTN_TK, ATTN_TQ, ATTN_HEADS
    return pl.pallas_call(
        _attn_kernel,
        grid=(nh // hpb, s // tq),
        in_specs=[pl.BlockSpec((hpb, tq, QK_PAD), lambda h, i: (h, i, 0)),
                  pl.BlockSpec((hpb, s, QK_PAD), lambda h, i: (h, 0, 0)),
                  pl.BlockSpec((hpb, s // tk, VT_PAD, tk), lambda h, i: (h, 0, 0, 0))],
        out_specs=pl.BlockSpec((tq, hpb * V_HEAD_DIM), lambda h, i: (i, h)),
        out_shape=jax.ShapeDtypeStruct((s, nh * V_HEAD_DIM), BF16),
        scratch_shapes=[pltpu.VMEM((hpb, tk, tq), F32), pltpu.VMEM((hpb, tk, tq), F32),
                        pltpu.VMEM((hpb, 1, tq), F32), pltpu.VMEM((hpb, VT_PAD, tq), F32)],
        compiler_params=_params(("parallel", "arbitrary")),
        name="attention",
    )(q, k, vt)


def _output_kernel(x_ref, attn_ref, sz_ref, sgm_ref, co_ref, sgc_ref,
                   womla_ref, woconv_ref, wout_ref, g_ref, o_ref):
    a = (attn_ref[...].astype(F32) * sz_ref[...].astype(F32)).astype(BF16)
    y_mla = _dot(a, womla_ref[...])
    y_conv = _dot(co_ref[...], woconv_ref[...])
    merged = sgm_ref[...].astype(F32) * y_mla + sgc_ref[...].astype(F32) * y_conv
    out = _dot(merged.astype(BF16), wout_ref[...])
    o_ref[...] = x_ref[...] + _rms(out, g_ref[...])


def _output(x, attn, gates, co, womla, woconv, wout, g, tm=256):
    s, d = x.shape
    row = lambda c: pl.BlockSpec((tm, d), lambda i: (i, c))
    return pl.pallas_call(
        _output_kernel,
        grid=(s // tm,),
        in_specs=[row(0), row(0), row(0), row(1), row(0), row(2),
                  _resident(womla.shape), _resident(woconv.shape), _resident(wout.shape),
                  _resident(g.shape)],
        out_specs=row(0),
        out_shape=jax.ShapeDtypeStruct((s, d), F32),
        compiler_params=_params(("parallel",)),
        name="output",
    )(x, attn, gates, gates, co, gates, womla, woconv, wout, g)


def _swap_halves(r):
    half = r.shape[-1] // 2
    return jnp.concatenate([r[..., half:], r[..., :half]], axis=-1)


def kernel(x, positions, pre_norm_g, w_in, q_a_norm_g, w_q_b, kv_a_norm_g, w_kv_b, conv_w,
           w_o_mla, w_o_conv, w_out, post_norm_g):
    b, s, d = x.shape
    assert b == 1 and d == D_MODEL
    x2 = x[0]

    w_in_t = w_in.T
    wq3 = w_q_b.reshape(Q_LORA_RANK, N_HEADS, QK_NOPE_DIM + ROPE_DIM)
    wq_rope = wq3[..., QK_NOPE_DIM:]
    wq = jnp.concatenate([wq3[..., :QK_NOPE_DIM], wq_rope, _swap_halves(wq_rope)], axis=-1)
    wq = wq.reshape(Q_LORA_RANK, N_HEADS * QK_PAD).astype(BF16)
    wkv3 = w_kv_b.reshape(KV_LORA_RANK, N_HEADS, QK_NOPE_DIM + V_HEAD_DIM)
    wk = wkv3[..., :QK_NOPE_DIM].reshape(KV_LORA_RANK, -1).astype(BF16)
    wvt = wkv3[..., QK_NOPE_DIM:].reshape(KV_LORA_RANK, -1).T.astype(BF16)

    inv_freq = ROPE_THETA ** (-jnp.arange(0, ROPE_DIM, 2, dtype=F32) / ROPE_DIM)
    ang = positions[0].astype(F32)[:, None] * inv_freq
    cos, sin = jnp.cos(ang), jnp.sin(ang)
    zeros = jnp.zeros((s, LANES - ROPE_DIM), F32)
    cos_t = jnp.concatenate([cos, cos, zeros], axis=1)
    sin_t = jnp.concatenate([-sin, sin, zeros], axis=1)

    h = _prenorm(x2, pre_norm_g.reshape(1, d))
    q, k, vt = _latent(h, w_in_t, q_a_norm_g.reshape(1, -1), kv_a_norm_g.reshape(1, -1), wq, wk, wvt,
                       cos_t, sin_t)
    gates = _gates(h, w_in_t)
    co = _conv(h, w_in_t, conv_w)
    attn = _attention(q, k, vt)
    out = _output(x2, attn, gates, co, w_o_mla.astype(BF16), w_o_conv.astype(BF16),
                  w_out.astype(BF16), post_norm_g.reshape(1, d))
    return out[None]
```

```python
import functools
import math

import jax
import jax.numpy as jnp
from jax import lax
from jax.experimental import pallas as pl
from jax.experimental.pallas import tpu as pltpu

D_MODEL = 2048
N_HEADS = 16
QK_NOPE_DIM = 128
ROPE_DIM = 64
V_HEAD_DIM = 128
Q_LORA_RANK = 512
KV_LORA_RANK = 512
MLA_WIDTH = N_HEADS * V_HEAD_DIM
CONV_WIDTH = D_MODEL
CONV_K = 3
ROPE_THETA = 10000.0
RMS_EPS = 1e-6

LANES = 128
SUBLANES = 8
QK_PAD = QK_NOPE_DIM + LANES
BF16_SUBLANES = 16
VT_PAD = V_HEAD_DIM + BF16_SUBLANES
ATTN_TK = 512
ATTN_TQ = 2 * ATTN_TK
ATTN_HEADS = 2
HEAD_GROUP = 4
LAT_WIDTH = Q_LORA_RANK + KV_LORA_RANK + LANES
LAT_IN = Q_LORA_RANK + KV_LORA_RANK + ROPE_DIM
IN_Z_MLA = LAT_IN
IN_C_IN = IN_Z_MLA + MLA_WIDTH
IN_B_GATE = IN_C_IN + CONV_WIDTH
IN_C_GATE = IN_B_GATE + CONV_WIDTH
IN_Z_CONV = IN_C_GATE + CONV_WIDTH
IN_G_MLA = IN_Z_CONV + CONV_WIDTH
IN_G_CONV = IN_G_MLA + D_MODEL
VMEM_LIMIT = 56 * 1024 * 1024

BF16 = jnp.bfloat16
F32 = jnp.float32


def _params(semantics):
    return pltpu.CompilerParams(dimension_semantics=semantics, vmem_limit_bytes=VMEM_LIMIT)


def _resident(shape):
    return pl.BlockSpec(shape, lambda *_: (0,) * len(shape), pipeline_mode=pl.Buffered(1))


def _rms(xf, g):
    r = lax.rsqrt(jnp.mean(xf * xf, axis=-1, keepdims=True) + RMS_EPS)
    return xf * r * g


def _dot(a, b):
    return jnp.dot(a, b, preferred_element_type=F32)


def _dot_nt(a, b):
    return lax.dot_general(a, b, (((1,), (1,)), ((), ())), preferred_element_type=F32)


def _wt_rows(tn, d, first_row):
    return pl.BlockSpec((pl.Element(tn), pl.Element(d)),
                        lambda j, i: (pl.multiple_of(first_row(j), SUBLANES), 0))


def _rope128(g2, cos_t, sin_t):
    return g2 * cos_t + pltpu.roll(g2, ROPE_DIM, axis=1) * sin_t


def _latent_kernel(x_ref, gpre_ref, win_ref, gq_ref, gkv_ref, wq_ref, wk_ref, wvt_ref, cos_ref, sin_ref,
                   h_ref, q_ref, k_ref, vt_ref, wlat_sc, *, scale):
    @pl.when(pl.program_id(0) == 0)
    def _():
        half = ROPE_DIM // 2
        wlat_sc[:LAT_IN, :] = win_ref[...].astype(BF16)
        wlat_sc[LAT_IN:LAT_IN + half, :] = win_ref[LAT_IN - half:LAT_IN, :].astype(BF16)
        wlat_sc[LAT_IN + half:, :] = win_ref[LAT_IN - ROPE_DIM:LAT_IN - half, :].astype(BF16)

    h = _rms(x_ref[...], gpre_ref[...]).astype(BF16)
    h_ref[...] = h
    lat = _dot_nt(h, wlat_sc[...])
    qa = _rms(lat[:, :Q_LORA_RANK], gq_ref[...]).astype(BF16)
    ckv = _rms(lat[:, Q_LORA_RANK:Q_LORA_RANK + KV_LORA_RANK], gkv_ref[...]).astype(BF16)
    cos_t = cos_ref[...]
    sin_t = sin_ref[...]
    k_rope = _rope128(lat[:, Q_LORA_RANK + KV_LORA_RANK:], cos_t, sin_t).astype(BF16)
    tm = qa.shape[0]
    vt = _dot_nt(wvt_ref[...], ckv)
    ones_rows = jnp.ones((VT_PAD - V_HEAD_DIM, tm), BF16)
    for grp in range(N_HEADS // HEAD_GROUP):
        q = _dot(qa, wq_ref[:, grp * HEAD_GROUP * QK_PAD:(grp + 1) * HEAD_GROUP * QK_PAD]) * scale
        kn = _dot(ckv, wk_ref[:, grp * HEAD_GROUP * QK_NOPE_DIM:(grp + 1) * HEAD_GROUP * QK_NOPE_DIM])
        for sub in range(HEAD_GROUP):
            hd = grp * HEAD_GROUP + sub
            base = sub * QK_PAD
            q_ref[hd, :, :QK_NOPE_DIM] = q[:, base:base + QK_NOPE_DIM].astype(BF16)
            q_ref[hd, :, QK_NOPE_DIM:] = _rope128(q[:, base + QK_NOPE_DIM:base + QK_PAD],
                                                  cos_t, sin_t).astype(BF16)
            k_ref[hd, :, :QK_NOPE_DIM] = kn[:, sub * QK_NOPE_DIM:(sub + 1) * QK_NOPE_DIM].astype(BF16)
            k_ref[hd, :, QK_NOPE_DIM:] = k_rope
            vt_ref[hd, 0, :V_HEAD_DIM, :] = vt[hd * V_HEAD_DIM:(hd + 1) * V_HEAD_DIM, :].astype(BF16)
            vt_ref[hd, 0, V_HEAD_DIM:, :] = ones_rows


def _latent(x, gpre, w_in_t, gq, gkv, wq, wk, wvt, cos_t, sin_t, tm=256):
    s, d = x.shape
    per_blk = ATTN_TK // tm
    win_spec = pl.BlockSpec((pl.Element(LAT_IN), pl.Element(d)), lambda i: (0, 0),
                            pipeline_mode=pl.Buffered(1))
    scale = math.log2(math.e) / math.sqrt(QK_NOPE_DIM + ROPE_DIM)
    return pl.pallas_call(
        functools.partial(_latent_kernel, scale=scale),
        grid=(s // tm,),
        in_specs=[
            pl.BlockSpec((tm, d), lambda i: (i, 0)), _resident(gpre.shape),
            win_spec, _resident(gq.shape), _resident(gkv.shape),
            _resident(wq.shape), _resident(wk.shape), _resident(wvt.shape),
            pl.BlockSpec((tm, LANES), lambda i: (i, 0)),
            pl.BlockSpec((tm, LANES), lambda i: (i, 0)),
        ],
        out_specs=[
            pl.BlockSpec((tm, d), lambda i: (i, 0)),
            pl.BlockSpec((N_HEADS, tm, QK_PAD), lambda i: (0, i, 0)),
            pl.BlockSpec((N_HEADS, tm, QK_PAD), lambda i: (0, i, 0)),
            pl.BlockSpec((N_HEADS, 1, VT_PAD, tm), lambda i: (0, i // per_blk, 0, i % per_blk)),
        ],
        out_shape=[
            jax.ShapeDtypeStruct((s, d), BF16),
            jax.ShapeDtypeStruct((N_HEADS, s, QK_PAD), BF16),
            jax.ShapeDtypeStruct((N_HEADS, s, QK_PAD), BF16),
            jax.ShapeDtypeStruct((N_HEADS, s // ATTN_TK, VT_PAD, ATTN_TK), BF16),
        ],
        scratch_shapes=[pltpu.VMEM((LAT_WIDTH, d), BF16)],
        compiler_params=_params(("arbitrary",)),
        name="latent",
    )(x, gpre, w_in_t, gq, gkv, wq, wk, wvt, cos_t, sin_t)


def _gates_kernel(h_ref, w_ref, o_ref, w_sc, *, n_silu_blocks):
    @pl.when(pl.program_id(1) == 0)
    def _():
        w_sc[...] = w_ref[...].astype(BF16)

    r = _dot_nt(h_ref[...], w_sc[...])
    sg = jax.nn.sigmoid(r)
    is_silu = pl.program_id(0) < n_silu_blocks
    o_ref[...] = jnp.where(is_silu, r * sg, sg).astype(BF16)


def _gates(h, w_in_t, tm=1024, tn=1024):
    s, d = h.shape
    n_silu_blocks = MLA_WIDTH // tn
    gap = IN_G_MLA - IN_Z_MLA - MLA_WIDTH
    return pl.pallas_call(
        functools.partial(_gates_kernel, n_silu_blocks=n_silu_blocks),
        grid=(3 * D_MODEL // tn, s // tm),
        in_specs=[pl.BlockSpec((tm, d), lambda j, i: (i, 0)),
                  _wt_rows(tn, d, lambda j: IN_Z_MLA + j * tn + jnp.where(j < n_silu_blocks, 0, gap))],
        out_specs=pl.BlockSpec((tm, tn), lambda j, i: (i, j)),
        out_shape=jax.ShapeDtypeStruct((s, 3 * D_MODEL), BF16),
        scratch_shapes=[pltpu.VMEM((tn, d), BF16)],
        compiler_params=_params(("arbitrary", "arbitrary")),
        name="gates",
    )(h, w_in_t)


def _conv_kernel(h_ref, wci_ref, wbg_ref, wcg_ref, wzc_ref, cw_ref, o_ref,
                 wci_sc, wbg_sc, wcg_sc, wzc_sc, carry_ref):
    i = pl.program_id(1)

    @pl.when(i == 0)
    def _():
        carry_ref[...] = jnp.zeros_like(carry_ref)
        wci_sc[...] = wci_ref[...].astype(BF16)
        wbg_sc[...] = wbg_ref[...].astype(BF16)
        wcg_sc[...] = wcg_ref[...].astype(BF16)
        wzc_sc[...] = wzc_ref[...].astype(BF16)

    h = h_ref[...]
    u = _dot_nt(h, wcg_sc[...]) * _dot_nt(h, wci_sc[...])
    b_gate = _dot_nt(h, wbg_sc[...])
    z_conv = _dot_nt(h, wzc_sc[...])
    prev = carry_ref[...]
    row = lax.broadcasted_iota(jnp.int32, u.shape, 0)
    u1 = jnp.where(row == 0, prev[SUBLANES - 1:SUBLANES, :], pltpu.roll(u, 1, axis=0))
    u2 = jnp.where(row == 0, prev[SUBLANES - 2:SUBLANES - 1, :],
                   jnp.where(row == 1, prev[SUBLANES - 1:SUBLANES, :], pltpu.roll(u, 2, axis=0)))
    cw = cw_ref[...]
    conv = cw[0:1, :] * u2 + cw[1:2, :] * u1 + cw[2:3, :] * u
    o_ref[...] = (b_gate * conv * (z_conv * jax.nn.sigmoid(z_conv))).astype(BF16)
    carry_ref[...] = u[u.shape[0] - SUBLANES:, :]


def _conv(h, w_in_t, conv_w, tm=1024, tn=256):
    s, d = h.shape
    seg_specs = [_wt_rows(tn, d, lambda j, start=start: start + j * tn)
                 for start in (IN_C_IN, IN_B_GATE, IN_C_GATE, IN_Z_CONV)]
    return pl.pallas_call(
        _conv_kernel,
        grid=(CONV_WIDTH // tn, s // tm),
        in_specs=[pl.BlockSpec((tm, d), lambda j, i: (i, 0))] + seg_specs
        + [pl.BlockSpec((CONV_K, tn), lambda j, i: (0, j))],
        out_specs=pl.BlockSpec((tm, tn), lambda j, i: (i, j)),
        out_shape=jax.ShapeDtypeStruct((s, CONV_WIDTH), BF16),
        scratch_shapes=[pltpu.VMEM((tn, d), BF16)] * 4 + [pltpu.VMEM((SUBLANES, tn), F32)],
        compiler_params=_params(("arbitrary", "arbitrary")),
        name="conv",
    )(h, *([w_in_t] * 4), conv_w)


def _attn_kernel(q_ref, k_ref, vt_ref, o_ref, s0_ref, s1_ref, m_ref, acc_ref):
    i = pl.program_id(1)
    tk, tq = ATTN_TK, ATTN_TQ
    all_q = slice(0, tq)
    late_q = slice(tk, tq)
    heads = range(ATTN_HEADS)

    def qk(hd, j, s_ref, qs):
        k = k_ref[hd, pl.ds(pl.multiple_of(j * tk, tk), tk), :]
        s_ref[hd, :, qs] = _dot_nt(k, q_ref[hd, qs, :])

    def softmax_pv(hd, j, s_ref, qs, masked):
        s = s_ref[hd, :, qs]
        if masked:
            key = lax.broadcasted_iota(jnp.int32, s.shape, 0)
            qry = lax.broadcasted_iota(jnp.int32, s.shape, 1)
            s = jnp.where(key <= qry, s, -jnp.inf)
        m_prev = m_ref[hd, :, qs]
        m_new = jnp.maximum(m_prev, jnp.max(s, axis=0, keepdims=True))
        alpha = jnp.exp2(m_prev - m_new)
        p = jnp.exp2(s - m_new).astype(BF16)
        acc_ref[hd, :, qs] = alpha * acc_ref[hd, :, qs] + _dot(vt_ref[hd, j], p)
        m_ref[hd, :, qs] = m_new

    m_ref[...] = jnp.full_like(m_ref, -jnp.inf)
    acc_ref[...] = jnp.zeros_like(acc_ref)
    for hd in heads:
        qk(hd, 0, s0_ref, all_q)

    def pair(jj, carry):
        j0 = 2 * jj
        for hd in heads:
            qk(hd, j0 + 1, s1_ref, all_q)
            softmax_pv(hd, j0, s0_ref, all_q, masked=False)
        for hd in heads:
            qk(hd, j0 + 2, s0_ref, all_q)
            softmax_pv(hd, j0 + 1, s1_ref, all_q, masked=False)
        return carry

    lax.fori_loop(0, i, pair, 0)
    for hd in heads:
        qk(hd, 2 * i + 1, s1_ref, late_q)
        softmax_pv(hd, 2 * i, s0_ref, all_q, masked=True)
    for hd in heads:
        softmax_pv(hd, 2 * i + 1, s1_ref, late_q, masked=True)

    for hd in heads:
        acc = acc_ref[hd]
        out_t = acc[:V_HEAD_DIM, :] / acc[V_HEAD_DIM:V_HEAD_DIM + 1, :]
        o_ref[:, hd * V_HEAD_DIM:(hd + 1) * V_HEAD_DIM] = out_t.T.astype(BF16)


def _attention(q, k, vt):
    nh, s, _ = q.shape
    tk, tq, hpb = ATTN_TK, ATTN_TQ, ATTN_HEADS
    return pl.pallas_call(
        _attn_kernel,
        grid=(nh // hpb, s // tq),
        in_specs=[pl.BlockSpec((hpb, tq, QK_PAD), lambda h, i: (h, i, 0)),
                  pl.BlockSpec((hpb, s, QK_PAD), lambda h, i: (h, 0, 0)),
                  pl.BlockSpec((hpb, s // tk, VT_PAD, tk), lambda h, i: (h, 0, 0, 0))],
        out_specs=pl.BlockSpec((tq, hpb * V_HEAD_DIM), lambda h, i: (i, h)),
        out_shape=jax.ShapeDtypeStruct((s, nh * V_HEAD_DIM), BF16),
        scratch_shapes=[pltpu.VMEM((hpb, tk, tq), F32), pltpu.VMEM((hpb, tk, tq), F32),
                        pltpu.VMEM((hpb, 1, tq), F32), pltpu.VMEM((hpb, VT_PAD, tq), F32)],
        compiler_params=_params(("parallel", "arbitrary")),
        name="attention",
    )(q, k, vt)


def _output_kernel(x_ref, attn_ref, sz_ref, sgm_ref, co_ref, sgc_ref,
                   womla_ref, woconv_ref, wout_ref, g_ref, o_ref):
    a = (attn_ref[...].astype(F32) * sz_ref[...].astype(F32)).astype(BF16)
    y_mla = _dot(a, womla_ref[...])
    y_conv = _dot(co_ref[...], woconv_ref[...])
    merged = sgm_ref[...].astype(F32) * y_mla + sgc_ref[...].astype(F32) * y_conv
    out = _dot(merged.astype(BF16), wout_ref[...])
    o_ref[...] = x_ref[...] + _rms(out, g_ref[...])


def _output(x, attn, gates, co, womla, woconv, wout, g, tm=256):
    s, d = x.shape
    row = lambda c: pl.BlockSpec((tm, d), lambda i: (i, c))
    return pl.pallas_call(
        _output_kernel,
        grid=(s // tm,),
        in_specs=[row(0), row(0), row(0), row(1), row(0), row(2),
                  _resident(womla.shape), _resident(woconv.shape), _resident(wout.shape),
                  _resident(g.shape)],
        out_specs=row(0),
        out_shape=jax.ShapeDtypeStruct((s, d), F32),
        compiler_params=_params(("parallel",)),
        name="output",
    )(x, attn, gates, gates, co, gates, womla, woconv, wout, g)


def _swap_halves(r):
    half = r.shape[-1] // 2
    return jnp.concatenate([r[..., half:], r[..., :half]], axis=-1)


def kernel(x, positions, pre_norm_g, w_in, q_a_norm_g, w_q_b, kv_a_norm_g, w_kv_b, conv_w,
           w_o_mla, w_o_conv, w_out, post_norm_g):
    b, s, d = x.shape
    assert b == 1 and d == D_MODEL
    x2 = x[0]

    w_in_t = w_in.T
    wq3 = w_q_b.reshape(Q_LORA_RANK, N_HEADS, QK_NOPE_DIM + ROPE_DIM)
    wq_rope = wq3[..., QK_NOPE_DIM:]
    wq = jnp.concatenate([wq3[..., :QK_NOPE_DIM], wq_rope, _swap_halves(wq_rope)], axis=-1)
    wq = wq.reshape(Q_LORA_RANK, N_HEADS * QK_PAD).astype(BF16)
    wkv3 = w_kv_b.reshape(KV_LORA_RANK, N_HEADS, QK_NOPE_DIM + V_HEAD_DIM)
    wk = wkv3[..., :QK_NOPE_DIM].reshape(KV_LORA_RANK, -1).astype(BF16)
    wvt = wkv3[..., QK_NOPE_DIM:].reshape(KV_LORA_RANK, -1).T.astype(BF16)

    inv_freq = ROPE_THETA ** (-jnp.arange(0, ROPE_DIM, 2, dtype=F32) / ROPE_DIM)
    half = ROPE_DIM // 2
    lane_freq = jnp.concatenate([inv_freq, inv_freq, jnp.zeros((LANES - ROPE_DIM,), F32)])
    lane_cos = jnp.concatenate([jnp.ones((ROPE_DIM,), F32), jnp.zeros((LANES - ROPE_DIM,), F32)])
    lane_sin = jnp.concatenate([-jnp.ones((half,), F32), jnp.ones((half,), F32),
                                jnp.zeros((LANES - ROPE_DIM,), F32)])
    ang = positions[0].astype(F32)[:, None] * lane_freq
    cos_t = jnp.cos(ang) * lane_cos
    sin_t = jnp.sin(ang) * lane_sin

    h, q, k, vt = _latent(x2, pre_norm_g.reshape(1, d), w_in_t, q_a_norm_g.reshape(1, -1),
                          kv_a_norm_g.reshape(1, -1), wq, wk, wvt, cos_t, sin_t)
    gates = _gates(h, w_in_t)
    co = _conv(h, w_in_t, conv_w)
    attn = _attention(q, k, vt)
    out = _output(x2, attn, gates, co, w_o_mla.astype(BF16), w_o_conv.astype(BF16),
                  w_out.astype(BF16), post_norm_g.reshape(1, d))
    return out[None]
```

```python
import functools
import math

import jax
import jax.numpy as jnp
from jax import lax
from jax.experimental import pallas as pl
from jax.experimental.pallas import tpu as pltpu

D_MODEL = 2048
N_HEADS = 16
QK_NOPE_DIM = 128
ROPE_DIM = 64
V_HEAD_DIM = 128
Q_LORA_RANK = 512
KV_LORA_RANK = 512
MLA_WIDTH = N_HEADS * V_HEAD_DIM
CONV_WIDTH = D_MODEL
CONV_K = 3
ROPE_THETA = 10000.0
RMS_EPS = 1e-6

LANES = 128
SUBLANES = 8
QK_PAD = QK_NOPE_DIM + LANES
BF16_SUBLANES = 16
VT_PAD = V_HEAD_DIM + BF16_SUBLANES
ATTN_TK = 512
ATTN_TQ = 2 * ATTN_TK
ATTN_HEADS = 2
HEAD_GROUP = 4
LAT_WIDTH = Q_LORA_RANK + KV_LORA_RANK + LANES
LAT_IN = Q_LORA_RANK + KV_LORA_RANK + ROPE_DIM
IN_Z_MLA = LAT_IN
IN_C_IN = IN_Z_MLA + MLA_WIDTH
IN_B_GATE = IN_C_IN + CONV_WIDTH
IN_C_GATE = IN_B_GATE + CONV_WIDTH
IN_Z_CONV = IN_C_GATE + CONV_WIDTH
IN_G_MLA = IN_Z_CONV + CONV_WIDTH
IN_G_CONV = IN_G_MLA + D_MODEL
VMEM_LIMIT = 56 * 1024 * 1024

BF16 = jnp.bfloat16
F32 = jnp.float32


def _params(semantics):
    return pltpu.CompilerParams(dimension_semantics=semantics, vmem_limit_bytes=VMEM_LIMIT)


def _resident(shape):
    return pl.BlockSpec(shape, lambda *_: (0,) * len(shape), pipeline_mode=pl.Buffered(1))


def _rms(xf, g):
    r = lax.rsqrt(jnp.mean(xf * xf, axis=-1, keepdims=True) + RMS_EPS)
    return xf * r * g


def _sigmoid(x):
    return 0.5 * jnp.tanh(0.5 * x) + 0.5


def _dot(a, b):
    return jnp.dot(a, b, preferred_element_type=F32)


def _dot_nt(a, b):
    return lax.dot_general(a, b, (((1,), (1,)), ((), ())), preferred_element_type=F32)


def _wt_rows(tn, d, first_row):
    return pl.BlockSpec((pl.Element(tn), pl.Element(d)),
                        lambda j, i: (pl.multiple_of(first_row(j), SUBLANES), 0))


def _rope128(g2, cos_t, sin_t):
    return g2 * cos_t + pltpu.roll(g2, ROPE_DIM, axis=1) * sin_t


def _latent_kernel(x_ref, gpre_ref, win_ref, gq_ref, gkv_ref, wq_ref, wk_ref, wvt_ref, cos_ref, sin_ref,
                   h_ref, q_ref, k_ref, vt_ref, wlat_sc, *, scale):
    @pl.when(pl.program_id(0) == 0)
    def _():
        half = ROPE_DIM // 2
        wlat_sc[:LAT_IN, :] = win_ref[...].astype(BF16)
        wlat_sc[LAT_IN:LAT_IN + half, :] = win_ref[LAT_IN - half:LAT_IN, :].astype(BF16)
        wlat_sc[LAT_IN + half:, :] = win_ref[LAT_IN - ROPE_DIM:LAT_IN - half, :].astype(BF16)

    h = _rms(x_ref[...], gpre_ref[...]).astype(BF16)
    h_ref[...] = h
    lat = _dot_nt(h, wlat_sc[...])
    qa = _rms(lat[:, :Q_LORA_RANK], gq_ref[...]).astype(BF16)
    ckv = _rms(lat[:, Q_LORA_RANK:Q_LORA_RANK + KV_LORA_RANK], gkv_ref[...]).astype(BF16)
    cos_t = cos_ref[...]
    sin_t = sin_ref[...]
    k_rope = _rope128(lat[:, Q_LORA_RANK + KV_LORA_RANK:], cos_t, sin_t).astype(BF16)
    tm = qa.shape[0]
    vt = _dot_nt(wvt_ref[...], ckv)
    ones_rows = jnp.ones((VT_PAD - V_HEAD_DIM, tm), BF16)
    for grp in range(N_HEADS // HEAD_GROUP):
        q = _dot(qa, wq_ref[:, grp * HEAD_GROUP * QK_PAD:(grp + 1) * HEAD_GROUP * QK_PAD]) * scale
        kn = _dot(ckv, wk_ref[:, grp * HEAD_GROUP * QK_NOPE_DIM:(grp + 1) * HEAD_GROUP * QK_NOPE_DIM])
        for sub in range(HEAD_GROUP):
            hd = grp * HEAD_GROUP + sub
            base = sub * QK_PAD
            q_ref[hd, :, :QK_NOPE_DIM] = q[:, base:base + QK_NOPE_DIM].astype(BF16)
            q_ref[hd, :, QK_NOPE_DIM:] = _rope128(q[:, base + QK_NOPE_DIM:base + QK_PAD],
                                                  cos_t, sin_t).astype(BF16)
            k_ref[hd, :, :QK_NOPE_DIM] = kn[:, sub * QK_NOPE_DIM:(sub + 1) * QK_NOPE_DIM].astype(BF16)
            k_ref[hd, :, QK_NOPE_DIM:] = k_rope
            vt_ref[hd, 0, :V_HEAD_DIM, :] = vt[hd * V_HEAD_DIM:(hd + 1) * V_HEAD_DIM, :].astype(BF16)
            vt_ref[hd, 0, V_HEAD_DIM:, :] = ones_rows


def _latent(x, gpre, w_in_t, gq, gkv, wq, wk, wvt, cos_t, sin_t, tm=256):
    s, d = x.shape
    per_blk = ATTN_TK // tm
    win_spec = pl.BlockSpec((pl.Element(LAT_IN), pl.Element(d)), lambda i: (0, 0),
                            pipeline_mode=pl.Buffered(1))
    scale = math.log2(math.e) / math.sqrt(QK_NOPE_DIM + ROPE_DIM)
    return pl.pallas_call(
        functools.partial(_latent_kernel, scale=scale),
        grid=(s // tm,),
        in_specs=[
            pl.BlockSpec((tm, d), lambda i: (i, 0)), _resident(gpre.shape),
            win_spec, _resident(gq.shape), _resident(gkv.shape),
            _resident(wq.shape), _resident(wk.shape), _resident(wvt.shape),
            pl.BlockSpec((tm, LANES), lambda i: (i, 0)),
            pl.BlockSpec((tm, LANES), lambda i: (i, 0)),
        ],
        out_specs=[
            pl.BlockSpec((tm, d), lambda i: (i, 0)),
            pl.BlockSpec((N_HEADS, tm, QK_PAD), lambda i: (0, i, 0)),
            pl.BlockSpec((N_HEADS, tm, QK_PAD), lambda i: (0, i, 0)),
            pl.BlockSpec((N_HEADS, 1, VT_PAD, tm), lambda i: (0, i // per_blk, 0, i % per_blk)),
        ],
        out_shape=[
            jax.ShapeDtypeStruct((s, d), BF16),
            jax.ShapeDtypeStruct((N_HEADS, s, QK_PAD), BF16),
            jax.ShapeDtypeStruct((N_HEADS, s, QK_PAD), BF16),
            jax.ShapeDtypeStruct((N_HEADS, s // ATTN_TK, VT_PAD, ATTN_TK), BF16),
        ],
        scratch_shapes=[pltpu.VMEM((LAT_WIDTH, d), BF16)],
        compiler_params=_params(("arbitrary",)),
        name="latent",
    )(x, gpre, w_in_t, gq, gkv, wq, wk, wvt, cos_t, sin_t)


def _gates_kernel(h_ref, w_ref, o_ref, w_sc, *, n_silu_blocks):
    @pl.when(pl.program_id(1) == 0)
    def _():
        w_sc[...] = w_ref[...].astype(BF16)

    r = _dot_nt(h_ref[...], w_sc[...])
    sg = _sigmoid(r)
    is_silu = pl.program_id(0) < n_silu_blocks
    o_ref[...] = jnp.where(is_silu, r * sg, sg).astype(BF16)


def _gates(h, w_in_t, tm=1024, tn=1024):
    s, d = h.shape
    n_silu_blocks = MLA_WIDTH // tn
    gap = IN_G_MLA - IN_Z_MLA - MLA_WIDTH
    return pl.pallas_call(
        functools.partial(_gates_kernel, n_silu_blocks=n_silu_blocks),
        grid=(3 * D_MODEL // tn, s // tm),
        in_specs=[pl.BlockSpec((tm, d), lambda j, i: (i, 0)),
                  _wt_rows(tn, d, lambda j: IN_Z_MLA + j * tn + jnp.where(j < n_silu_blocks, 0, gap))],
        out_specs=pl.BlockSpec((tm, tn), lambda j, i: (i, j)),
        out_shape=jax.ShapeDtypeStruct((s, 3 * D_MODEL), BF16),
        scratch_shapes=[pltpu.VMEM((tn, d), BF16)],
        compiler_params=_params(("arbitrary", "arbitrary")),
        name="gates",
    )(h, w_in_t)


def _conv_kernel(h_ref, wci_ref, wbg_ref, wcg_ref, wzc_ref, cw_ref, o_ref,
                 wci_sc, wbg_sc, wcg_sc, wzc_sc, carry_ref):
    i = pl.program_id(1)

    @pl.when(i == 0)
    def _():
        carry_ref[...] = jnp.zeros_like(carry_ref)
        wci_sc[...] = wci_ref[...].astype(BF16)
        wbg_sc[...] = wbg_ref[...].astype(BF16)
        wcg_sc[...] = wcg_ref[...].astype(BF16)
        wzc_sc[...] = wzc_ref[...].astype(BF16)

    h = h_ref[...]
    u = _dot_nt(h, wcg_sc[...]) * _dot_nt(h, wci_sc[...])
    b_gate = _dot_nt(h, wbg_sc[...])
    z_conv = _dot_nt(h, wzc_sc[...])
    prev = carry_ref[...]
    row = lax.broadcasted_iota(jnp.int32, u.shape, 0)
    u1 = jnp.where(row == 0, prev[SUBLANES - 1:SUBLANES, :], pltpu.roll(u, 1, axis=0))
    u2 = jnp.where(row == 0, prev[SUBLANES - 2:SUBLANES - 1, :],
                   jnp.where(row == 1, prev[SUBLANES - 1:SUBLANES, :], pltpu.roll(u, 2, axis=0)))
    cw = cw_ref[...]
    conv = cw[0:1, :] * u2 + cw[1:2, :] * u1 + cw[2:3, :] * u
    o_ref[...] = (b_gate * conv * (z_conv * _sigmoid(z_conv))).astype(BF16)
    carry_ref[...] = u[u.shape[0] - SUBLANES:, :]


def _conv(h, w_in_t, conv_w, tm=1024, tn=256):
    s, d = h.shape
    seg_specs = [_wt_rows(tn, d, lambda j, start=start: start + j * tn)
                 for start in (IN_C_IN, IN_B_GATE, IN_C_GATE, IN_Z_CONV)]
    return pl.pallas_call(
        _conv_kernel,
        grid=(CONV_WIDTH // tn, s // tm),
        in_specs=[pl.BlockSpec((tm, d), lambda j, i: (i, 0))] + seg_specs
        + [pl.BlockSpec((CONV_K, tn), lambda j, i: (0, j))],
        out_specs=pl.BlockSpec((tm, tn), lambda j, i: (i, j)),
        out_shape=jax.ShapeDtypeStruct((s, CONV_WIDTH), BF16),
        scratch_shapes=[pltpu.VMEM((tn, d), BF16)] * 4 + [pltpu.VMEM((SUBLANES, tn), F32)],
        compiler_params=_params(("arbitrary", "arbitrary")),
        name="conv",
    )(h, *([w_in_t] * 4), conv_w)


def _attn_kernel(q_ref, k_ref, vt_ref, o_ref, s0_ref, s1_ref, m_ref, acc_ref):
    i = pl.program_id(1)
    tk, tq = ATTN_TK, ATTN_TQ
    all_q = slice(0, tq)
    late_q = slice(tk, tq)
    heads = range(ATTN_HEADS)

    def qk(hd, j, s_ref, qs):
        k = k_ref[hd, pl.ds(pl.multiple_of(j * tk, tk), tk), :]
        s_ref[hd, :, qs] = _dot_nt(k, q_ref[hd, qs, :])

    def softmax_pv(hd, j, s_ref, qs, masked):
        s = s_ref[hd, :, qs]
        if masked:
            key = lax.broadcasted_iota(jnp.int32, s.shape, 0)
            qry = lax.broadcasted_iota(jnp.int32, s.shape, 1)
            s = jnp.where(key <= qry, s, -jnp.inf)
        m_prev = m_ref[hd, :, qs]
        m_new = jnp.maximum(m_prev, jnp.max(s, axis=0, keepdims=True))
        alpha = jnp.exp2(m_prev - m_new)
        p = jnp.exp2(s - m_new).astype(BF16)
        acc_ref[hd, :, qs] = alpha * acc_ref[hd, :, qs] + _dot(vt_ref[hd, j], p)
        m_ref[hd, :, qs] = m_new

    m_ref[...] = jnp.full_like(m_ref, -jnp.inf)
    acc_ref[...] = jnp.zeros_like(acc_ref)
    for hd in heads:
        qk(hd, 0, s0_ref, all_q)

    def pair(j0):
        for hd in heads:
            qk(hd, j0 + 1, s1_ref, all_q)
            softmax_pv(hd, j0, s0_ref, all_q, masked=False)
        for hd in heads:
            qk(hd, j0 + 2, s0_ref, all_q)
            softmax_pv(hd, j0 + 1, s1_ref, all_q, masked=False)

    def two_pairs(jj, carry):
        pair(4 * jj)
        pair(4 * jj + 2)
        return carry

    lax.fori_loop(0, i // 2, two_pairs, 0)

    @pl.when(i % 2 == 1)
    def _():
        pair(2 * i - 2)

    for hd in heads:
        qk(hd, 2 * i + 1, s1_ref, late_q)
        softmax_pv(hd, 2 * i, s0_ref, all_q, masked=True)
    for hd in heads:
        softmax_pv(hd, 2 * i + 1, s1_ref, late_q, masked=True)

    for hd in heads:
        acc = acc_ref[hd]
        out_t = acc[:V_HEAD_DIM, :] / acc[V_HEAD_DIM:V_HEAD_DIM + 1, :]
        o_ref[:, hd * V_HEAD_DIM:(hd + 1) * V_HEAD_DIM] = out_t.T.astype(BF16)


def _attention(q, k, vt):
    nh, s, _ = q.shape
    tk, tq, hpb = ATTN_TK, ATTN_TQ, ATTN_HEADS
    return pl.pallas_call(
        _attn_kernel,
        grid=(nh // hpb, s // tq),
        in_specs=[pl.BlockSpec((hpb, tq, QK_PAD), lambda h, i: (h, i, 0)),
                  pl.BlockSpec((hpb, s, QK_PAD), lambda h, i: (h, 0, 0)),
                  pl.BlockSpec((hpb, s // tk, VT_PAD, tk), lambda h, i: (h, 0, 0, 0))],
        out_specs=pl.BlockSpec((tq, hpb * V_HEAD_DIM), lambda h, i: (i, h)),
        out_shape=jax.ShapeDtypeStruct((s, nh * V_HEAD_DIM), BF16),
        scratch_shapes=[pltpu.VMEM((hpb, tk, tq), F32), pltpu.VMEM((hpb, tk, tq), F32),
                        pltpu.VMEM((hpb, 1, tq), F32), pltpu.VMEM((hpb, VT_PAD, tq), F32)],
        compiler_params=_params(("parallel", "arbitrary")),
        name="attention",
    )(q, k, vt)


def _output_kernel(x_ref, attn_ref, sz_ref, sgm_ref, co_ref, sgc_ref,
                   womla_ref, woconv_ref, wout_ref, g_ref, o_ref):
    a = (attn_ref[...].astype(F32) * sz_ref[...].astype(F32)).astype(BF16)
    y_mla = _dot(a, womla_ref[...])
    y_conv = _dot(co_ref[...], woconv_ref[...])
    merged = sgm_ref[...].astype(F32) * y_mla + sgc_ref[...].astype(F32) * y_conv
    out = _dot(merged.astype(BF16), wout_ref[...])
    o_ref[...] = x_ref[...] + _rms(out, g_ref[...])


def _output(x, attn, gates, co, womla, woconv, wout, g, tm=256):
    s, d = x.shape
    row = lambda c: pl.BlockSpec((tm, d), lambda i: (i, c))
    return pl.pallas_call(
        _output_kernel,
        grid=(s // tm,),
        in_specs=[row(0), row(0), row(0), row(1), row(0), row(2),
                  _resident(womla.shape), _resident(woconv.shape), _resident(wout.shape),
                  _resident(g.shape)],
        out_specs=row(0),
        out_shape=jax.ShapeDtypeStruct((s, d), F32),
        compiler_params=_params(("parallel",)),
        name="output",
    )(x, attn, gates, gates, co, gates, womla, woconv, wout, g)


def _swap_halves(r):
    half = r.shape[-1] // 2
    return jnp.concatenate([r[..., half:], r[..., :half]], axis=-1)


def kernel(x, positions, pre_norm_g, w_in, q_a_norm_g, w_q_b, kv_a_norm_g, w_kv_b, conv_w,
           w_o_mla, w_o_conv, w_out, post_norm_g):
    b, s, d = x.shape
    assert b == 1 and d == D_MODEL
    x2 = x[0]

    w_in_t = w_in.T
    wq3 = w_q_b.reshape(Q_LORA_RANK, N_HEADS, QK_NOPE_DIM + ROPE_DIM)
    wq_rope = wq3[..., QK_NOPE_DIM:]
    wq = jnp.concatenate([wq3[..., :QK_NOPE_DIM], wq_rope, _swap_halves(wq_rope)], axis=-1)
    wq = wq.reshape(Q_LORA_RANK, N_HEADS * QK_PAD).astype(BF16)
    wkv3 = w_kv_b.reshape(KV_LORA_RANK, N_HEADS, QK_NOPE_DIM + V_HEAD_DIM)
    wk = wkv3[..., :QK_NOPE_DIM].reshape(KV_LORA_RANK, -1).astype(BF16)
    wvt = wkv3[..., QK_NOPE_DIM:].reshape(KV_LORA_RANK, -1).T.astype(BF16)

    inv_freq = ROPE_THETA ** (-jnp.arange(0, ROPE_DIM, 2, dtype=F32) / ROPE_DIM)
    half = ROPE_DIM // 2
    lane_freq = jnp.concatenate([inv_freq, inv_freq, jnp.zeros((LANES - ROPE_DIM,), F32)])
    lane_cos = jnp.concatenate([jnp.ones((ROPE_DIM,), F32), jnp.zeros((LANES - ROPE_DIM,), F32)])
    lane_sin = jnp.concatenate([-jnp.ones((half,), F32), jnp.ones((half,), F32),
                                jnp.zeros((LANES - ROPE_DIM,), F32)])
    ang = positions[0].astype(F32)[:, None] * lane_freq
    cos_t = jnp.cos(ang) * lane_cos
    sin_t = jnp.sin(ang) * lane_sin

    h, q, k, vt = _latent(x2, pre_norm_g.reshape(1, d), w_in_t, q_a_norm_g.reshape(1, -1),
                          kv_a_norm_g.reshape(1, -1), wq, wk, wvt, cos_t, sin_t)
    gates = _gates(h, w_in_t)
    co = _conv(h, w_in_t, conv_w)
    attn = _attention(q, k, vt)
    out = _output(x2, attn, gates, co, w_o_mla.astype(BF16), w_o_conv.astype(BF16),
                  w_out.astype(BF16), post_norm_g.reshape(1, d))
    return out[None]
```

```python
import functools
import math

import jax
import jax.numpy as jnp
from jax import lax
from jax.experimental import pallas as pl
from jax.experimental.pallas import tpu as pltpu

D_MODEL = 2048
N_HEADS = 16
QK_NOPE_DIM = 128
ROPE_DIM = 64
V_HEAD_DIM = 128
Q_LORA_RANK = 512
KV_LORA_RANK = 512
MLA_WIDTH = N_HEADS * V_HEAD_DIM
CONV_WIDTH = D_MODEL
CONV_K = 3
ROPE_THETA = 10000.0
RMS_EPS = 1e-6

LANES = 128
SUBLANES = 8
QK_PAD = QK_NOPE_DIM + LANES
BF16_SUBLANES = 16
VT_PAD = V_HEAD_DIM + BF16_SUBLANES
ATTN_TK = 512
ATTN_TQ = 2 * ATTN_TK
ATTN_HEADS = 2
HEAD_GROUP = 4
LAT_WIDTH = Q_LORA_RANK + KV_LORA_RANK + LANES
LAT_IN = Q_LORA_RANK + KV_LORA_RANK + ROPE_DIM
IN_Z_MLA = LAT_IN
IN_C_IN = IN_Z_MLA + MLA_WIDTH
IN_B_GATE = IN_C_IN + CONV_WIDTH
IN_C_GATE = IN_B_GATE + CONV_WIDTH
IN_Z_CONV = IN_C_GATE + CONV_WIDTH
IN_G_MLA = IN_Z_CONV + CONV_WIDTH
IN_G_CONV = IN_G_MLA + D_MODEL
VMEM_LIMIT = 56 * 1024 * 1024

BF16 = jnp.bfloat16
F32 = jnp.float32


def _params(semantics):
    return pltpu.CompilerParams(dimension_semantics=semantics, vmem_limit_bytes=VMEM_LIMIT)


def _resident(shape):
    return pl.BlockSpec(shape, lambda *_: (0,) * len(shape), pipeline_mode=pl.Buffered(1))


def _rms(xf, g):
    r = lax.rsqrt(jnp.mean(xf * xf, axis=-1, keepdims=True) + RMS_EPS)
    return xf * r * g


def _sigmoid(x):
    return 0.5 * jnp.tanh(0.5 * x) + 0.5


def _dot(a, b):
    return jnp.dot(a, b, preferred_element_type=F32)


def _dot_nt(a, b):
    return lax.dot_general(a, b, (((1,), (1,)), ((), ())), preferred_element_type=F32)


def _wt_rows(tn, d, first_row):
    return pl.BlockSpec((pl.Element(tn), pl.Element(d)),
                        lambda j, i: (pl.multiple_of(first_row(j), SUBLANES), 0))


def _rope128(g2, cos_t, sin_t):
    return g2 * cos_t + pltpu.roll(g2, ROPE_DIM, axis=1) * sin_t


def _latent_kernel(x_ref, gpre_ref, win_ref, gq_ref, gkv_ref, wqt_ref, wk_ref, wvt_ref, cos_ref, sin_ref,
                   h_ref, qt_ref, k_ref, vt_ref, wlat_sc, *, scale):
    @pl.when(pl.program_id(0) == 0)
    def _():
        half = ROPE_DIM // 2
        wlat_sc[:LAT_IN, :] = win_ref[...].astype(BF16)
        wlat_sc[LAT_IN:LAT_IN + half, :] = win_ref[LAT_IN - half:LAT_IN, :].astype(BF16)
        wlat_sc[LAT_IN + half:, :] = win_ref[LAT_IN - ROPE_DIM:LAT_IN - half, :].astype(BF16)

    h = _rms(x_ref[...], gpre_ref[...]).astype(BF16)
    h_ref[...] = h
    lat = _dot_nt(h, wlat_sc[...])
    qa = _rms(lat[:, :Q_LORA_RANK], gq_ref[...]).astype(BF16)
    ckv = _rms(lat[:, Q_LORA_RANK:Q_LORA_RANK + KV_LORA_RANK], gkv_ref[...]).astype(BF16)
    cos_t = cos_ref[...]
    sin_t = sin_ref[...]
    k_rope = _rope128(lat[:, Q_LORA_RANK + KV_LORA_RANK:], cos_t, sin_t).astype(BF16)
    tm = qa.shape[0]
    vt = _dot_nt(wvt_ref[...], ckv)
    ones_rows = jnp.ones((VT_PAD - V_HEAD_DIM, tm), BF16)
    cos_rows = cos_t.T
    sin_rows = sin_t.T
    for grp in range(N_HEADS // HEAD_GROUP):
        qt = _dot_nt(wqt_ref[grp * HEAD_GROUP * QK_PAD:(grp + 1) * HEAD_GROUP * QK_PAD, :], qa) * scale
        kn = _dot(ckv, wk_ref[:, grp * HEAD_GROUP * QK_NOPE_DIM:(grp + 1) * HEAD_GROUP * QK_NOPE_DIM])
        for sub in range(HEAD_GROUP):
            hd = grp * HEAD_GROUP + sub
            base = sub * QK_PAD
            qt_ref[hd, :QK_NOPE_DIM, :] = qt[base:base + QK_NOPE_DIM, :].astype(BF16)
            g2 = qt[base + QK_NOPE_DIM:base + QK_PAD, :]
            g2_swapped = jnp.concatenate([g2[ROPE_DIM:], g2[:ROPE_DIM]], axis=0)
            qt_ref[hd, QK_NOPE_DIM:, :] = (g2 * cos_rows + g2_swapped * sin_rows).astype(BF16)
            k_ref[hd, :, :QK_NOPE_DIM] = kn[:, sub * QK_NOPE_DIM:(sub + 1) * QK_NOPE_DIM].astype(BF16)
            k_ref[hd, :, QK_NOPE_DIM:] = k_rope
            vt_ref[hd, 0, :V_HEAD_DIM, :] = vt[hd * V_HEAD_DIM:(hd + 1) * V_HEAD_DIM, :].astype(BF16)
            vt_ref[hd, 0, V_HEAD_DIM:, :] = ones_rows


def _latent(x, gpre, w_in_t, gq, gkv, wqt, wk, wvt, cos_t, sin_t, tm=256):
    s, d = x.shape
    per_blk = ATTN_TK // tm
    win_spec = pl.BlockSpec((pl.Element(LAT_IN), pl.Element(d)), lambda i: (0, 0),
                            pipeline_mode=pl.Buffered(1))
    scale = math.log2(math.e) / math.sqrt(QK_NOPE_DIM + ROPE_DIM)
    return pl.pallas_call(
        functools.partial(_latent_kernel, scale=scale),
        grid=(s // tm,),
        in_specs=[
            pl.BlockSpec((tm, d), lambda i: (i, 0)), _resident(gpre.shape),
            win_spec, _resident(gq.shape), _resident(gkv.shape),
            _resident(wqt.shape), _resident(wk.shape), _resident(wvt.shape),
            pl.BlockSpec((tm, LANES), lambda i: (i, 0)),
            pl.BlockSpec((tm, LANES), lambda i: (i, 0)),
        ],
        out_specs=[
            pl.BlockSpec((tm, d), lambda i: (i, 0)),
            pl.BlockSpec((N_HEADS, QK_PAD, tm), lambda i: (0, 0, i)),
            pl.BlockSpec((N_HEADS, tm, QK_PAD), lambda i: (0, i, 0)),
            pl.BlockSpec((N_HEADS, 1, VT_PAD, tm), lambda i: (0, i // per_blk, 0, i % per_blk)),
        ],
        out_shape=[
            jax.ShapeDtypeStruct((s, d), BF16),
            jax.ShapeDtypeStruct((N_HEADS, QK_PAD, s), BF16),
            jax.ShapeDtypeStruct((N_HEADS, s, QK_PAD), BF16),
            jax.ShapeDtypeStruct((N_HEADS, s // ATTN_TK, VT_PAD, ATTN_TK), BF16),
        ],
        scratch_shapes=[pltpu.VMEM((LAT_WIDTH, d), BF16)],
        compiler_params=_params(("arbitrary",)),
        name="latent",
    )(x, gpre, w_in_t, gq, gkv, wqt, wk, wvt, cos_t, sin_t)


def _gates_kernel(h_ref, w_ref, o_ref, w_sc, *, n_silu_blocks):
    @pl.when(pl.program_id(1) == 0)
    def _():
        w_sc[...] = w_ref[...].astype(BF16)

    r = _dot_nt(h_ref[...], w_sc[...])
    sg = _sigmoid(r)
    is_silu = pl.program_id(0) < n_silu_blocks
    o_ref[...] = jnp.where(is_silu, r * sg, sg).astype(BF16)


def _gates(h, w_in_t, tm=1024, tn=1024):
    s, d = h.shape
    n_silu_blocks = MLA_WIDTH // tn
    gap = IN_G_MLA - IN_Z_MLA - MLA_WIDTH
    return pl.pallas_call(
        functools.partial(_gates_kernel, n_silu_blocks=n_silu_blocks),
        grid=(3 * D_MODEL // tn, s // tm),
        in_specs=[pl.BlockSpec((tm, d), lambda j, i: (i, 0)),
                  _wt_rows(tn, d, lambda j: IN_Z_MLA + j * tn + jnp.where(j < n_silu_blocks, 0, gap))],
        out_specs=pl.BlockSpec((tm, tn), lambda j, i: (i, j)),
        out_shape=jax.ShapeDtypeStruct((s, 3 * D_MODEL), BF16),
        scratch_shapes=[pltpu.VMEM((tn, d), BF16)],
        compiler_params=_params(("arbitrary", "arbitrary")),
        name="gates",
    )(h, w_in_t)


def _conv_kernel(h_ref, wci_ref, wbg_ref, wcg_ref, wzc_ref, cw_ref, o_ref,
                 wci_sc, wbg_sc, wcg_sc, wzc_sc, carry_ref):
    i = pl.program_id(1)

    @pl.when(i == 0)
    def _():
        carry_ref[...] = jnp.zeros_like(carry_ref)
        wci_sc[...] = wci_ref[...].astype(BF16)
        wbg_sc[...] = wbg_ref[...].astype(BF16)
        wcg_sc[...] = wcg_ref[...].astype(BF16)
        wzc_sc[...] = wzc_ref[...].astype(BF16)

    h = h_ref[...]
    u = _dot_nt(h, wcg_sc[...]) * _dot_nt(h, wci_sc[...])
    b_gate = _dot_nt(h, wbg_sc[...])
    z_conv = _dot_nt(h, wzc_sc[...])
    prev = carry_ref[...]
    row = lax.broadcasted_iota(jnp.int32, u.shape, 0)
    u1 = jnp.where(row == 0, prev[SUBLANES - 1:SUBLANES, :], pltpu.roll(u, 1, axis=0))
    u2 = jnp.where(row == 0, prev[SUBLANES - 2:SUBLANES - 1, :],
                   jnp.where(row == 1, prev[SUBLANES - 1:SUBLANES, :], pltpu.roll(u, 2, axis=0)))
    cw = cw_ref[...]
    conv = cw[0:1, :] * u2 + cw[1:2, :] * u1 + cw[2:3, :] * u
    o_ref[...] = (b_gate * conv * (z_conv * _sigmoid(z_conv))).astype(BF16)
    carry_ref[...] = u[u.shape[0] - SUBLANES:, :]


def _conv(h, w_in_t, conv_w, tm=1024, tn=256):
    s, d = h.shape
    seg_specs = [_wt_rows(tn, d, lambda j, start=start: start + j * tn)
                 for start in (IN_C_IN, IN_B_GATE, IN_C_GATE, IN_Z_CONV)]
    return pl.pallas_call(
        _conv_kernel,
        grid=(CONV_WIDTH // tn, s // tm),
        in_specs=[pl.BlockSpec((tm, d), lambda j, i: (i, 0))] + seg_specs
        + [pl.BlockSpec((CONV_K, tn), lambda j, i: (0, j))],
        out_specs=pl.BlockSpec((tm, tn), lambda j, i: (i, j)),
        out_shape=jax.ShapeDtypeStruct((s, CONV_WIDTH), BF16),
        scratch_shapes=[pltpu.VMEM((tn, d), BF16)] * 4 + [pltpu.VMEM((SUBLANES, tn), F32)],
        compiler_params=_params(("arbitrary", "arbitrary")),
        name="conv",
    )(h, *([w_in_t] * 4), conv_w)


def _attn_kernel(qt_ref, k_ref, vt_ref, o_ref, s0_ref, s1_ref, m_ref, acc_ref):
    i = pl.program_id(1)
    tk, tq = ATTN_TK, ATTN_TQ
    all_q = slice(0, tq)
    late_q = slice(tk, tq)
    heads = range(ATTN_HEADS)

    def qk(hd, j, s_ref, qs):
        k = k_ref[hd, pl.ds(pl.multiple_of(j * tk, tk), tk), :]
        s_ref[hd, :, qs] = _dot(k, qt_ref[hd, :, qs])

    def softmax_pv(hd, j, s_ref, qs, masked):
        s = s_ref[hd, :, qs]
        if masked:
            key = lax.broadcasted_iota(jnp.int32, s.shape, 0)
            qry = lax.broadcasted_iota(jnp.int32, s.shape, 1)
            s = jnp.where(key <= qry, s, -jnp.inf)
        m_prev = m_ref[hd, :, qs]
        m_new = jnp.maximum(m_prev, jnp.max(s, axis=0, keepdims=True))
        alpha = jnp.exp2(m_prev - m_new)
        p = jnp.exp2(s - m_new).astype(BF16)
        acc_ref[hd, :, qs] = alpha * acc_ref[hd, :, qs] + _dot(vt_ref[hd, j], p)
        m_ref[hd, :, qs] = m_new

    m_ref[...] = jnp.full_like(m_ref, -jnp.inf)
    acc_ref[...] = jnp.zeros_like(acc_ref)
    for hd in heads:
        qk(hd, 0, s0_ref, all_q)

    def pair(j0):
        for hd in heads:
            qk(hd, j0 + 1, s1_ref, all_q)
            softmax_pv(hd, j0, s0_ref, all_q, masked=False)
        for hd in heads:
            qk(hd, j0 + 2, s0_ref, all_q)
            softmax_pv(hd, j0 + 1, s1_ref, all_q, masked=False)

    def two_pairs(jj, carry):
        pair(4 * jj)
        pair(4 * jj + 2)
        return carry

    lax.fori_loop(0, i // 2, two_pairs, 0)

    @pl.when(i % 2 == 1)
    def _():
        pair(2 * i - 2)

    for hd in heads:
        qk(hd, 2 * i + 1, s1_ref, late_q)
        softmax_pv(hd, 2 * i, s0_ref, all_q, masked=True)
    for hd in heads:
        softmax_pv(hd, 2 * i + 1, s1_ref, late_q, masked=True)

    for hd in heads:
        acc = acc_ref[hd]
        out_t = acc[:V_HEAD_DIM, :] / acc[V_HEAD_DIM:V_HEAD_DIM + 1, :]
        o_ref[:, hd * V_HEAD_DIM:(hd + 1) * V_HEAD_DIM] = out_t.T.astype(BF16)


def _attention(qt, k, vt):
    nh, s, _ = k.shape
    tk, tq, hpb = ATTN_TK, ATTN_TQ, ATTN_HEADS
    return pl.pallas_call(
        _attn_kernel,
        grid=(nh // hpb, s // tq),
        in_specs=[pl.BlockSpec((hpb, QK_PAD, tq), lambda h, i: (h, 0, i)),
                  pl.BlockSpec((hpb, s, QK_PAD), lambda h, i: (h, 0, 0)),
                  pl.BlockSpec((hpb, s // tk, VT_PAD, tk), lambda h, i: (h, 0, 0, 0))],
        out_specs=pl.BlockSpec((tq, hpb * V_HEAD_DIM), lambda h, i: (i, h)),
        out_shape=jax.ShapeDtypeStruct((s, nh * V_HEAD_DIM), BF16),
        scratch_shapes=[pltpu.VMEM((hpb, tk, tq), F32), pltpu.VMEM((hpb, tk, tq), F32),
                        pltpu.VMEM((hpb, 1, tq), F32), pltpu.VMEM((hpb, VT_PAD, tq), F32)],
        compiler_params=_params(("parallel", "arbitrary")),
        name="attention",
    )(qt, k, vt)


def _output_kernel(x_ref, attn_ref, sz_ref, sgm_ref, co_ref, sgc_ref,
                   womla_ref, woconv_ref, wout_ref, g_ref, o_ref):
    a = (attn_ref[...].astype(F32) * sz_ref[...].astype(F32)).astype(BF16)
    y_mla = _dot(a, womla_ref[...])
    y_conv = _dot(co_ref[...], woconv_ref[...])
    merged = sgm_ref[...].astype(F32) * y_mla + sgc_ref[...].astype(F32) * y_conv
    out = _dot(merged.astype(BF16), wout_ref[...])
    o_ref[...] = x_ref[...] + _rms(out, g_ref[...])


def _output(x, attn, gates, co, womla, woconv, wout, g, tm=256):
    s, d = x.shape
    row = lambda c: pl.BlockSpec((tm, d), lambda i: (i, c))
    return pl.pallas_call(
        _output_kernel,
        grid=(s // tm,),
        in_specs=[row(0), row(0), row(0), row(1), row(0), row(2),
                  _resident(womla.shape), _resident(woconv.shape), _resident(wout.shape),
                  _resident(g.shape)],
        out_specs=row(0),
        out_shape=jax.ShapeDtypeStruct((s, d), F32),
        compiler_params=_params(("parallel",)),
        name="output",
    )(x, attn, gates, gates, co, gates, womla, woconv, wout, g)


def _swap_halves(r):
    half = r.shape[-1] // 2
    return jnp.concatenate([r[..., half:], r[..., :half]], axis=-1)


def kernel(x, positions, pre_norm_g, w_in, q_a_norm_g, w_q_b, kv_a_norm_g, w_kv_b, conv_w,
           w_o_mla, w_o_conv, w_out, post_norm_g):
    b, s, d = x.shape
    assert b == 1 and d == D_MODEL
    x2 = x[0]

    w_in_t = w_in.T
    wq3 = w_q_b.reshape(Q_LORA_RANK, N_HEADS, QK_NOPE_DIM + ROPE_DIM)
    wq_rope = wq3[..., QK_NOPE_DIM:]
    wq = jnp.concatenate([wq3[..., :QK_NOPE_DIM], wq_rope, _swap_halves(wq_rope)], axis=-1)
    wqt = wq.reshape(Q_LORA_RANK, N_HEADS * QK_PAD).T.astype(BF16)
    wkv3 = w_kv_b.reshape(KV_LORA_RANK, N_HEADS, QK_NOPE_DIM + V_HEAD_DIM)
    wk = wkv3[..., :QK_NOPE_DIM].reshape(KV_LORA_RANK, -1).astype(BF16)
    wvt = wkv3[..., QK_NOPE_DIM:].reshape(KV_LORA_RANK, -1).T.astype(BF16)

    inv_freq = ROPE_THETA ** (-jnp.arange(0, ROPE_DIM, 2, dtype=F32) / ROPE_DIM)
    half = ROPE_DIM // 2
    lane_freq = jnp.concatenate([inv_freq, inv_freq, jnp.zeros((LANES - ROPE_DIM,), F32)])
    lane_cos = jnp.concatenate([jnp.ones((ROPE_DIM,), F32), jnp.zeros((LANES - ROPE_DIM,), F32)])
    lane_sin = jnp.concatenate([-jnp.ones((half,), F32), jnp.ones((half,), F32),
                                jnp.zeros((LANES - ROPE_DIM,), F32)])
    ang = positions[0].astype(F32)[:, None] * lane_freq
    cos_t = jnp.cos(ang) * lane_cos
    sin_t = jnp.sin(ang) * lane_sin

    h, qt, k, vt = _latent(x2, pre_norm_g.reshape(1, d), w_in_t, q_a_norm_g.reshape(1, -1),
                           kv_a_norm_g.reshape(1, -1), wqt, wk, wvt, cos_t, sin_t)
    gates = _gates(h, w_in_t)
    co = _conv(h, w_in_t, conv_w)
    attn = _attention(qt, k, vt)
    out = _output(x2, attn, gates, co, w_o_mla.astype(BF16), w_o_conv.astype(BF16),
                  w_out.astype(BF16), post_norm_g.reshape(1, d))
    return out[None]
```

```python
import functools
import math

import jax
import jax.numpy as jnp
from jax import lax
from jax.experimental import pallas as pl
from jax.experimental.pallas import tpu as pltpu

D_MODEL = 2048
N_HEADS = 16
QK_NOPE_DIM = 128
ROPE_DIM = 64
V_HEAD_DIM = 128
Q_LORA_RANK = 512
KV_LORA_RANK = 512
MLA_WIDTH = N_HEADS * V_HEAD_DIM
CONV_WIDTH = D_MODEL
CONV_K = 3
ROPE_THETA = 10000.0
RMS_EPS = 1e-6

LANES = 128
SUBLANES = 8
QK_PAD = QK_NOPE_DIM + LANES
BF16_SUBLANES = 16
VT_PAD = V_HEAD_DIM + BF16_SUBLANES
ATTN_TK = 512
ATTN_TQ = 2 * ATTN_TK
ATTN_HEADS = 2
HEAD_GROUP = 4
LAT_WIDTH = Q_LORA_RANK + KV_LORA_RANK + LANES
LAT_IN = Q_LORA_RANK + KV_LORA_RANK + ROPE_DIM
IN_Z_MLA = LAT_IN
IN_C_IN = IN_Z_MLA + MLA_WIDTH
IN_B_GATE = IN_C_IN + CONV_WIDTH
IN_C_GATE = IN_B_GATE + CONV_WIDTH
IN_Z_CONV = IN_C_GATE + CONV_WIDTH
IN_G_MLA = IN_Z_CONV + CONV_WIDTH
IN_G_CONV = IN_G_MLA + D_MODEL
VMEM_LIMIT = 56 * 1024 * 1024

BF16 = jnp.bfloat16
F32 = jnp.float32


def _params(semantics):
    return pltpu.CompilerParams(dimension_semantics=semantics, vmem_limit_bytes=VMEM_LIMIT)


def _resident(shape):
    return pl.BlockSpec(shape, lambda *_: (0,) * len(shape), pipeline_mode=pl.Buffered(1))


def _rms(xf, g):
    r = lax.rsqrt(jnp.mean(xf * xf, axis=-1, keepdims=True) + RMS_EPS)
    return xf * r * g


def _sigmoid(x):
    return 0.5 * jnp.tanh(0.5 * x) + 0.5


def _dot(a, b):
    return jnp.dot(a, b, preferred_element_type=F32)


def _dot_nt(a, b):
    return lax.dot_general(a, b, (((1,), (1,)), ((), ())), preferred_element_type=F32)


def _wt_rows(tn, d, first_row):
    return pl.BlockSpec((pl.Element(tn), pl.Element(d)),
                        lambda j, i: (pl.multiple_of(first_row(j), SUBLANES), 0))


def _rope128(g2, cos_t, sin_t):
    return g2 * cos_t + pltpu.roll(g2, ROPE_DIM, axis=1) * sin_t


def _latent_kernel(x_ref, gpre_ref, win_ref, gq_ref, gkv_ref, wqt_ref, wk_ref, wvt_ref, cos_ref, sin_ref,
                   h_ref, qt_ref, k_ref, vt_ref, wlat_sc, *, scale):
    @pl.when(pl.program_id(0) == 0)
    def _():
        half = ROPE_DIM // 2
        wlat_sc[:LAT_IN, :] = win_ref[...].astype(BF16)
        wlat_sc[LAT_IN:LAT_IN + half, :] = win_ref[LAT_IN - half:LAT_IN, :].astype(BF16)
        wlat_sc[LAT_IN + half:, :] = win_ref[LAT_IN - ROPE_DIM:LAT_IN - half, :].astype(BF16)

    h = _rms(x_ref[...], gpre_ref[...]).astype(BF16)
    h_ref[...] = h
    lat = _dot_nt(h, wlat_sc[...])
    qa = _rms(lat[:, :Q_LORA_RANK], gq_ref[...]).astype(BF16)
    ckv = _rms(lat[:, Q_LORA_RANK:Q_LORA_RANK + KV_LORA_RANK], gkv_ref[...]).astype(BF16)
    cos_t = cos_ref[...]
    sin_t = sin_ref[...]
    k_rope = _rope128(lat[:, Q_LORA_RANK + KV_LORA_RANK:], cos_t, sin_t).astype(BF16)
    tm = qa.shape[0]
    vt = _dot_nt(wvt_ref[...], ckv)
    ones_rows = jnp.ones((VT_PAD - V_HEAD_DIM, tm), BF16)
    cos_rows = cos_t.T
    sin_rows = sin_t.T
    for grp in range(N_HEADS // HEAD_GROUP):
        qt = _dot_nt(wqt_ref[grp * HEAD_GROUP * QK_PAD:(grp + 1) * HEAD_GROUP * QK_PAD, :], qa) * scale
        kn = _dot(ckv, wk_ref[:, grp * HEAD_GROUP * QK_NOPE_DIM:(grp + 1) * HEAD_GROUP * QK_NOPE_DIM])
        for sub in range(HEAD_GROUP):
            hd = grp * HEAD_GROUP + sub
            base = sub * QK_PAD
            qt_ref[hd, :QK_NOPE_DIM, :] = qt[base:base + QK_NOPE_DIM, :].astype(BF16)
            g2 = qt[base + QK_NOPE_DIM:base + QK_PAD, :]
            g2_swapped = jnp.concatenate([g2[ROPE_DIM:], g2[:ROPE_DIM]], axis=0)
            qt_ref[hd, QK_NOPE_DIM:, :] = (g2 * cos_rows + g2_swapped * sin_rows).astype(BF16)
            k_ref[hd, :, :QK_NOPE_DIM] = kn[:, sub * QK_NOPE_DIM:(sub + 1) * QK_NOPE_DIM].astype(BF16)
            k_ref[hd, :, QK_NOPE_DIM:] = k_rope
            vt_ref[hd, 0, :V_HEAD_DIM, :] = vt[hd * V_HEAD_DIM:(hd + 1) * V_HEAD_DIM, :].astype(BF16)
            vt_ref[hd, 0, V_HEAD_DIM:, :] = ones_rows


def _latent(x, gpre, w_in_t, gq, gkv, wqt, wk, wvt, cos_t, sin_t, tm=256):
    s, d = x.shape
    per_blk = ATTN_TK // tm
    win_spec = pl.BlockSpec((pl.Element(LAT_IN), pl.Element(d)), lambda i: (0, 0),
                            pipeline_mode=pl.Buffered(1))
    scale = math.log2(math.e) / math.sqrt(QK_NOPE_DIM + ROPE_DIM)
    return pl.pallas_call(
        functools.partial(_latent_kernel, scale=scale),
        grid=(s // tm,),
        in_specs=[
            pl.BlockSpec((tm, d), lambda i: (i, 0)), _resident(gpre.shape),
            win_spec, _resident(gq.shape), _resident(gkv.shape),
            _resident(wqt.shape), _resident(wk.shape), _resident(wvt.shape),
            pl.BlockSpec((tm, LANES), lambda i: (i, 0)),
            pl.BlockSpec((tm, LANES), lambda i: (i, 0)),
        ],
        out_specs=[
            pl.BlockSpec((tm, d), lambda i: (i, 0)),
            pl.BlockSpec((N_HEADS, QK_PAD, tm), lambda i: (0, 0, i)),
            pl.BlockSpec((N_HEADS, tm, QK_PAD), lambda i: (0, i, 0)),
            pl.BlockSpec((N_HEADS, 1, VT_PAD, tm), lambda i: (0, i // per_blk, 0, i % per_blk)),
        ],
        out_shape=[
            jax.ShapeDtypeStruct((s, d), BF16),
            jax.ShapeDtypeStruct((N_HEADS, QK_PAD, s), BF16),
            jax.ShapeDtypeStruct((N_HEADS, s, QK_PAD), BF16),
            jax.ShapeDtypeStruct((N_HEADS, s // ATTN_TK, VT_PAD, ATTN_TK), BF16),
        ],
        scratch_shapes=[pltpu.VMEM((LAT_WIDTH, d), BF16)],
        compiler_params=_params(("arbitrary",)),
        name="latent",
    )(x, gpre, w_in_t, gq, gkv, wqt, wk, wvt, cos_t, sin_t)


def _gates_kernel(h_ref, w_ref, f32a_ref, f32b_ref, f32c_ref, o_ref, bf16a_ref, bf16b_ref, bf16c_ref, w_sc,
                  *, n_silu_blocks):
    @pl.when(pl.program_id(1) == 0)
    def _():
        w_sc[...] = w_ref[...].astype(BF16)

    bf16a_ref[...] = f32a_ref[...].astype(BF16)
    bf16b_ref[...] = f32b_ref[...].astype(BF16)
    bf16c_ref[...] = f32c_ref[...].astype(BF16)

    r = _dot_nt(h_ref[...], w_sc[...])
    sg = _sigmoid(r)
    is_silu = pl.program_id(0) < n_silu_blocks
    o_ref[...] = jnp.where(is_silu, r * sg, sg).astype(BF16)


def _gates(h, w_in_t, side_casts, tm=1024, tn=1024):
    s, d = h.shape
    n_silu_blocks = MLA_WIDTH // tn
    gap = IN_G_MLA - IN_Z_MLA - MLA_WIDTH
    n_j, n_i = 3 * D_MODEL // tn, s // tm
    cast_rows = next(r for r in range(BF16_SUBLANES, d + 1, BF16_SUBLANES)
                     if d % r == 0 and d // r <= n_j * n_i)
    last_cast_block = d // cast_rows - 1
    cast_spec = pl.BlockSpec((cast_rows, d), lambda j, i: (jnp.minimum(j * n_i + i, last_cast_block), 0))
    return pl.pallas_call(
        functools.partial(_gates_kernel, n_silu_blocks=n_silu_blocks),
        grid=(n_j, n_i),
        in_specs=[pl.BlockSpec((tm, d), lambda j, i: (i, 0)),
                  _wt_rows(tn, d, lambda j: IN_Z_MLA + j * tn + jnp.where(j < n_silu_blocks, 0, gap)),
                  cast_spec, cast_spec, cast_spec],
        out_specs=[pl.BlockSpec((tm, tn), lambda j, i: (i, j)), cast_spec, cast_spec, cast_spec],
        out_shape=[jax.ShapeDtypeStruct((s, 3 * D_MODEL), BF16)]
        + [jax.ShapeDtypeStruct(w.shape, BF16) for w in side_casts],
        scratch_shapes=[pltpu.VMEM((tn, d), BF16)],
        compiler_params=_params(("arbitrary", "arbitrary")),
        name="gates",
    )(h, w_in_t, *side_casts)


def _conv_kernel(h_ref, wci_ref, wbg_ref, wcg_ref, wzc_ref, cw_ref, o_ref,
                 wci_sc, wbg_sc, wcg_sc, wzc_sc, carry_ref):
    i = pl.program_id(1)

    @pl.when(i == 0)
    def _():
        carry_ref[...] = jnp.zeros_like(carry_ref)
        wci_sc[...] = wci_ref[...].astype(BF16)
        wbg_sc[...] = wbg_ref[...].astype(BF16)
        wcg_sc[...] = wcg_ref[...].astype(BF16)
        wzc_sc[...] = wzc_ref[...].astype(BF16)

    h = h_ref[...]
    u = _dot_nt(h, wcg_sc[...]) * _dot_nt(h, wci_sc[...])
    b_gate = _dot_nt(h, wbg_sc[...])
    z_conv = _dot_nt(h, wzc_sc[...])
    prev = carry_ref[...]
    row = lax.broadcasted_iota(jnp.int32, u.shape, 0)
    u1 = jnp.where(row == 0, prev[SUBLANES - 1:SUBLANES, :], pltpu.roll(u, 1, axis=0))
    u2 = jnp.where(row == 0, prev[SUBLANES - 2:SUBLANES - 1, :],
                   jnp.where(row == 1, prev[SUBLANES - 1:SUBLANES, :], pltpu.roll(u, 2, axis=0)))
    cw = cw_ref[...]
    conv = cw[0:1, :] * u2 + cw[1:2, :] * u1 + cw[2:3, :] * u
    o_ref[...] = (b_gate * conv * (z_conv * _sigmoid(z_conv))).astype(BF16)
    carry_ref[...] = u[u.shape[0] - SUBLANES:, :]


def _conv(h, w_in_t, conv_w, tm=1024, tn=256):
    s, d = h.shape
    seg_specs = [_wt_rows(tn, d, lambda j, start=start: start + j * tn)
                 for start in (IN_C_IN, IN_B_GATE, IN_C_GATE, IN_Z_CONV)]
    return pl.pallas_call(
        _conv_kernel,
        grid=(CONV_WIDTH // tn, s // tm),
        in_specs=[pl.BlockSpec((tm, d), lambda j, i: (i, 0))] + seg_specs
        + [pl.BlockSpec((CONV_K, tn), lambda j, i: (0, j))],
        out_specs=pl.BlockSpec((tm, tn), lambda j, i: (i, j)),
        out_shape=jax.ShapeDtypeStruct((s, CONV_WIDTH), BF16),
        scratch_shapes=[pltpu.VMEM((tn, d), BF16)] * 4 + [pltpu.VMEM((SUBLANES, tn), F32)],
        compiler_params=_params(("arbitrary", "arbitrary")),
        name="conv",
    )(h, *([w_in_t] * 4), conv_w)


def _attn_kernel(qta_ref, qtb_ref, k_ref, vt_ref, o_ref, s0_ref, s1_ref, m_ref, acc_ref):
    i = pl.program_id(1)
    tk, tq = ATTN_TK, ATTN_TQ
    all_q = slice(0, tq)
    late_q = slice(tk, tq)
    heads = range(ATTN_HEADS)

    def qk(qt_ref, hd, j, s_ref, qs):
        k = k_ref[hd, pl.ds(pl.multiple_of(j * tk, tk), tk), :]
        s_ref[hd, :, qs] = _dot(k, qt_ref[hd, :, qs])

    def softmax_pv(slot, hd, j, s_ref, qs, masked):
        s = s_ref[hd, :, qs]
        if masked:
            key = lax.broadcasted_iota(jnp.int32, s.shape, 0)
            qry = lax.broadcasted_iota(jnp.int32, s.shape, 1)
            s = jnp.where(key <= qry, s, -jnp.inf)
        m_prev = m_ref[slot, hd, :, qs]
        m_new = jnp.maximum(m_prev, jnp.max(s, axis=0, keepdims=True))
        alpha = jnp.exp2(m_prev - m_new)
        p = jnp.exp2(s - m_new).astype(BF16)
        acc_ref[slot, hd, :, qs] = alpha * acc_ref[slot, hd, :, qs] + _dot(vt_ref[hd, j], p)
        m_ref[slot, hd, :, qs] = m_new

    def first_scores(qt_ref):
        for hd in heads:
            qk(qt_ref, hd, 0, s0_ref, all_q)

    def unmasked_blocks(slot, qt_ref, qi):
        def pair(j0):
            for hd in heads:
                qk(qt_ref, hd, j0 + 1, s1_ref, all_q)
                softmax_pv(slot, hd, j0, s0_ref, all_q, masked=False)
            for hd in heads:
                qk(qt_ref, hd, j0 + 2, s0_ref, all_q)
                softmax_pv(slot, hd, j0 + 1, s1_ref, all_q, masked=False)

        def two_pairs(jj, carry):
            pair(4 * jj)
            pair(4 * jj + 2)
            return carry

        lax.fori_loop(0, qi // 2, two_pairs, 0)

        @pl.when(qi % 2 == 1)
        def _():
            pair(2 * qi - 2)

    def diagonal_blocks(slot, qt_ref, qi):
        for hd in heads:
            qk(qt_ref, hd, 2 * qi + 1, s1_ref, late_q)
            softmax_pv(slot, hd, 2 * qi, s0_ref, all_q, masked=True)
        for hd in heads:
            softmax_pv(slot, hd, 2 * qi + 1, s1_ref, late_q, masked=True)
        for hd in heads:
            acc = acc_ref[slot, hd]
            out_t = acc[:V_HEAD_DIM, :] / acc[V_HEAD_DIM:V_HEAD_DIM + 1, :]
            o_ref[slot, :, hd * V_HEAD_DIM:(hd + 1) * V_HEAD_DIM] = out_t.T.astype(BF16)

    m_ref[...] = jnp.full_like(m_ref, -jnp.inf)
    acc_ref[...] = jnp.zeros_like(acc_ref)
    qi_a, qi_b = i, i + pl.num_programs(1)
    first_scores(qta_ref)
    unmasked_blocks(0, qta_ref, qi_a)
    diagonal_blocks(0, qta_ref, qi_a)
    first_scores(qtb_ref)
    unmasked_blocks(1, qtb_ref, qi_b)
    diagonal_blocks(1, qtb_ref, qi_b)


def _attention(qt, k, vt):
    nh, s, _ = k.shape
    tk, tq, hpb = ATTN_TK, ATTN_TQ, ATTN_HEADS
    n_half = s // tq // 2
    return pl.pallas_call(
        _attn_kernel,
        grid=(nh // hpb, n_half),
        in_specs=[pl.BlockSpec((hpb, QK_PAD, tq), lambda h, i: (h, 0, i)),
                  pl.BlockSpec((hpb, QK_PAD, tq), lambda h, i: (h, 0, i + n_half)),
                  pl.BlockSpec((hpb, s, QK_PAD), lambda h, i: (h, 0, 0)),
                  pl.BlockSpec((hpb, s // tk, VT_PAD, tk), lambda h, i: (h, 0, 0, 0))],
        out_specs=pl.BlockSpec((2, tq, hpb * V_HEAD_DIM), lambda h, i: (0, i, h)),
        out_shape=jax.ShapeDtypeStruct((2, s // 2, nh * V_HEAD_DIM), BF16),
        scratch_shapes=[pltpu.VMEM((hpb, tk, tq), F32), pltpu.VMEM((hpb, tk, tq), F32),
                        pltpu.VMEM((2, hpb, 1, tq), F32), pltpu.VMEM((2, hpb, VT_PAD, tq), F32)],
        compiler_params=_params(("parallel", "arbitrary")),
        name="attention",
    )(qt, qt, k, vt)


def _output_kernel(x_ref, attn_ref, sz_ref, sgm_ref, co_ref, sgc_ref,
                   womla_ref, woconv_ref, wout_ref, g_ref, o_ref):
    a = (attn_ref[...].astype(F32) * sz_ref[...].astype(F32)).astype(BF16)
    y_mla = _dot(a, womla_ref[...])
    y_conv = _dot(co_ref[...], woconv_ref[...])
    merged = sgm_ref[...].astype(F32) * y_mla + sgc_ref[...].astype(F32) * y_conv
    out = _dot(merged.astype(BF16), wout_ref[...])
    o_ref[...] = x_ref[...] + _rms(out, g_ref[...])


def _output(x, attn, gates, co, womla, woconv, wout, g, tm=256):
    s, d = x.shape
    row = lambda c: pl.BlockSpec((tm, d), lambda i: (i, c))
    return pl.pallas_call(
        _output_kernel,
        grid=(s // tm,),
        in_specs=[row(0), row(0), row(0), row(1), row(0), row(2),
                  _resident(womla.shape), _resident(woconv.shape), _resident(wout.shape),
                  _resident(g.shape)],
        out_specs=row(0),
        out_shape=jax.ShapeDtypeStruct((s, d), F32),
        compiler_params=_params(("parallel",)),
        name="output",
    )(x, attn, gates, gates, co, gates, womla, woconv, wout, g)


def _swap_halves(r):
    half = r.shape[-1] // 2
    return jnp.concatenate([r[..., half:], r[..., :half]], axis=-1)


def kernel(x, positions, pre_norm_g, w_in, q_a_norm_g, w_q_b, kv_a_norm_g, w_kv_b, conv_w,
           w_o_mla, w_o_conv, w_out, post_norm_g):
    b, s, d = x.shape
    assert b == 1 and d == D_MODEL
    x2 = x[0]

    w_in_t = w_in.T
    wq3 = w_q_b.reshape(Q_LORA_RANK, N_HEADS, QK_NOPE_DIM + ROPE_DIM)
    wq_rope = wq3[..., QK_NOPE_DIM:]
    wq = jnp.concatenate([wq3[..., :QK_NOPE_DIM], wq_rope, _swap_halves(wq_rope)], axis=-1)
    wqt = wq.reshape(Q_LORA_RANK, N_HEADS * QK_PAD).T.astype(BF16)
    wkv3 = w_kv_b.reshape(KV_LORA_RANK, N_HEADS, QK_NOPE_DIM + V_HEAD_DIM)
    wk = wkv3[..., :QK_NOPE_DIM].reshape(KV_LORA_RANK, -1).astype(BF16)
    wvt = wkv3[..., QK_NOPE_DIM:].reshape(KV_LORA_RANK, -1).T.astype(BF16)

    inv_freq = ROPE_THETA ** (-jnp.arange(0, ROPE_DIM, 2, dtype=F32) / ROPE_DIM)
    half = ROPE_DIM // 2
    lane_freq = jnp.concatenate([inv_freq, inv_freq, jnp.zeros((LANES - ROPE_DIM,), F32)])
    lane_cos = jnp.concatenate([jnp.ones((ROPE_DIM,), F32), jnp.zeros((LANES - ROPE_DIM,), F32)])
    lane_sin = jnp.concatenate([-jnp.ones((half,), F32), jnp.ones((half,), F32),
                                jnp.zeros((LANES - ROPE_DIM,), F32)])
    ang = positions[0].astype(F32)[:, None] * lane_freq
    cos_t = jnp.cos(ang) * lane_cos
    sin_t = jnp.sin(ang) * lane_sin

    h, qt, k, vt = _latent(x2, pre_norm_g.reshape(1, d), w_in_t, q_a_norm_g.reshape(1, -1),
                           kv_a_norm_g.reshape(1, -1), wqt, wk, wvt, cos_t, sin_t)
    gates, womla, woconv, wout = _gates(h, w_in_t, (w_o_mla, w_o_conv, w_out))
    co = _conv(h, w_in_t, conv_w)
    attn = _attention(qt, k, vt).reshape(s, MLA_WIDTH)
    out = _output(x2, attn, gates, co, womla, woconv, wout, post_norm_g.reshape(1, d))
    return out[None]
```

```python
import functools
import math

import jax
import jax.numpy as jnp
from jax import lax
from jax.experimental import pallas as pl
from jax.experimental.pallas import tpu as pltpu

D_MODEL = 2048
N_HEADS = 16
QK_NOPE_DIM = 128
ROPE_DIM = 64
V_HEAD_DIM = 128
Q_LORA_RANK = 512
KV_LORA_RANK = 512
MLA_WIDTH = N_HEADS * V_HEAD_DIM
CONV_WIDTH = D_MODEL
CONV_K = 3
ROPE_THETA = 10000.0
RMS_EPS = 1e-6

LANES = 128
SUBLANES = 8
QK_PAD = QK_NOPE_DIM + LANES
BF16_SUBLANES = 16
VT_PAD = V_HEAD_DIM + BF16_SUBLANES
ATTN_TK = 512
ATTN_TQ = 2 * ATTN_TK
ATTN_HEADS = 2
PAIRS_PER_TRIP = 2
HEAD_GROUP = 4
LAT_WIDTH = Q_LORA_RANK + KV_LORA_RANK + LANES
LAT_IN = Q_LORA_RANK + KV_LORA_RANK + ROPE_DIM
IN_Z_MLA = LAT_IN
IN_C_IN = IN_Z_MLA + MLA_WIDTH
IN_B_GATE = IN_C_IN + CONV_WIDTH
IN_C_GATE = IN_B_GATE + CONV_WIDTH
IN_Z_CONV = IN_C_GATE + CONV_WIDTH
IN_G_MLA = IN_Z_CONV + CONV_WIDTH
IN_G_CONV = IN_G_MLA + D_MODEL
VMEM_LIMIT = 56 * 1024 * 1024

BF16 = jnp.bfloat16
F32 = jnp.float32


def _params(semantics):
    return pltpu.CompilerParams(dimension_semantics=semantics, vmem_limit_bytes=VMEM_LIMIT)


def _resident(shape):
    return pl.BlockSpec(shape, lambda *_: (0,) * len(shape), pipeline_mode=pl.Buffered(1))


def _rms(xf, g):
    r = lax.rsqrt(jnp.mean(xf * xf, axis=-1, keepdims=True) + RMS_EPS)
    return xf * r * g


def _sigmoid(x):
    return 0.5 * jnp.tanh(0.5 * x) + 0.5


def _dot(a, b):
    return jnp.dot(a, b, preferred_element_type=F32)


def _dot_nt(a, b):
    return lax.dot_general(a, b, (((1,), (1,)), ((), ())), preferred_element_type=F32)


def _wt_rows(tn, d, first_row):
    return pl.BlockSpec((pl.Element(tn), pl.Element(d)),
                        lambda j, i: (pl.multiple_of(first_row(j), SUBLANES), 0))


def _rope128(g2, cos_t, sin_t):
    return g2 * cos_t + pltpu.roll(g2, ROPE_DIM, axis=1) * sin_t


def _latent_kernel(x_ref, gpre_ref, win_ref, gq_ref, gkv_ref, wqt_ref, wk_ref, wvt_ref, cos_ref, sin_ref,
                   h_ref, qt_ref, k_ref, vt_ref, wlat_sc, *, scale):
    @pl.when(pl.program_id(0) == 0)
    def _():
        half = ROPE_DIM // 2
        wlat_sc[:LAT_IN, :] = win_ref[...].astype(BF16)
        wlat_sc[LAT_IN:LAT_IN + half, :] = win_ref[LAT_IN - half:LAT_IN, :].astype(BF16)
        wlat_sc[LAT_IN + half:, :] = win_ref[LAT_IN - ROPE_DIM:LAT_IN - half, :].astype(BF16)

    h = _rms(x_ref[...], gpre_ref[...]).astype(BF16)
    h_ref[...] = h
    lat = _dot_nt(h, wlat_sc[...])
    qa = _rms(lat[:, :Q_LORA_RANK], gq_ref[...]).astype(BF16)
    ckv = _rms(lat[:, Q_LORA_RANK:Q_LORA_RANK + KV_LORA_RANK], gkv_ref[...]).astype(BF16)
    cos_rows = cos_ref[...]
    sin_rows = sin_ref[...]
    k_rope = _rope128(lat[:, Q_LORA_RANK + KV_LORA_RANK:], cos_rows.T, sin_rows.T).astype(BF16)
    tm = qa.shape[0]
    vt = _dot_nt(wvt_ref[...], ckv)
    ones_rows = jnp.ones((VT_PAD - V_HEAD_DIM, tm), BF16)
    for grp in range(N_HEADS // HEAD_GROUP):
        qt = _dot_nt(wqt_ref[grp * HEAD_GROUP * QK_PAD:(grp + 1) * HEAD_GROUP * QK_PAD, :], qa) * scale
        kn = _dot(ckv, wk_ref[:, grp * HEAD_GROUP * QK_NOPE_DIM:(grp + 1) * HEAD_GROUP * QK_NOPE_DIM])
        for sub in range(HEAD_GROUP):
            hd = grp * HEAD_GROUP + sub
            base = sub * QK_PAD
            qt_ref[hd, :QK_NOPE_DIM, :] = qt[base:base + QK_NOPE_DIM, :].astype(BF16)
            g2 = qt[base + QK_NOPE_DIM:base + QK_PAD, :]
            g2_swapped = jnp.concatenate([g2[ROPE_DIM:], g2[:ROPE_DIM]], axis=0)
            qt_ref[hd, QK_NOPE_DIM:, :] = (g2 * cos_rows + g2_swapped * sin_rows).astype(BF16)
            k_ref[hd, :, :QK_NOPE_DIM] = kn[:, sub * QK_NOPE_DIM:(sub + 1) * QK_NOPE_DIM].astype(BF16)
            k_ref[hd, :, QK_NOPE_DIM:] = k_rope
            vt_ref[hd, 0, :V_HEAD_DIM, :] = vt[hd * V_HEAD_DIM:(hd + 1) * V_HEAD_DIM, :].astype(BF16)
            vt_ref[hd, 0, V_HEAD_DIM:, :] = ones_rows


def _latent(x, gpre, w_in_t, gq, gkv, wqt, wk, wvt, cos_rows, sin_rows, tm=256):
    s, d = x.shape
    per_blk = ATTN_TK // tm
    win_spec = pl.BlockSpec((pl.Element(LAT_IN), pl.Element(d)), lambda i: (0, 0),
                            pipeline_mode=pl.Buffered(1))
    scale = math.log2(math.e) / math.sqrt(QK_NOPE_DIM + ROPE_DIM)
    return pl.pallas_call(
        functools.partial(_latent_kernel, scale=scale),
        grid=(s // tm,),
        in_specs=[
            pl.BlockSpec((tm, d), lambda i: (i, 0)), _resident(gpre.shape),
            win_spec, _resident(gq.shape), _resident(gkv.shape),
            _resident(wqt.shape), _resident(wk.shape), _resident(wvt.shape),
            pl.BlockSpec((LANES, tm), lambda i: (0, i)),
            pl.BlockSpec((LANES, tm), lambda i: (0, i)),
        ],
        out_specs=[
            pl.BlockSpec((tm, d), lambda i: (i, 0)),
            pl.BlockSpec((N_HEADS, QK_PAD, tm), lambda i: (0, 0, i)),
            pl.BlockSpec((N_HEADS, tm, QK_PAD), lambda i: (0, i, 0)),
            pl.BlockSpec((N_HEADS, 1, VT_PAD, tm), lambda i: (0, i // per_blk, 0, i % per_blk)),
        ],
        out_shape=[
            jax.ShapeDtypeStruct((s, d), BF16),
            jax.ShapeDtypeStruct((N_HEADS, QK_PAD, s), BF16),
            jax.ShapeDtypeStruct((N_HEADS, s, QK_PAD), BF16),
            jax.ShapeDtypeStruct((N_HEADS, s // ATTN_TK, VT_PAD, ATTN_TK), BF16),
        ],
        scratch_shapes=[pltpu.VMEM((LAT_WIDTH, d), BF16)],
        compiler_params=_params(("arbitrary",)),
        name="latent",
    )(x, gpre, w_in_t, gq, gkv, wqt, wk, wvt, cos_rows, sin_rows)


def _gates_kernel(h_ref, w_ref, f32a_ref, f32b_ref, f32c_ref, o_ref, bf16a_ref, bf16b_ref, bf16c_ref, w_sc,
                  *, n_silu_blocks):
    @pl.when(pl.program_id(1) == 0)
    def _():
        w_sc[...] = w_ref[...].astype(BF16)

    bf16a_ref[...] = f32a_ref[...].astype(BF16)
    bf16b_ref[...] = f32b_ref[...].astype(BF16)
    bf16c_ref[...] = f32c_ref[...].astype(BF16)

    r = _dot_nt(h_ref[...], w_sc[...])
    sg = _sigmoid(r)
    is_silu = pl.program_id(0) < n_silu_blocks
    o_ref[...] = jnp.where(is_silu, r * sg, sg).astype(BF16)


def _gates(h, w_in_t, side_casts, tm=1024, tn=1024):
    s, d = h.shape
    n_silu_blocks = MLA_WIDTH // tn
    gap = IN_G_MLA - IN_Z_MLA - MLA_WIDTH
    n_j, n_i = 3 * D_MODEL // tn, s // tm
    cast_rows = next(r for r in range(BF16_SUBLANES, d + 1, BF16_SUBLANES)
                     if d % r == 0 and d // r <= n_j * n_i)
    last_cast_block = d // cast_rows - 1
    cast_spec = pl.BlockSpec((cast_rows, d), lambda j, i: (jnp.minimum(j * n_i + i, last_cast_block), 0))
    return pl.pallas_call(
        functools.partial(_gates_kernel, n_silu_blocks=n_silu_blocks),
        grid=(n_j, n_i),
        in_specs=[pl.BlockSpec((tm, d), lambda j, i: (i, 0)),
                  _wt_rows(tn, d, lambda j: IN_Z_MLA + j * tn + jnp.where(j < n_silu_blocks, 0, gap)),
                  cast_spec, cast_spec, cast_spec],
        out_specs=[pl.BlockSpec((tm, tn), lambda j, i: (i, j)), cast_spec, cast_spec, cast_spec],
        out_shape=[jax.ShapeDtypeStruct((s, 3 * D_MODEL), BF16)]
        + [jax.ShapeDtypeStruct(w.shape, BF16) for w in side_casts],
        scratch_shapes=[pltpu.VMEM((tn, d), BF16)],
        compiler_params=_params(("arbitrary", "arbitrary")),
        name="gates",
    )(h, w_in_t, *side_casts)


def _conv_kernel(h_ref, wci_ref, wbg_ref, wcg_ref, wzc_ref, cw_ref, o_ref,
                 wci_sc, wbg_sc, wcg_sc, wzc_sc, carry_ref):
    i = pl.program_id(1)

    @pl.when(i == 0)
    def _():
        carry_ref[...] = jnp.zeros_like(carry_ref)
        wci_sc[...] = wci_ref[...].astype(BF16)
        wbg_sc[...] = wbg_ref[...].astype(BF16)
        wcg_sc[...] = wcg_ref[...].astype(BF16)
        wzc_sc[...] = wzc_ref[...].astype(BF16)

    h = h_ref[...]
    u = _dot_nt(h, wcg_sc[...]) * _dot_nt(h, wci_sc[...])
    b_gate = _dot_nt(h, wbg_sc[...])
    z_conv = _dot_nt(h, wzc_sc[...])
    prev = carry_ref[...]
    row = lax.broadcasted_iota(jnp.int32, u.shape, 0)
    u1 = jnp.where(row == 0, prev[SUBLANES - 1:SUBLANES, :], pltpu.roll(u, 1, axis=0))
    u2 = jnp.where(row == 0, prev[SUBLANES - 2:SUBLANES - 1, :],
                   jnp.where(row == 1, prev[SUBLANES - 1:SUBLANES, :], pltpu.roll(u, 2, axis=0)))
    cw = cw_ref[...]
    conv = cw[0:1, :] * u2 + cw[1:2, :] * u1 + cw[2:3, :] * u
    o_ref[...] = (b_gate * conv * (z_conv * _sigmoid(z_conv))).astype(BF16)
    carry_ref[...] = u[u.shape[0] - SUBLANES:, :]


def _conv(h, w_in_t, conv_w, tm=2048, tn=256):
    s, d = h.shape
    seg_specs = [_wt_rows(tn, d, lambda j, start=start: start + j * tn)
                 for start in (IN_C_IN, IN_B_GATE, IN_C_GATE, IN_Z_CONV)]
    return pl.pallas_call(
        _conv_kernel,
        grid=(CONV_WIDTH // tn, s // tm),
        in_specs=[pl.BlockSpec((tm, d), lambda j, i: (i, 0))] + seg_specs
        + [pl.BlockSpec((CONV_K, tn), lambda j, i: (0, j))],
        out_specs=pl.BlockSpec((tm, tn), lambda j, i: (i, j)),
        out_shape=jax.ShapeDtypeStruct((s, CONV_WIDTH), BF16),
        scratch_shapes=[pltpu.VMEM((tn, d), BF16)] * 4 + [pltpu.VMEM((SUBLANES, tn), F32)],
        compiler_params=_params(("arbitrary", "arbitrary")),
        name="conv",
    )(h, *([w_in_t] * 4), conv_w)


def _attn_kernel(qta_ref, qtb_ref, k_ref, vt_ref, o_ref, s0_ref, s1_ref, m_ref, acc_ref):
    i = pl.program_id(1)
    tk, tq = ATTN_TK, ATTN_TQ
    all_q = slice(0, tq)
    late_q = slice(tk, tq)
    heads = range(ATTN_HEADS)

    def qk(qt_ref, hd, j, s_ref, qs):
        k = k_ref[hd, pl.ds(pl.multiple_of(j * tk, tk), tk), :]
        s_ref[hd, :, qs] = _dot(k, qt_ref[hd, :, qs])

    def softmax_pv(slot, hd, j, s_ref, qs, masked):
        s = s_ref[hd, :, qs]
        if masked:
            key = lax.broadcasted_iota(jnp.int32, s.shape, 0)
            qry = lax.broadcasted_iota(jnp.int32, s.shape, 1)
            s = jnp.where(key <= qry, s, -jnp.inf)
        m_prev = m_ref[slot, hd, :, qs]
        m_new = jnp.maximum(m_prev, jnp.max(s, axis=0, keepdims=True))
        alpha = jnp.exp2(m_prev - m_new)
        p = jnp.exp2(s - m_new).astype(BF16)
        acc_ref[slot, hd, :, qs] = alpha * acc_ref[slot, hd, :, qs] + _dot(vt_ref[hd, j], p)
        m_ref[slot, hd, :, qs] = m_new

    def first_scores(qt_ref):
        for hd in heads:
            qk(qt_ref, hd, 0, s0_ref, all_q)

    def unmasked_blocks(slot, qt_ref, qi):
        def pair(j0):
            for hd in heads:
                qk(qt_ref, hd, j0 + 1, s1_ref, all_q)
                softmax_pv(slot, hd, j0, s0_ref, all_q, masked=False)
            for hd in heads:
                qk(qt_ref, hd, j0 + 2, s0_ref, all_q)
                softmax_pv(slot, hd, j0 + 1, s1_ref, all_q, masked=False)

        def trip(jj, carry):
            for sub in range(PAIRS_PER_TRIP):
                pair(2 * (PAIRS_PER_TRIP * jj + sub))
            return carry

        lax.fori_loop(0, qi // PAIRS_PER_TRIP, trip, 0)
        done = qi - qi % PAIRS_PER_TRIP
        for extra in range(PAIRS_PER_TRIP - 1):
            @pl.when(qi % PAIRS_PER_TRIP > extra)
            def _(extra=extra):
                pair(2 * (done + extra))

    def diagonal_blocks(slot, qt_ref, qi):
        for hd in heads:
            qk(qt_ref, hd, 2 * qi + 1, s1_ref, late_q)
            softmax_pv(slot, hd, 2 * qi, s0_ref, all_q, masked=True)
        for hd in heads:
            softmax_pv(slot, hd, 2 * qi + 1, s1_ref, late_q, masked=True)
        for hd in heads:
            acc = acc_ref[slot, hd]
            out_t = acc[:V_HEAD_DIM, :] / acc[V_HEAD_DIM:V_HEAD_DIM + 1, :]
            o_ref[slot, :, hd * V_HEAD_DIM:(hd + 1) * V_HEAD_DIM] = out_t.T.astype(BF16)

    m_ref[...] = jnp.full_like(m_ref, -jnp.inf)
    acc_ref[...] = jnp.zeros_like(acc_ref)
    qi_a, qi_b = i, i + pl.num_programs(1)
    first_scores(qta_ref)
    unmasked_blocks(0, qta_ref, qi_a)
    diagonal_blocks(0, qta_ref, qi_a)
    first_scores(qtb_ref)
    unmasked_blocks(1, qtb_ref, qi_b)
    diagonal_blocks(1, qtb_ref, qi_b)


def _attention(qt, k, vt):
    nh, s, _ = k.shape
    tk, tq, hpb = ATTN_TK, ATTN_TQ, ATTN_HEADS
    n_half = s // tq // 2
    return pl.pallas_call(
        _attn_kernel,
        grid=(nh // hpb, n_half),
        in_specs=[pl.BlockSpec((hpb, QK_PAD, tq), lambda h, i: (h, 0, i)),
                  pl.BlockSpec((hpb, QK_PAD, tq), lambda h, i: (h, 0, i + n_half)),
                  pl.BlockSpec((hpb, s, QK_PAD), lambda h, i: (h, 0, 0)),
                  pl.BlockSpec((hpb, s // tk, VT_PAD, tk), lambda h, i: (h, 0, 0, 0))],
        out_specs=pl.BlockSpec((2, tq, hpb * V_HEAD_DIM), lambda h, i: (0, i, h)),
        out_shape=jax.ShapeDtypeStruct((2, s // 2, nh * V_HEAD_DIM), BF16),
        scratch_shapes=[pltpu.VMEM((hpb, tk, tq), F32), pltpu.VMEM((hpb, tk, tq), F32),
                        pltpu.VMEM((2, hpb, 1, tq), F32), pltpu.VMEM((2, hpb, VT_PAD, tq), F32)],
        compiler_params=_params(("parallel", "arbitrary")),
        name="attention",
    )(qt, qt, k, vt)


def _output_kernel(x_ref, attn_ref, sz_ref, sgm_ref, co_ref, sgc_ref,
                   womla_ref, woconv_ref, wout_ref, g_ref, o_ref):
    a = (attn_ref[...].astype(F32) * sz_ref[...].astype(F32)).astype(BF16)
    y_mla = _dot(a, womla_ref[...])
    y_conv = _dot(co_ref[...], woconv_ref[...])
    merged = sgm_ref[...].astype(F32) * y_mla + sgc_ref[...].astype(F32) * y_conv
    out = _dot(merged.astype(BF16), wout_ref[...])
    o_ref[...] = x_ref[...] + _rms(out, g_ref[...])


def _output(x, attn, gates, co, womla, woconv, wout, g, tm=256):
    s, d = x.shape
    row = lambda c: pl.BlockSpec((tm, d), lambda i: (i, c))
    return pl.pallas_call(
        _output_kernel,
        grid=(s // tm,),
        in_specs=[row(0), row(0), row(0), row(1), row(0), row(2),
                  _resident(womla.shape), _resident(woconv.shape), _resident(wout.shape),
                  _resident(g.shape)],
        out_specs=row(0),
        out_shape=jax.ShapeDtypeStruct((s, d), F32),
        compiler_params=_params(("parallel",)),
        name="output",
    )(x, attn, gates, gates, co, gates, womla, woconv, wout, g)


def _swap_halves(r):
    half = r.shape[-1] // 2
    return jnp.concatenate([r[..., half:], r[..., :half]], axis=-1)


def kernel(x, positions, pre_norm_g, w_in, q_a_norm_g, w_q_b, kv_a_norm_g, w_kv_b, conv_w,
           w_o_mla, w_o_conv, w_out, post_norm_g):
    b, s, d = x.shape
    assert b == 1 and d == D_MODEL
    x2 = x[0]

    w_in_t = w_in.T
    wq3 = w_q_b.reshape(Q_LORA_RANK, N_HEADS, QK_NOPE_DIM + ROPE_DIM)
    wq_rope = wq3[..., QK_NOPE_DIM:]
    wq = jnp.concatenate([wq3[..., :QK_NOPE_DIM], wq_rope, _swap_halves(wq_rope)], axis=-1)
    wqt = wq.reshape(Q_LORA_RANK, N_HEADS * QK_PAD).T.astype(BF16)
    wkv3 = w_kv_b.reshape(KV_LORA_RANK, N_HEADS, QK_NOPE_DIM + V_HEAD_DIM)
    wk = wkv3[..., :QK_NOPE_DIM].reshape(KV_LORA_RANK, -1).astype(BF16)
    wvt = wkv3[..., QK_NOPE_DIM:].reshape(KV_LORA_RANK, -1).T.astype(BF16)

    inv_freq = ROPE_THETA ** (-jnp.arange(0, ROPE_DIM, 2, dtype=F32) / ROPE_DIM)
    ang = inv_freq[:, None] * positions[0].astype(F32)[None, :]
    cos, sin = jnp.cos(ang), jnp.sin(ang)
    zero_rows = jnp.zeros((LANES - ROPE_DIM, s), F32)
    cos_rows = jnp.concatenate([cos, cos, zero_rows], axis=0)
    sin_rows = jnp.concatenate([-sin, sin, zero_rows], axis=0)

    h, qt, k, vt = _latent(x2, pre_norm_g.reshape(1, d), w_in_t, q_a_norm_g.reshape(1, -1),
                           kv_a_norm_g.reshape(1, -1), wqt, wk, wvt, cos_rows, sin_rows)
    gates, womla, woconv, wout = _gates(h, w_in_t, (w_o_mla, w_o_conv, w_out))
    co = _conv(h, w_in_t, conv_w)
    attn = _attention(qt, k, vt).reshape(s, MLA_WIDTH)
    out = _output(x2, attn, gates, co, womla, woconv, wout, post_norm_g.reshape(1, d))
    return out[None]
```

```python
import functools
import math

import jax
import jax.numpy as jnp
from jax import lax
from jax.experimental import pallas as pl
from jax.experimental.pallas import tpu as pltpu

D_MODEL = 2048
N_HEADS = 16
QK_NOPE_DIM = 128
ROPE_DIM = 64
V_HEAD_DIM = 128
Q_LORA_RANK = 512
KV_LORA_RANK = 512
MLA_WIDTH = N_HEADS * V_HEAD_DIM
CONV_WIDTH = D_MODEL
CONV_K = 3
ROPE_THETA = 10000.0
RMS_EPS = 1e-6

LANES = 128
SUBLANES = 8
QK_PAD = QK_NOPE_DIM + LANES
BF16_SUBLANES = 16
VT_PAD = V_HEAD_DIM + BF16_SUBLANES
ATTN_TK = 512
ATTN_TQ = 2 * ATTN_TK
ATTN_HEADS = 2
PAIRS_PER_TRIP = 2
HEAD_GROUP = 4
LAT_WIDTH = Q_LORA_RANK + KV_LORA_RANK + LANES
LAT_IN = Q_LORA_RANK + KV_LORA_RANK + ROPE_DIM
IN_Z_MLA = LAT_IN
IN_C_IN = IN_Z_MLA + MLA_WIDTH
IN_B_GATE = IN_C_IN + CONV_WIDTH
IN_C_GATE = IN_B_GATE + CONV_WIDTH
IN_Z_CONV = IN_C_GATE + CONV_WIDTH
IN_G_MLA = IN_Z_CONV + CONV_WIDTH
IN_G_CONV = IN_G_MLA + D_MODEL
VMEM_LIMIT = 56 * 1024 * 1024

BF16 = jnp.bfloat16
F32 = jnp.float32


def _params(semantics):
    return pltpu.CompilerParams(dimension_semantics=semantics, vmem_limit_bytes=VMEM_LIMIT)


def _resident(shape):
    return pl.BlockSpec(shape, lambda *_: (0,) * len(shape), pipeline_mode=pl.Buffered(1))


def _rms(xf, g):
    r = lax.rsqrt(jnp.mean(xf * xf, axis=-1, keepdims=True) + RMS_EPS)
    return xf * r * g


def _sigmoid(x):
    return 0.5 * jnp.tanh(0.5 * x) + 0.5


def _dot(a, b):
    return jnp.dot(a, b, preferred_element_type=F32)


def _dot_nt(a, b):
    return lax.dot_general(a, b, (((1,), (1,)), ((), ())), preferred_element_type=F32)


def _wt_rows(tn, d, first_row):
    return pl.BlockSpec((pl.Element(tn), pl.Element(d)),
                        lambda j, i: (pl.multiple_of(first_row(j), SUBLANES), 0))


def _rope128(g2, cos_t, sin_t):
    return g2 * cos_t + pltpu.roll(g2, ROPE_DIM, axis=1) * sin_t


def _latent_kernel(x_ref, gpre_ref, win_ref, gq_ref, gkv_ref, wqt_ref, wk_ref, wvt_ref, cos_ref, sin_ref,
                   h_ref, qt_ref, k_ref, vt_ref, wlat_sc, *, scale):
    @pl.when(pl.program_id(0) == 0)
    def _():
        half = ROPE_DIM // 2
        for c in range(0, LAT_IN - ROPE_DIM, LANES):
            wlat_sc[:, c:c + LANES] = win_ref[c:c + LANES, :].T.astype(BF16)
        rope_rows = jnp.concatenate([win_ref[LAT_IN - ROPE_DIM:LAT_IN, :], win_ref[LAT_IN - half:LAT_IN, :],
                                     win_ref[LAT_IN - ROPE_DIM:LAT_IN - half, :]], axis=0)
        wlat_sc[:, LAT_IN - ROPE_DIM:] = rope_rows.T.astype(BF16)

    h = _rms(x_ref[...], gpre_ref[...]).astype(BF16)
    h_ref[...] = h
    lat = _dot(h, wlat_sc[...])
    qa = _rms(lat[:, :Q_LORA_RANK], gq_ref[...]).astype(BF16)
    ckv = _rms(lat[:, Q_LORA_RANK:Q_LORA_RANK + KV_LORA_RANK], gkv_ref[...]).astype(BF16)
    cos_rows = cos_ref[...]
    sin_rows = sin_ref[...]
    k_rope = _rope128(lat[:, Q_LORA_RANK + KV_LORA_RANK:], cos_rows.T, sin_rows.T).astype(BF16)
    tm = qa.shape[0]
    vt = _dot_nt(wvt_ref[...], ckv)
    ones_rows = jnp.ones((VT_PAD - V_HEAD_DIM, tm), BF16)
    for grp in range(N_HEADS // HEAD_GROUP):
        qt = _dot_nt(wqt_ref[grp * HEAD_GROUP * QK_PAD:(grp + 1) * HEAD_GROUP * QK_PAD, :], qa) * scale
        kn = _dot(ckv, wk_ref[:, grp * HEAD_GROUP * QK_NOPE_DIM:(grp + 1) * HEAD_GROUP * QK_NOPE_DIM])
        for sub in range(HEAD_GROUP):
            hd = grp * HEAD_GROUP + sub
            base = sub * QK_PAD
            qt_ref[hd, :QK_NOPE_DIM, :] = qt[base:base + QK_NOPE_DIM, :].astype(BF16)
            g2 = qt[base + QK_NOPE_DIM:base + QK_PAD, :]
            g2_swapped = jnp.concatenate([g2[ROPE_DIM:], g2[:ROPE_DIM]], axis=0)
            qt_ref[hd, QK_NOPE_DIM:, :] = (g2 * cos_rows + g2_swapped * sin_rows).astype(BF16)
            k_ref[hd, :, :QK_NOPE_DIM] = kn[:, sub * QK_NOPE_DIM:(sub + 1) * QK_NOPE_DIM].astype(BF16)
            k_ref[hd, :, QK_NOPE_DIM:] = k_rope
            vt_ref[hd, 0, :V_HEAD_DIM, :] = vt[hd * V_HEAD_DIM:(hd + 1) * V_HEAD_DIM, :].astype(BF16)
            vt_ref[hd, 0, V_HEAD_DIM:, :] = ones_rows


def _latent(x, gpre, w_in_t, gq, gkv, wqt, wk, wvt, cos_rows, sin_rows, tm=256):
    s, d = x.shape
    per_blk = ATTN_TK // tm
    win_spec = pl.BlockSpec((pl.Element(LAT_IN), pl.Element(d)), lambda i: (0, 0),
                            pipeline_mode=pl.Buffered(1))
    scale = math.log2(math.e) / math.sqrt(QK_NOPE_DIM + ROPE_DIM)
    return pl.pallas_call(
        functools.partial(_latent_kernel, scale=scale),
        grid=(s // tm,),
        in_specs=[
            pl.BlockSpec((tm, d), lambda i: (i, 0)), _resident(gpre.shape),
            win_spec, _resident(gq.shape), _resident(gkv.shape),
            _resident(wqt.shape), _resident(wk.shape), _resident(wvt.shape),
            pl.BlockSpec((LANES, tm), lambda i: (0, i)),
            pl.BlockSpec((LANES, tm), lambda i: (0, i)),
        ],
        out_specs=[
            pl.BlockSpec((tm, d), lambda i: (i, 0)),
            pl.BlockSpec((N_HEADS, QK_PAD, tm), lambda i: (0, 0, i)),
            pl.BlockSpec((N_HEADS, tm, QK_PAD), lambda i: (0, i, 0)),
            pl.BlockSpec((N_HEADS, 1, VT_PAD, tm), lambda i: (0, i // per_blk, 0, i % per_blk)),
        ],
        out_shape=[
            jax.ShapeDtypeStruct((s, d), BF16),
            jax.ShapeDtypeStruct((N_HEADS, QK_PAD, s), BF16),
            jax.ShapeDtypeStruct((N_HEADS, s, QK_PAD), BF16),
            jax.ShapeDtypeStruct((N_HEADS, s // ATTN_TK, VT_PAD, ATTN_TK), BF16),
        ],
        scratch_shapes=[pltpu.VMEM((d, LAT_WIDTH), BF16)],
        compiler_params=_params(("arbitrary",)),
        name="latent",
    )(x, gpre, w_in_t, gq, gkv, wqt, wk, wvt, cos_rows, sin_rows)


def _gates_kernel(h_ref, w_ref, f32a_ref, f32b_ref, f32c_ref, o_ref, bf16a_ref, bf16b_ref, bf16c_ref, w_sc,
                  *, n_silu_blocks):
    @pl.when(pl.program_id(1) == 0)
    def _():
        w_sc[...] = w_ref[...].astype(BF16)

    bf16a_ref[...] = f32a_ref[...].astype(BF16)
    bf16b_ref[...] = f32b_ref[...].astype(BF16)
    bf16c_ref[...] = f32c_ref[...].astype(BF16)

    r = _dot_nt(h_ref[...], w_sc[...])
    sg = _sigmoid(r)
    is_silu = pl.program_id(0) < n_silu_blocks
    o_ref[...] = jnp.where(is_silu, r * sg, sg).astype(BF16)


def _gates(h, w_in_t, side_casts, tm=1024, tn=1024):
    s, d = h.shape
    n_silu_blocks = MLA_WIDTH // tn
    gap = IN_G_MLA - IN_Z_MLA - MLA_WIDTH
    n_j, n_i = 3 * D_MODEL // tn, s // tm
    cast_rows = next(r for r in range(BF16_SUBLANES, d + 1, BF16_SUBLANES)
                     if d % r == 0 and d // r <= n_j * n_i)
    last_cast_block = d // cast_rows - 1
    cast_spec = pl.BlockSpec((cast_rows, d), lambda j, i: (jnp.minimum(j * n_i + i, last_cast_block), 0))
    return pl.pallas_call(
        functools.partial(_gates_kernel, n_silu_blocks=n_silu_blocks),
        grid=(n_j, n_i),
        in_specs=[pl.BlockSpec((tm, d), lambda j, i: (i, 0)),
                  _wt_rows(tn, d, lambda j: IN_Z_MLA + j * tn + jnp.where(j < n_silu_blocks, 0, gap)),
                  cast_spec, cast_spec, cast_spec],
        out_specs=[pl.BlockSpec((tm, tn), lambda j, i: (i, j)), cast_spec, cast_spec, cast_spec],
        out_shape=[jax.ShapeDtypeStruct((s, 3 * D_MODEL), BF16)]
        + [jax.ShapeDtypeStruct(w.shape, BF16) for w in side_casts],
        scratch_shapes=[pltpu.VMEM((tn, d), BF16)],
        compiler_params=_params(("arbitrary", "arbitrary")),
        name="gates",
    )(h, w_in_t, *side_casts)


def _conv_kernel(h_ref, wci_ref, wbg_ref, wcg_ref, wzc_ref, cw_ref, o_ref,
                 wci_sc, wbg_sc, wcg_sc, wzc_sc, carry_ref):
    i = pl.program_id(1)

    @pl.when(i == 0)
    def _():
        carry_ref[...] = jnp.zeros_like(carry_ref)
        wci_sc[...] = wci_ref[...].astype(BF16)
        wbg_sc[...] = wbg_ref[...].astype(BF16)
        wcg_sc[...] = wcg_ref[...].astype(BF16)
        wzc_sc[...] = wzc_ref[...].astype(BF16)

    h = h_ref[...]
    u = _dot_nt(h, wcg_sc[...]) * _dot_nt(h, wci_sc[...])
    b_gate = _dot_nt(h, wbg_sc[...])
    z_conv = _dot_nt(h, wzc_sc[...])
    prev = carry_ref[...]
    row = lax.broadcasted_iota(jnp.int32, u.shape, 0)
    u1 = jnp.where(row == 0, prev[SUBLANES - 1:SUBLANES, :], pltpu.roll(u, 1, axis=0))
    u2 = jnp.where(row == 0, prev[SUBLANES - 2:SUBLANES - 1, :],
                   jnp.where(row == 1, prev[SUBLANES - 1:SUBLANES, :], pltpu.roll(u, 2, axis=0)))
    cw = cw_ref[...]
    conv = cw[0:1, :] * u2 + cw[1:2, :] * u1 + cw[2:3, :] * u
    o_ref[...] = (b_gate * conv * (z_conv * _sigmoid(z_conv))).astype(BF16)
    carry_ref[...] = u[u.shape[0] - SUBLANES:, :]


def _conv(h, w_in_t, conv_w, tm=2048, tn=256):
    s, d = h.shape
    seg_specs = [_wt_rows(tn, d, lambda j, start=start: start + j * tn)
                 for start in (IN_C_IN, IN_B_GATE, IN_C_GATE, IN_Z_CONV)]
    return pl.pallas_call(
        _conv_kernel,
        grid=(CONV_WIDTH // tn, s // tm),
        in_specs=[pl.BlockSpec((tm, d), lambda j, i: (i, 0))] + seg_specs
        + [pl.BlockSpec((CONV_K, tn), lambda j, i: (0, j))],
        out_specs=pl.BlockSpec((tm, tn), lambda j, i: (i, j)),
        out_shape=jax.ShapeDtypeStruct((s, CONV_WIDTH), BF16),
        scratch_shapes=[pltpu.VMEM((tn, d), BF16)] * 4 + [pltpu.VMEM((SUBLANES, tn), F32)],
        compiler_params=_params(("arbitrary", "arbitrary")),
        name="conv",
    )(h, *([w_in_t] * 4), conv_w)


def _attn_kernel(qta_ref, qtb_ref, k_ref, vt_ref, o_ref, s0_ref, s1_ref, m_ref, acc_ref):
    i = pl.program_id(1)
    tk, tq = ATTN_TK, ATTN_TQ
    all_q = slice(0, tq)
    late_q = slice(tk, tq)
    heads = range(ATTN_HEADS)

    def qk(qt_ref, hd, j, s_ref, qs):
        k = k_ref[hd, pl.ds(pl.multiple_of(j * tk, tk), tk), :]
        s_ref[hd, :, qs] = _dot(k, qt_ref[hd, :, qs])

    def softmax_pv(slot, hd, j, s_ref, qs, masked):
        s = s_ref[hd, :, qs]
        if masked:
            key = lax.broadcasted_iota(jnp.int32, s.shape, 0)
            qry = lax.broadcasted_iota(jnp.int32, s.shape, 1)
            s = jnp.where(key <= qry, s, -jnp.inf)
        m_prev = m_ref[slot, hd, :, qs]
        m_new = jnp.maximum(m_prev, jnp.max(s, axis=0, keepdims=True))
        alpha = jnp.exp2(m_prev - m_new)
        p = jnp.exp2(s - m_new).astype(BF16)
        acc_ref[slot, hd, :, qs] = alpha * acc_ref[slot, hd, :, qs] + _dot(vt_ref[hd, j], p)
        m_ref[slot, hd, :, qs] = m_new

    def first_scores(qt_ref):
        for hd in heads:
            qk(qt_ref, hd, 0, s0_ref, all_q)

    def unmasked_blocks(slot, qt_ref, qi):
        def pair(j0):
            for hd in heads:
                qk(qt_ref, hd, j0 + 1, s1_ref, all_q)
                softmax_pv(slot, hd, j0, s0_ref, all_q, masked=False)
            for hd in heads:
                qk(qt_ref, hd, j0 + 2, s0_ref, all_q)
                softmax_pv(slot, hd, j0 + 1, s1_ref, all_q, masked=False)

        def trip(jj, carry):
            for sub in range(PAIRS_PER_TRIP):
                pair(2 * (PAIRS_PER_TRIP * jj + sub))
            return carry

        lax.fori_loop(0, qi // PAIRS_PER_TRIP, trip, 0)
        done = qi - qi % PAIRS_PER_TRIP
        for extra in range(PAIRS_PER_TRIP - 1):
            @pl.when(qi % PAIRS_PER_TRIP > extra)
            def _(extra=extra):
                pair(2 * (done + extra))

    def diagonal_blocks(slot, qt_ref, qi):
        for hd in heads:
            qk(qt_ref, hd, 2 * qi + 1, s1_ref, late_q)
            softmax_pv(slot, hd, 2 * qi, s0_ref, all_q, masked=True)
        for hd in heads:
            softmax_pv(slot, hd, 2 * qi + 1, s1_ref, late_q, masked=True)
        for hd in heads:
            acc = acc_ref[slot, hd]
            out_t = acc[:V_HEAD_DIM, :] / acc[V_HEAD_DIM:V_HEAD_DIM + 1, :]
            o_ref[slot, :, hd * V_HEAD_DIM:(hd + 1) * V_HEAD_DIM] = out_t.T.astype(BF16)

    m_ref[...] = jnp.full_like(m_ref, -jnp.inf)
    acc_ref[...] = jnp.zeros_like(acc_ref)
    qi_a, qi_b = i, i + pl.num_programs(1)
    first_scores(qta_ref)
    unmasked_blocks(0, qta_ref, qi_a)
    diagonal_blocks(0, qta_ref, qi_a)
    first_scores(qtb_ref)
    unmasked_blocks(1, qtb_ref, qi_b)
    diagonal_blocks(1, qtb_ref, qi_b)


def _attention(qt, k, vt):
    nh, s, _ = k.shape
    tk, tq, hpb = ATTN_TK, ATTN_TQ, ATTN_HEADS
    n_half = s // tq // 2
    return pl.pallas_call(
        _attn_kernel,
        grid=(nh // hpb, n_half),
        in_specs=[pl.BlockSpec((hpb, QK_PAD, tq), lambda h, i: (h, 0, i)),
                  pl.BlockSpec((hpb, QK_PAD, tq), lambda h, i: (h, 0, i + n_half)),
                  pl.BlockSpec((hpb, s, QK_PAD), lambda h, i: (h, 0, 0)),
                  pl.BlockSpec((hpb, s // tk, VT_PAD, tk), lambda h, i: (h, 0, 0, 0))],
        out_specs=pl.BlockSpec((2, tq, hpb * V_HEAD_DIM), lambda h, i: (0, i, h)),
        out_shape=jax.ShapeDtypeStruct((2, s // 2, nh * V_HEAD_DIM), BF16),
        scratch_shapes=[pltpu.VMEM((hpb, tk, tq), F32), pltpu.VMEM((hpb, tk, tq), F32),
                        pltpu.VMEM((2, hpb, 1, tq), F32), pltpu.VMEM((2, hpb, VT_PAD, tq), F32)],
        compiler_params=_params(("parallel", "arbitrary")),
        name="attention",
    )(qt, qt, k, vt)


def _output_kernel(x_ref, attn_ref, sz_ref, sgm_ref, co_ref, sgc_ref,
                   womla_ref, woconv_ref, wout_ref, g_ref, o_ref):
    a = (attn_ref[...].astype(F32) * sz_ref[...].astype(F32)).astype(BF16)
    y_mla = _dot(a, womla_ref[...])
    y_conv = _dot(co_ref[...], woconv_ref[...])
    merged = sgm_ref[...].astype(F32) * y_mla + sgc_ref[...].astype(F32) * y_conv
    out = _dot(merged.astype(BF16), wout_ref[...])
    o_ref[...] = x_ref[...] + _rms(out, g_ref[...])


def _output(x, attn, gates, co, womla, woconv, wout, g, tm=256):
    s, d = x.shape
    row = lambda c: pl.BlockSpec((tm, d), lambda i: (i, c))
    return pl.pallas_call(
        _output_kernel,
        grid=(s // tm,),
        in_specs=[row(0), row(0), row(0), row(1), row(0), row(2),
                  _resident(womla.shape), _resident(woconv.shape), _resident(wout.shape),
                  _resident(g.shape)],
        out_specs=row(0),
        out_shape=jax.ShapeDtypeStruct((s, d), F32),
        compiler_params=_params(("parallel",)),
        name="output",
    )(x, attn, gates, gates, co, gates, womla, woconv, wout, g)


def _swap_halves(r):
    half = r.shape[-1] // 2
    return jnp.concatenate([r[..., half:], r[..., :half]], axis=-1)


def kernel(x, positions, pre_norm_g, w_in, q_a_norm_g, w_q_b, kv_a_norm_g, w_kv_b, conv_w,
           w_o_mla, w_o_conv, w_out, post_norm_g):
    b, s, d = x.shape
    assert b == 1 and d == D_MODEL
    x2 = x[0]

    w_in_t = w_in.T
    wq3 = w_q_b.reshape(Q_LORA_RANK, N_HEADS, QK_NOPE_DIM + ROPE_DIM)
    wq_rope = wq3[..., QK_NOPE_DIM:]
    wq = jnp.concatenate([wq3[..., :QK_NOPE_DIM], wq_rope, _swap_halves(wq_rope)], axis=-1)
    wqt = wq.reshape(Q_LORA_RANK, N_HEADS * QK_PAD).T.astype(BF16)
    wkv3 = w_kv_b.reshape(KV_LORA_RANK, N_HEADS, QK_NOPE_DIM + V_HEAD_DIM)
    wk = wkv3[..., :QK_NOPE_DIM].reshape(KV_LORA_RANK, -1).astype(BF16)
    wvt = wkv3[..., QK_NOPE_DIM:].reshape(KV_LORA_RANK, -1).T.astype(BF16)

    inv_freq = ROPE_THETA ** (-jnp.arange(0, ROPE_DIM, 2, dtype=F32) / ROPE_DIM)
    ang = inv_freq[:, None] * positions[0].astype(F32)[None, :]
    cos, sin = jnp.cos(ang), jnp.sin(ang)
    zero_rows = jnp.zeros((LANES - ROPE_DIM, s), F32)
    cos_rows = jnp.concatenate([cos, cos, zero_rows], axis=0)
    sin_rows = jnp.concatenate([-sin, sin, zero_rows], axis=0)

    h, qt, k, vt = _latent(x2, pre_norm_g.reshape(1, d), w_in_t, q_a_norm_g.reshape(1, -1),
                           kv_a_norm_g.reshape(1, -1), wqt, wk, wvt, cos_rows, sin_rows)
    gates, womla, woconv, wout = _gates(h, w_in_t, (w_o_mla, w_o_conv, w_out))
    co = _conv(h, w_in_t, conv_w)
    attn = _attention(qt, k, vt).reshape(s, MLA_WIDTH)
    out = _output(x2, attn, gates, co, womla, woconv, wout, post_norm_g.reshape(1, d))
    return out[None]
```

```python
import functools
import math

import jax
import jax.numpy as jnp
from jax import lax
from jax.experimental import pallas as pl
from jax.experimental.pallas import tpu as pltpu

D_MODEL = 2048
N_HEADS = 16
QK_NOPE_DIM = 128
ROPE_DIM = 64
V_HEAD_DIM = 128
Q_LORA_RANK = 512
KV_LORA_RANK = 512
MLA_WIDTH = N_HEADS * V_HEAD_DIM
CONV_WIDTH = D_MODEL
CONV_K = 3
ROPE_THETA = 10000.0
RMS_EPS = 1e-6

LANES = 128
SUBLANES = 8
QK_PAD = QK_NOPE_DIM + LANES
BF16_SUBLANES = 16
VT_PAD = V_HEAD_DIM + BF16_SUBLANES
ATTN_TK = 512
ATTN_TQ = 2 * ATTN_TK
ATTN_HEADS = 2
PAIRS_PER_TRIP = 2
HEAD_GROUP = 4
LAT_WIDTH = Q_LORA_RANK + KV_LORA_RANK + LANES
LAT_IN = Q_LORA_RANK + KV_LORA_RANK + ROPE_DIM
IN_Z_MLA = LAT_IN
IN_C_IN = IN_Z_MLA + MLA_WIDTH
IN_B_GATE = IN_C_IN + CONV_WIDTH
IN_C_GATE = IN_B_GATE + CONV_WIDTH
IN_Z_CONV = IN_C_GATE + CONV_WIDTH
IN_G_MLA = IN_Z_CONV + CONV_WIDTH
IN_G_CONV = IN_G_MLA + D_MODEL
VMEM_LIMIT = 56 * 1024 * 1024

BF16 = jnp.bfloat16
F32 = jnp.float32


def _params(semantics):
    return pltpu.CompilerParams(dimension_semantics=semantics, vmem_limit_bytes=VMEM_LIMIT)


def _resident(shape):
    return pl.BlockSpec(shape, lambda *_: (0,) * len(shape), pipeline_mode=pl.Buffered(1))


def _rms(xf, g):
    r = lax.rsqrt(jnp.mean(xf * xf, axis=-1, keepdims=True) + RMS_EPS)
    return xf * r * g


def _sigmoid(x):
    return 0.5 * jnp.tanh(0.5 * x) + 0.5


def _dot(a, b):
    return jnp.dot(a, b, preferred_element_type=F32)


def _dot_nt(a, b):
    return lax.dot_general(a, b, (((1,), (1,)), ((), ())), preferred_element_type=F32)


def _transpose_cast(dst_sc, src_ref, n_rows):
    for c in range(0, n_rows, LANES):
        dst_sc[:, c:c + LANES] = src_ref[c:c + LANES, :].T.astype(BF16)


def _wt_rows(tn, d, first_row):
    return pl.BlockSpec((pl.Element(tn), pl.Element(d)),
                        lambda j, i: (pl.multiple_of(first_row(j), SUBLANES), 0))


def _rope128(g2, cos_t, sin_t):
    return g2 * cos_t + pltpu.roll(g2, ROPE_DIM, axis=1) * sin_t


def _latent_kernel(x_ref, gpre_ref, win_ref, gq_ref, gkv_ref, wqt_ref, wk_ref, wvt_ref, cos_ref, sin_ref,
                   h_ref, qt_ref, k_ref, vt_ref, wlat_sc, *, scale):
    @pl.when(pl.program_id(0) == 0)
    def _():
        half = ROPE_DIM // 2
        _transpose_cast(wlat_sc, win_ref, LAT_IN - ROPE_DIM)
        rope_rows = jnp.concatenate([win_ref[LAT_IN - ROPE_DIM:LAT_IN, :], win_ref[LAT_IN - half:LAT_IN, :],
                                     win_ref[LAT_IN - ROPE_DIM:LAT_IN - half, :]], axis=0)
        wlat_sc[:, LAT_IN - ROPE_DIM:] = rope_rows.T.astype(BF16)

    h = _rms(x_ref[...], gpre_ref[...]).astype(BF16)
    h_ref[...] = h
    lat = _dot(h, wlat_sc[...])
    qa = _rms(lat[:, :Q_LORA_RANK], gq_ref[...]).astype(BF16)
    ckv = _rms(lat[:, Q_LORA_RANK:Q_LORA_RANK + KV_LORA_RANK], gkv_ref[...]).astype(BF16)
    cos_rows = cos_ref[...]
    sin_rows = sin_ref[...]
    k_rope = _rope128(lat[:, Q_LORA_RANK + KV_LORA_RANK:], cos_rows.T, sin_rows.T).astype(BF16)
    tm = qa.shape[0]
    vt = _dot_nt(wvt_ref[...], ckv)
    ones_rows = jnp.ones((VT_PAD - V_HEAD_DIM, tm), BF16)
    for grp in range(N_HEADS // HEAD_GROUP):
        qt = _dot_nt(wqt_ref[grp * HEAD_GROUP * QK_PAD:(grp + 1) * HEAD_GROUP * QK_PAD, :], qa) * scale
        kn = _dot(ckv, wk_ref[:, grp * HEAD_GROUP * QK_NOPE_DIM:(grp + 1) * HEAD_GROUP * QK_NOPE_DIM])
        for sub in range(HEAD_GROUP):
            hd = grp * HEAD_GROUP + sub
            base = sub * QK_PAD
            qt_ref[hd, :QK_NOPE_DIM, :] = qt[base:base + QK_NOPE_DIM, :].astype(BF16)
            g2 = qt[base + QK_NOPE_DIM:base + QK_PAD, :]
            g2_swapped = jnp.concatenate([g2[ROPE_DIM:], g2[:ROPE_DIM]], axis=0)
            qt_ref[hd, QK_NOPE_DIM:, :] = (g2 * cos_rows + g2_swapped * sin_rows).astype(BF16)
            k_ref[hd, :, :QK_NOPE_DIM] = kn[:, sub * QK_NOPE_DIM:(sub + 1) * QK_NOPE_DIM].astype(BF16)
            k_ref[hd, :, QK_NOPE_DIM:] = k_rope
            vt_ref[hd, 0, :V_HEAD_DIM, :] = vt[hd * V_HEAD_DIM:(hd + 1) * V_HEAD_DIM, :].astype(BF16)
            vt_ref[hd, 0, V_HEAD_DIM:, :] = ones_rows


def _latent(x, gpre, w_in_t, gq, gkv, wqt, wk, wvt, cos_rows, sin_rows, tm=256):
    s, d = x.shape
    per_blk = ATTN_TK // tm
    win_spec = pl.BlockSpec((pl.Element(LAT_IN), pl.Element(d)), lambda i: (0, 0),
                            pipeline_mode=pl.Buffered(1))
    scale = math.log2(math.e) / math.sqrt(QK_NOPE_DIM + ROPE_DIM)
    return pl.pallas_call(
        functools.partial(_latent_kernel, scale=scale),
        grid=(s // tm,),
        in_specs=[
            pl.BlockSpec((tm, d), lambda i: (i, 0)), _resident(gpre.shape),
            win_spec, _resident(gq.shape), _resident(gkv.shape),
            _resident(wqt.shape), _resident(wk.shape), _resident(wvt.shape),
            pl.BlockSpec((LANES, tm), lambda i: (0, i)),
            pl.BlockSpec((LANES, tm), lambda i: (0, i)),
        ],
        out_specs=[
            pl.BlockSpec((tm, d), lambda i: (i, 0)),
            pl.BlockSpec((N_HEADS, QK_PAD, tm), lambda i: (0, 0, i)),
            pl.BlockSpec((N_HEADS, tm, QK_PAD), lambda i: (0, i, 0)),
            pl.BlockSpec((N_HEADS, 1, VT_PAD, tm), lambda i: (0, i // per_blk, 0, i % per_blk)),
        ],
        out_shape=[
            jax.ShapeDtypeStruct((s, d), BF16),
            jax.ShapeDtypeStruct((N_HEADS, QK_PAD, s), BF16),
            jax.ShapeDtypeStruct((N_HEADS, s, QK_PAD), BF16),
            jax.ShapeDtypeStruct((N_HEADS, s // ATTN_TK, VT_PAD, ATTN_TK), BF16),
        ],
        scratch_shapes=[pltpu.VMEM((d, LAT_WIDTH), BF16)],
        compiler_params=_params(("arbitrary",)),
        name="latent",
    )(x, gpre, w_in_t, gq, gkv, wqt, wk, wvt, cos_rows, sin_rows)


def _gates_kernel(h_ref, w_ref, f32a_ref, f32b_ref, f32c_ref, o_ref, bf16a_ref, bf16b_ref, bf16c_ref, w_sc,
                  *, n_silu_blocks):
    @pl.when(pl.program_id(1) == 0)
    def _():
        _transpose_cast(w_sc, w_ref, w_ref.shape[0])

    bf16a_ref[...] = f32a_ref[...].astype(BF16)
    bf16b_ref[...] = f32b_ref[...].astype(BF16)
    bf16c_ref[...] = f32c_ref[...].astype(BF16)

    r = _dot(h_ref[...], w_sc[...])
    sg = _sigmoid(r)
    is_silu = pl.program_id(0) < n_silu_blocks
    o_ref[...] = jnp.where(is_silu, r * sg, sg).astype(BF16)


def _gates(h, w_in_t, side_casts, tm=1024, tn=1024):
    s, d = h.shape
    n_silu_blocks = MLA_WIDTH // tn
    gap = IN_G_MLA - IN_Z_MLA - MLA_WIDTH
    n_j, n_i = 3 * D_MODEL // tn, s // tm
    cast_rows = next(r for r in range(BF16_SUBLANES, d + 1, BF16_SUBLANES)
                     if d % r == 0 and d // r <= n_j * n_i)
    last_cast_block = d // cast_rows - 1
    cast_spec = pl.BlockSpec((cast_rows, d), lambda j, i: (jnp.minimum(j * n_i + i, last_cast_block), 0))
    return pl.pallas_call(
        functools.partial(_gates_kernel, n_silu_blocks=n_silu_blocks),
        grid=(n_j, n_i),
        in_specs=[pl.BlockSpec((tm, d), lambda j, i: (i, 0)),
                  _wt_rows(tn, d, lambda j: IN_Z_MLA + j * tn + jnp.where(j < n_silu_blocks, 0, gap)),
                  cast_spec, cast_spec, cast_spec],
        out_specs=[pl.BlockSpec((tm, tn), lambda j, i: (i, j)), cast_spec, cast_spec, cast_spec],
        out_shape=[jax.ShapeDtypeStruct((s, 3 * D_MODEL), BF16)]
        + [jax.ShapeDtypeStruct(w.shape, BF16) for w in side_casts],
        scratch_shapes=[pltpu.VMEM((d, tn), BF16)],
        compiler_params=_params(("arbitrary", "arbitrary")),
        name="gates",
    )(h, w_in_t, *side_casts)


def _conv_kernel(h_ref, wci_ref, wbg_ref, wcg_ref, wzc_ref, cw_ref, o_ref,
                 wci_sc, wbg_sc, wcg_sc, wzc_sc, carry_ref):
    i = pl.program_id(1)

    @pl.when(i == 0)
    def _():
        carry_ref[...] = jnp.zeros_like(carry_ref)
        for w_sc, w_ref in ((wci_sc, wci_ref), (wbg_sc, wbg_ref), (wcg_sc, wcg_ref), (wzc_sc, wzc_ref)):
            _transpose_cast(w_sc, w_ref, w_ref.shape[0])

    h = h_ref[...]
    u = _dot(h, wcg_sc[...]) * _dot(h, wci_sc[...])
    b_gate = _dot(h, wbg_sc[...])
    z_conv = _dot(h, wzc_sc[...])
    prev = carry_ref[...]
    row = lax.broadcasted_iota(jnp.int32, u.shape, 0)
    u1 = jnp.where(row == 0, prev[SUBLANES - 1:SUBLANES, :], pltpu.roll(u, 1, axis=0))
    u2 = jnp.where(row == 0, prev[SUBLANES - 2:SUBLANES - 1, :],
                   jnp.where(row == 1, prev[SUBLANES - 1:SUBLANES, :], pltpu.roll(u, 2, axis=0)))
    cw = cw_ref[...]
    conv = cw[0:1, :] * u2 + cw[1:2, :] * u1 + cw[2:3, :] * u
    o_ref[...] = (b_gate * conv * (z_conv * _sigmoid(z_conv))).astype(BF16)
    carry_ref[...] = u[u.shape[0] - SUBLANES:, :]


def _conv(h, w_in_t, conv_w, tm=2048, tn=256):
    s, d = h.shape
    seg_specs = [_wt_rows(tn, d, lambda j, start=start: start + j * tn)
                 for start in (IN_C_IN, IN_B_GATE, IN_C_GATE, IN_Z_CONV)]
    return pl.pallas_call(
        _conv_kernel,
        grid=(CONV_WIDTH // tn, s // tm),
        in_specs=[pl.BlockSpec((tm, d), lambda j, i: (i, 0))] + seg_specs
        + [pl.BlockSpec((CONV_K, tn), lambda j, i: (0, j))],
        out_specs=pl.BlockSpec((tm, tn), lambda j, i: (i, j)),
        out_shape=jax.ShapeDtypeStruct((s, CONV_WIDTH), BF16),
        scratch_shapes=[pltpu.VMEM((d, tn), BF16)] * 4 + [pltpu.VMEM((SUBLANES, tn), F32)],
        compiler_params=_params(("arbitrary", "arbitrary")),
        name="conv",
    )(h, *([w_in_t] * 4), conv_w)


def _attn_kernel(qta_ref, qtb_ref, k_ref, vt_ref, o_ref, s0_ref, s1_ref, m_ref, acc_ref):
    i = pl.program_id(1)
    tk, tq = ATTN_TK, ATTN_TQ
    all_q = slice(0, tq)
    late_q = slice(tk, tq)
    heads = range(ATTN_HEADS)

    def qk(qt_ref, hd, j, s_ref, qs):
        k = k_ref[hd, pl.ds(pl.multiple_of(j * tk, tk), tk), :]
        s_ref[hd, :, qs] = _dot(k, qt_ref[hd, :, qs])

    def softmax_pv(slot, hd, j, s_ref, qs, masked):
        s = s_ref[hd, :, qs]
        if masked:
            key = lax.broadcasted_iota(jnp.int32, s.shape, 0)
            qry = lax.broadcasted_iota(jnp.int32, s.shape, 1)
            s = jnp.where(key <= qry, s, -jnp.inf)
        m_prev = m_ref[slot, hd, :, qs]
        m_new = jnp.maximum(m_prev, jnp.max(s, axis=0, keepdims=True))
        alpha = jnp.exp2(m_prev - m_new)
        p = jnp.exp2(s - m_new).astype(BF16)
        acc_ref[slot, hd, :, qs] = alpha * acc_ref[slot, hd, :, qs] + _dot(vt_ref[hd, j], p)
        m_ref[slot, hd, :, qs] = m_new

    def first_scores(qt_ref):
        for hd in heads:
            qk(qt_ref, hd, 0, s0_ref, all_q)

    def unmasked_blocks(slot, qt_ref, qi):
        def pair(j0):
            for hd in heads:
                qk(qt_ref, hd, j0 + 1, s1_ref, all_q)
                softmax_pv(slot, hd, j0, s0_ref, all_q, masked=False)
            for hd in heads:
                qk(qt_ref, hd, j0 + 2, s0_ref, all_q)
                softmax_pv(slot, hd, j0 + 1, s1_ref, all_q, masked=False)

        def trip(jj, carry):
            for sub in range(PAIRS_PER_TRIP):
                pair(2 * (PAIRS_PER_TRIP * jj + sub))
            return carry

        lax.fori_loop(0, qi // PAIRS_PER_TRIP, trip, 0)
        done = qi - qi % PAIRS_PER_TRIP
        for extra in range(PAIRS_PER_TRIP - 1):
            @pl.when(qi % PAIRS_PER_TRIP > extra)
            def _(extra=extra):
                pair(2 * (done + extra))

    def diagonal_blocks(slot, qt_ref, qi):
        for hd in heads:
            qk(qt_ref, hd, 2 * qi + 1, s1_ref, late_q)
            softmax_pv(slot, hd, 2 * qi, s0_ref, all_q, masked=True)
        for hd in heads:
            softmax_pv(slot, hd, 2 * qi + 1, s1_ref, late_q, masked=True)
        for hd in heads:
            acc = acc_ref[slot, hd]
            out_t = acc[:V_HEAD_DIM, :] / acc[V_HEAD_DIM:V_HEAD_DIM + 1, :]
            o_ref[slot, :, hd * V_HEAD_DIM:(hd + 1) * V_HEAD_DIM] = out_t.T.astype(BF16)

    m_ref[...] = jnp.full_like(m_ref, -jnp.inf)
    acc_ref[...] = jnp.zeros_like(acc_ref)
    qi_a, qi_b = i, i + pl.num_programs(1)
    first_scores(qta_ref)
    unmasked_blocks(0, qta_ref, qi_a)
    diagonal_blocks(0, qta_ref, qi_a)
    first_scores(qtb_ref)
    unmasked_blocks(1, qtb_ref, qi_b)
    diagonal_blocks(1, qtb_ref, qi_b)


def _attention(qt, k, vt):
    nh, s, _ = k.shape
    tk, tq, hpb = ATTN_TK, ATTN_TQ, ATTN_HEADS
    n_half = s // tq // 2
    return pl.pallas_call(
        _attn_kernel,
        grid=(nh // hpb, n_half),
        in_specs=[pl.BlockSpec((hpb, QK_PAD, tq), lambda h, i: (h, 0, i)),
                  pl.BlockSpec((hpb, QK_PAD, tq), lambda h, i: (h, 0, i + n_half)),
                  pl.BlockSpec((hpb, s, QK_PAD), lambda h, i: (h, 0, 0)),
                  pl.BlockSpec((hpb, s // tk, VT_PAD, tk), lambda h, i: (h, 0, 0, 0))],
        out_specs=pl.BlockSpec((2, tq, hpb * V_HEAD_DIM), lambda h, i: (0, i, h)),
        out_shape=jax.ShapeDtypeStruct((2, s // 2, nh * V_HEAD_DIM), BF16),
        scratch_shapes=[pltpu.VMEM((hpb, tk, tq), F32), pltpu.VMEM((hpb, tk, tq), F32),
                        pltpu.VMEM((2, hpb, 1, tq), F32), pltpu.VMEM((2, hpb, VT_PAD, tq), F32)],
        compiler_params=_params(("parallel", "arbitrary")),
        name="attention",
    )(qt, qt, k, vt)


def _output_kernel(x_ref, attn_ref, sz_ref, sgm_ref, co_ref, sgc_ref,
                   womla_ref, woconv_ref, wout_ref, g_ref, o_ref):
    a = (attn_ref[...].astype(F32) * sz_ref[...].astype(F32)).astype(BF16)
    y_mla = _dot(a, womla_ref[...])
    y_conv = _dot(co_ref[...], woconv_ref[...])
    merged = sgm_ref[...].astype(F32) * y_mla + sgc_ref[...].astype(F32) * y_conv
    out = _dot(merged.astype(BF16), wout_ref[...])
    o_ref[...] = x_ref[...] + _rms(out, g_ref[...])


def _output(x, attn, gates, co, womla, woconv, wout, g, tm=256):
    s, d = x.shape
    row = lambda c: pl.BlockSpec((tm, d), lambda i: (i, c))
    return pl.pallas_call(
        _output_kernel,
        grid=(s // tm,),
        in_specs=[row(0), row(0), row(0), row(1), row(0), row(2),
                  _resident(womla.shape), _resident(woconv.shape), _resident(wout.shape),
                  _resident(g.shape)],
        out_specs=row(0),
        out_shape=jax.ShapeDtypeStruct((s, d), F32),
        compiler_params=_params(("parallel",)),
        name="output",
    )(x, attn, gates, gates, co, gates, womla, woconv, wout, g)


def _swap_halves(r):
    half = r.shape[-1] // 2
    return jnp.concatenate([r[..., half:], r[..., :half]], axis=-1)


def kernel(x, positions, pre_norm_g, w_in, q_a_norm_g, w_q_b, kv_a_norm_g, w_kv_b, conv_w,
           w_o_mla, w_o_conv, w_out, post_norm_g):
    b, s, d = x.shape
    assert b == 1 and d == D_MODEL
    x2 = x[0]

    w_in_t = w_in.T
    wq3 = w_q_b.reshape(Q_LORA_RANK, N_HEADS, QK_NOPE_DIM + ROPE_DIM)
    wq_rope = wq3[..., QK_NOPE_DIM:]
    wq = jnp.concatenate([wq3[..., :QK_NOPE_DIM], wq_rope, _swap_halves(wq_rope)], axis=-1)
    wqt = wq.reshape(Q_LORA_RANK, N_HEADS * QK_PAD).T.astype(BF16)
    wkv3 = w_kv_b.reshape(KV_LORA_RANK, N_HEADS, QK_NOPE_DIM + V_HEAD_DIM)
    wk = wkv3[..., :QK_NOPE_DIM].reshape(KV_LORA_RANK, -1).astype(BF16)
    wvt = wkv3[..., QK_NOPE_DIM:].reshape(KV_LORA_RANK, -1).T.astype(BF16)

    inv_freq = ROPE_THETA ** (-jnp.arange(0, ROPE_DIM, 2, dtype=F32) / ROPE_DIM)
    ang = inv_freq[:, None] * positions[0].astype(F32)[None, :]
    cos, sin = jnp.cos(ang), jnp.sin(ang)
    zero_rows = jnp.zeros((LANES - ROPE_DIM, s), F32)
    cos_rows = jnp.concatenate([cos, cos, zero_rows], axis=0)
    sin_rows = jnp.concatenate([-sin, sin, zero_rows], axis=0)

    h, qt, k, vt = _latent(x2, pre_norm_g.reshape(1, d), w_in_t, q_a_norm_g.reshape(1, -1),
                           kv_a_norm_g.reshape(1, -1), wqt, wk, wvt, cos_rows, sin_rows)
    gates, womla, woconv, wout = _gates(h, w_in_t, (w_o_mla, w_o_conv, w_out))
    co = _conv(h, w_in_t, conv_w)
    attn = _attention(qt, k, vt).reshape(s, MLA_WIDTH)
    out = _output(x2, attn, gates, co, womla, woconv, wout, post_norm_g.reshape(1, d))
    return out[None]
```

```python
import functools
import math

import jax
import jax.numpy as jnp
from jax import lax
from jax.experimental import pallas as pl
from jax.experimental.pallas import tpu as pltpu

D_MODEL = 2048
N_HEADS = 16
QK_NOPE_DIM = 128
ROPE_DIM = 64
V_HEAD_DIM = 128
Q_LORA_RANK = 512
KV_LORA_RANK = 512
MLA_WIDTH = N_HEADS * V_HEAD_DIM
CONV_WIDTH = D_MODEL
CONV_K = 3
ROPE_THETA = 10000.0
RMS_EPS = 1e-6

LANES = 128
SUBLANES = 8
QK_PAD = QK_NOPE_DIM + LANES
BF16_SUBLANES = 16
VT_PAD = V_HEAD_DIM + BF16_SUBLANES
ATTN_TK = 512
ATTN_TQ = 2 * ATTN_TK
ATTN_HEADS = 2
HEAD_GROUP = 4
LAT_WIDTH = Q_LORA_RANK + KV_LORA_RANK + LANES
LAT_IN = Q_LORA_RANK + KV_LORA_RANK + ROPE_DIM
IN_Z_MLA = LAT_IN
IN_C_IN = IN_Z_MLA + MLA_WIDTH
IN_B_GATE = IN_C_IN + CONV_WIDTH
IN_C_GATE = IN_B_GATE + CONV_WIDTH
IN_Z_CONV = IN_C_GATE + CONV_WIDTH
IN_G_MLA = IN_Z_CONV + CONV_WIDTH
IN_G_CONV = IN_G_MLA + D_MODEL
VMEM_LIMIT = 56 * 1024 * 1024

BF16 = jnp.bfloat16
F32 = jnp.float32


def _params(semantics):
    return pltpu.CompilerParams(dimension_semantics=semantics, vmem_limit_bytes=VMEM_LIMIT)


def _resident(shape):
    return pl.BlockSpec(shape, lambda *_: (0,) * len(shape), pipeline_mode=pl.Buffered(1))


def _rms(xf, g):
    r = lax.rsqrt(jnp.mean(xf * xf, axis=-1, keepdims=True) + RMS_EPS)
    return xf * r * g


def _sigmoid(x):
    return 0.5 * jnp.tanh(0.5 * x) + 0.5


def _dot(a, b):
    return jnp.dot(a, b, preferred_element_type=F32)


def _dot_nt(a, b):
    return lax.dot_general(a, b, (((1,), (1,)), ((), ())), preferred_element_type=F32)


def _transpose_cast(dst_sc, src_ref, n_rows):
    for c in range(0, n_rows, LANES):
        dst_sc[:, c:c + LANES] = src_ref[c:c + LANES, :].T.astype(BF16)


def _wt_rows(tn, d, first_row):
    return pl.BlockSpec((pl.Element(tn), pl.Element(d)),
                        lambda j, i: (pl.multiple_of(first_row(j), SUBLANES), 0))


def _rope128(g2, cos_t, sin_t):
    return g2 * cos_t + pltpu.roll(g2, ROPE_DIM, axis=1) * sin_t


def _latent_kernel(x_ref, gpre_ref, win_ref, gq_ref, gkv_ref, wqt_ref, wk_ref, wvt_ref, cos_ref, sin_ref,
                   h_ref, qt_ref, k_ref, vt_ref, wlat_sc, *, scale):
    @pl.when(pl.program_id(0) == 0)
    def _():
        half = ROPE_DIM // 2
        _transpose_cast(wlat_sc, win_ref, LAT_IN - ROPE_DIM)
        rope_rows = jnp.concatenate([win_ref[LAT_IN - ROPE_DIM:LAT_IN, :], win_ref[LAT_IN - half:LAT_IN, :],
                                     win_ref[LAT_IN - ROPE_DIM:LAT_IN - half, :]], axis=0)
        wlat_sc[:, LAT_IN - ROPE_DIM:] = rope_rows.T.astype(BF16)

    h = _rms(x_ref[...], gpre_ref[...]).astype(BF16)
    h_ref[...] = h
    lat = _dot(h, wlat_sc[...])
    qa = _rms(lat[:, :Q_LORA_RANK], gq_ref[...]).astype(BF16)
    ckv = _rms(lat[:, Q_LORA_RANK:Q_LORA_RANK + KV_LORA_RANK], gkv_ref[...]).astype(BF16)
    cos_rows = cos_ref[...]
    sin_rows = sin_ref[...]
    k_rope = _rope128(lat[:, Q_LORA_RANK + KV_LORA_RANK:], cos_rows.T, sin_rows.T).astype(BF16)
    tm = qa.shape[0]
    vt = _dot_nt(wvt_ref[...], ckv)
    ones_rows = jnp.ones((VT_PAD - V_HEAD_DIM, tm), BF16)
    for grp in range(N_HEADS // HEAD_GROUP):
        qt = _dot_nt(wqt_ref[grp * HEAD_GROUP * QK_PAD:(grp + 1) * HEAD_GROUP * QK_PAD, :], qa) * scale
        kn = _dot(ckv, wk_ref[:, grp * HEAD_GROUP * QK_NOPE_DIM:(grp + 1) * HEAD_GROUP * QK_NOPE_DIM])
        for sub in range(HEAD_GROUP):
            hd = grp * HEAD_GROUP + sub
            base = sub * QK_PAD
            qt_ref[hd, :QK_NOPE_DIM, :] = qt[base:base + QK_NOPE_DIM, :].astype(BF16)
            g2 = qt[base + QK_NOPE_DIM:base + QK_PAD, :]
            g2_swapped = jnp.concatenate([g2[ROPE_DIM:], g2[:ROPE_DIM]], axis=0)
            qt_ref[hd, QK_NOPE_DIM:, :] = (g2 * cos_rows + g2_swapped * sin_rows).astype(BF16)
            k_ref[hd, :, :QK_NOPE_DIM] = kn[:, sub * QK_NOPE_DIM:(sub + 1) * QK_NOPE_DIM].astype(BF16)
            k_ref[hd, :, QK_NOPE_DIM:] = k_rope
            vt_ref[hd, 0, :V_HEAD_DIM, :] = vt[hd * V_HEAD_DIM:(hd + 1) * V_HEAD_DIM, :].astype(BF16)
            vt_ref[hd, 0, V_HEAD_DIM:, :] = ones_rows


def _latent(x, gpre, w_in_t, gq, gkv, wqt, wk, wvt, cos_rows, sin_rows, tm=256):
    s, d = x.shape
    per_blk = ATTN_TK // tm
    win_spec = pl.BlockSpec((pl.Element(LAT_IN), pl.Element(d)), lambda i: (0, 0),
                            pipeline_mode=pl.Buffered(1))
    scale = math.log2(math.e) / math.sqrt(QK_NOPE_DIM + ROPE_DIM)
    return pl.pallas_call(
        functools.partial(_latent_kernel, scale=scale),
        grid=(s // tm,),
        in_specs=[
            pl.BlockSpec((tm, d), lambda i: (i, 0)), _resident(gpre.shape),
            win_spec, _resident(gq.shape), _resident(gkv.shape),
            _resident(wqt.shape), _resident(wk.shape), _resident(wvt.shape),
            pl.BlockSpec((LANES, tm), lambda i: (0, i)),
            pl.BlockSpec((LANES, tm), lambda i: (0, i)),
        ],
        out_specs=[
            pl.BlockSpec((tm, d), lambda i: (i, 0)),
            pl.BlockSpec((N_HEADS, QK_PAD, tm), lambda i: (0, 0, i)),
            pl.BlockSpec((N_HEADS, tm, QK_PAD), lambda i: (0, i, 0)),
            pl.BlockSpec((N_HEADS, 1, VT_PAD, tm), lambda i: (0, i // per_blk, 0, i % per_blk)),
        ],
        out_shape=[
            jax.ShapeDtypeStruct((s, d), BF16),
            jax.ShapeDtypeStruct((N_HEADS, QK_PAD, s), BF16),
            jax.ShapeDtypeStruct((N_HEADS, s, QK_PAD), BF16),
            jax.ShapeDtypeStruct((N_HEADS, s // ATTN_TK, VT_PAD, ATTN_TK), BF16),
        ],
        scratch_shapes=[pltpu.VMEM((d, LAT_WIDTH), BF16)],
        compiler_params=_params(("arbitrary",)),
        name="latent",
    )(x, gpre, w_in_t, gq, gkv, wqt, wk, wvt, cos_rows, sin_rows)


def _gates_kernel(h_ref, w_ref, f32a_ref, f32b_ref, f32c_ref, o_ref, bf16a_ref, bf16b_ref, bf16c_ref, w_sc,
                  *, n_silu_blocks):
    @pl.when(pl.program_id(1) == 0)
    def _():
        w_sc[...] = w_ref[...].astype(BF16)

    bf16a_ref[...] = f32a_ref[...].astype(BF16)
    bf16b_ref[...] = f32b_ref[...].astype(BF16)
    bf16c_ref[...] = f32c_ref[...].astype(BF16)

    r = _dot_nt(h_ref[...], w_sc[...])
    sg = _sigmoid(r)
    is_silu = pl.program_id(0) < n_silu_blocks
    o_ref[...] = jnp.where(is_silu, r * sg, sg).astype(BF16)


def _gates(h, w_in_t, side_casts, tm=1024, tn=1024):
    s, d = h.shape
    n_silu_blocks = MLA_WIDTH // tn
    gap = IN_G_MLA - IN_Z_MLA - MLA_WIDTH
    n_j, n_i = 3 * D_MODEL // tn, s // tm
    cast_rows = next(r for r in range(BF16_SUBLANES, d + 1, BF16_SUBLANES)
                     if d % r == 0 and d // r <= n_j * n_i)
    last_cast_block = d // cast_rows - 1
    cast_spec = pl.BlockSpec((cast_rows, d), lambda j, i: (jnp.minimum(j * n_i + i, last_cast_block), 0))
    return pl.pallas_call(
        functools.partial(_gates_kernel, n_silu_blocks=n_silu_blocks),
        grid=(n_j, n_i),
        in_specs=[pl.BlockSpec((tm, d), lambda j, i: (i, 0)),
                  _wt_rows(tn, d, lambda j: IN_Z_MLA + j * tn + jnp.where(j < n_silu_blocks, 0, gap)),
                  cast_spec, cast_spec, cast_spec],
        out_specs=[pl.BlockSpec((tm, tn), lambda j, i: (i, j)), cast_spec, cast_spec, cast_spec],
        out_shape=[jax.ShapeDtypeStruct((s, 3 * D_MODEL), BF16)]
        + [jax.ShapeDtypeStruct(w.shape, BF16) for w in side_casts],
        scratch_shapes=[pltpu.VMEM((tn, d), BF16)],
        compiler_params=_params(("arbitrary", "arbitrary")),
        name="gates",
    )(h, w_in_t, *side_casts)


def _conv_kernel(h_ref, wci_ref, wbg_ref, wcg_ref, wzc_ref, cw_ref, o_ref,
                 wci_sc, wbg_sc, wcg_sc, wzc_sc, carry_ref):
    i = pl.program_id(1)

    @pl.when(i == 0)
    def _():
        carry_ref[...] = jnp.zeros_like(carry_ref)
        wci_sc[...] = wci_ref[...].astype(BF16)
        wbg_sc[...] = wbg_ref[...].astype(BF16)
        wcg_sc[...] = wcg_ref[...].astype(BF16)
        wzc_sc[...] = wzc_ref[...].astype(BF16)

    h = h_ref[...]
    u = _dot_nt(h, wcg_sc[...]) * _dot_nt(h, wci_sc[...])
    b_gate = _dot_nt(h, wbg_sc[...])
    z_conv = _dot_nt(h, wzc_sc[...])
    prev = carry_ref[...]
    row = lax.broadcasted_iota(jnp.int32, u.shape, 0)
    u1 = jnp.where(row == 0, prev[SUBLANES - 1:SUBLANES, :], pltpu.roll(u, 1, axis=0))
    u2 = jnp.where(row == 0, prev[SUBLANES - 2:SUBLANES - 1, :],
                   jnp.where(row == 1, prev[SUBLANES - 1:SUBLANES, :], pltpu.roll(u, 2, axis=0)))
    cw = cw_ref[...]
    conv = cw[0:1, :] * u2 + cw[1:2, :] * u1 + cw[2:3, :] * u
    o_ref[...] = (b_gate * conv * (z_conv * _sigmoid(z_conv))).astype(BF16)
    carry_ref[...] = u[u.shape[0] - SUBLANES:, :]


def _conv(h, w_in_t, conv_w, tm=2048, tn=256):
    s, d = h.shape
    seg_specs = [_wt_rows(tn, d, lambda j, start=start: start + j * tn)
                 for start in (IN_C_IN, IN_B_GATE, IN_C_GATE, IN_Z_CONV)]
    return pl.pallas_call(
        _conv_kernel,
        grid=(CONV_WIDTH // tn, s // tm),
        in_specs=[pl.BlockSpec((tm, d), lambda j, i: (i, 0))] + seg_specs
        + [pl.BlockSpec((CONV_K, tn), lambda j, i: (0, j))],
        out_specs=pl.BlockSpec((tm, tn), lambda j, i: (i, j)),
        out_shape=jax.ShapeDtypeStruct((s, CONV_WIDTH), BF16),
        scratch_shapes=[pltpu.VMEM((tn, d), BF16)] * 4 + [pltpu.VMEM((SUBLANES, tn), F32)],
        compiler_params=_params(("arbitrary", "arbitrary")),
        name="conv",
    )(h, *([w_in_t] * 4), conv_w)


def _attn_kernel(qta_ref, qtb_ref, k_ref, vt_ref, o_ref, s0_ref, s1_ref, m_ref, acc_ref):
    i = pl.program_id(1)
    tk, tq = ATTN_TK, ATTN_TQ
    all_q = slice(0, tq)
    late_q = slice(tk, tq)
    heads = range(ATTN_HEADS)

    def qk(qt_ref, hd, j, s_ref, qs):
        k = k_ref[hd, pl.ds(pl.multiple_of(j * tk, tk), tk), :]
        s_ref[hd, :, qs] = _dot(k, qt_ref[hd, :, qs])

    def softmax_pv(slot, hd, j, s_ref, qs, masked):
        s = s_ref[hd, :, qs]
        if masked:
            key = lax.broadcasted_iota(jnp.int32, s.shape, 0)
            qry = lax.broadcasted_iota(jnp.int32, s.shape, 1)
            s = jnp.where(key <= qry, s, -jnp.inf)
        m_prev = m_ref[slot, hd, :, qs]
        m_new = jnp.maximum(m_prev, jnp.max(s, axis=0, keepdims=True))
        alpha = jnp.exp2(m_prev - m_new)
        p = jnp.exp2(s - m_new).astype(BF16)
        acc_ref[slot, hd, :, qs] = alpha * acc_ref[slot, hd, :, qs] + _dot(vt_ref[hd, j], p)
        m_ref[slot, hd, :, qs] = m_new

    def first_scores(qt_ref):
        for hd in heads:
            qk(qt_ref, hd, 0, s0_ref, all_q)

    def pair(slot, qt_ref, j0):
        for hd in heads:
            qk(qt_ref, hd, j0 + 1, s1_ref, all_q)
            softmax_pv(slot, hd, j0, s0_ref, all_q, masked=False)
        for hd in heads:
            qk(qt_ref, hd, j0 + 2, s0_ref, all_q)
            softmax_pv(slot, hd, j0 + 1, s1_ref, all_q, masked=False)

    def paired_trips(slot, qt_ref, qi):
        def trip(jj, carry):
            pair(slot, qt_ref, 4 * jj)
            pair(slot, qt_ref, 4 * jj + 2)
            return carry

        lax.fori_loop(0, qi // 2, trip, 0)

    def finish(slot, qt_ref, qi, odd, then=None):
        if odd:
            pair(slot, qt_ref, 2 * qi - 2)
        diagonal_blocks(slot, qt_ref, qi)
        if then is not None:
            then()

    def diagonal_blocks(slot, qt_ref, qi):
        for hd in heads:
            qk(qt_ref, hd, 2 * qi + 1, s1_ref, late_q)
            softmax_pv(slot, hd, 2 * qi, s0_ref, all_q, masked=True)
        for hd in heads:
            softmax_pv(slot, hd, 2 * qi + 1, s1_ref, late_q, masked=True)
        for hd in heads:
            acc = acc_ref[slot, hd]
            out_t = acc[:V_HEAD_DIM, :] / acc[V_HEAD_DIM:V_HEAD_DIM + 1, :]
            o_ref[slot, :, hd * V_HEAD_DIM:(hd + 1) * V_HEAD_DIM] = out_t.T.astype(BF16)

    m_ref[...] = jnp.full_like(m_ref, -jnp.inf)
    acc_ref[...] = jnp.zeros_like(acc_ref)
    n_half = pl.num_programs(1)
    qi_a, qi_b = i, i + n_half
    is_odd = i % 2 == 1
    first_scores(qta_ref)
    paired_trips(0, qta_ref, qi_a)
    for odd in (True, False):
        @pl.when(is_odd == odd)
        def _(odd=odd):
            finish(0, qta_ref, qi_a, odd, then=lambda: first_scores(qtb_ref))
    paired_trips(1, qtb_ref, qi_b)
    for odd in (True, False):
        @pl.when(is_odd == odd)
        def _(odd=odd):
            finish(1, qtb_ref, qi_b, odd)


def _attention(qt, k, vt):
    nh, s, _ = k.shape
    tk, tq, hpb = ATTN_TK, ATTN_TQ, ATTN_HEADS
    n_half = s // tq // 2
    return pl.pallas_call(
        _attn_kernel,
        grid=(nh // hpb, n_half),
        in_specs=[pl.BlockSpec((hpb, QK_PAD, tq), lambda h, i: (h, 0, i)),
                  pl.BlockSpec((hpb, QK_PAD, tq), lambda h, i: (h, 0, i + n_half)),
                  pl.BlockSpec((hpb, s, QK_PAD), lambda h, i: (h, 0, 0)),
                  pl.BlockSpec((hpb, s // tk, VT_PAD, tk), lambda h, i: (h, 0, 0, 0))],
        out_specs=pl.BlockSpec((2, tq, hpb * V_HEAD_DIM), lambda h, i: (0, i, h)),
        out_shape=jax.ShapeDtypeStruct((2, s // 2, nh * V_HEAD_DIM), BF16),
        scratch_shapes=[pltpu.VMEM((hpb, tk, tq), F32), pltpu.VMEM((hpb, tk, tq), F32),
                        pltpu.VMEM((2, hpb, 1, tq), F32), pltpu.VMEM((2, hpb, VT_PAD, tq), F32)],
        compiler_params=_params(("parallel", "arbitrary")),
        name="attention",
    )(qt, qt, k, vt)


def _output_kernel(x_ref, attn_ref, sz_ref, sgm_ref, co_ref, sgc_ref,
                   womla_ref, woconv_ref, wout_ref, g_ref, o_ref):
    a = (attn_ref[...].astype(F32) * sz_ref[...].astype(F32)).astype(BF16)
    y_mla = _dot(a, womla_ref[...])
    y_conv = _dot(co_ref[...], woconv_ref[...])
    merged = sgm_ref[...].astype(F32) * y_mla + sgc_ref[...].astype(F32) * y_conv
    out = _dot(merged.astype(BF16), wout_ref[...])
    o_ref[...] = x_ref[...] + _rms(out, g_ref[...])


def _output(x, attn, gates, co, womla, woconv, wout, g, tm=256):
    s, d = x.shape
    row = lambda c: pl.BlockSpec((tm, d), lambda i: (i, c))
    return pl.pallas_call(
        _output_kernel,
        grid=(s // tm,),
        in_specs=[row(0), row(0), row(0), row(1), row(0), row(2),
                  _resident(womla.shape), _resident(woconv.shape), _resident(wout.shape),
                  _resident(g.shape)],
        out_specs=row(0),
        out_shape=jax.ShapeDtypeStruct((s, d), F32),
        compiler_params=_params(("parallel",)),
        name="output",
    )(x, attn, gates, gates, co, gates, womla, woconv, wout, g)


def _prep_up_weights_kernel(wq_ref, wkv_ref, wqt_ref, wk_ref, wvt_ref):
    wq_t = wq_ref[...].T
    half = ROPE_DIM // 2
    for hd in range(N_HEADS):
        src = hd * (QK_NOPE_DIM + ROPE_DIM)
        dst = hd * QK_PAD
        rope = wq_t[src + QK_NOPE_DIM:src + QK_NOPE_DIM + ROPE_DIM, :].astype(BF16)
        wqt_ref[dst:dst + QK_NOPE_DIM, :] = wq_t[src:src + QK_NOPE_DIM, :].astype(BF16)
        wqt_ref[dst + QK_NOPE_DIM:dst + QK_NOPE_DIM + ROPE_DIM, :] = rope
        wqt_ref[dst + QK_NOPE_DIM + ROPE_DIM:dst + QK_NOPE_DIM + ROPE_DIM + half, :] = rope[half:, :]
        wqt_ref[dst + QK_NOPE_DIM + ROPE_DIM + half:dst + QK_PAD, :] = rope[:half, :]
        kv0 = hd * (QK_NOPE_DIM + V_HEAD_DIM)
        wk_ref[:, hd * QK_NOPE_DIM:(hd + 1) * QK_NOPE_DIM] = wkv_ref[:, kv0:kv0 + QK_NOPE_DIM].astype(BF16)
        wvt_ref[hd * V_HEAD_DIM:(hd + 1) * V_HEAD_DIM, :] = (
            wkv_ref[:, kv0 + QK_NOPE_DIM:kv0 + QK_NOPE_DIM + V_HEAD_DIM].T.astype(BF16))


def _prep_up_weights(w_q_b, w_kv_b):
    return pl.pallas_call(
        _prep_up_weights_kernel,
        out_shape=[jax.ShapeDtypeStruct((N_HEADS * QK_PAD, Q_LORA_RANK), BF16),
                   jax.ShapeDtypeStruct((KV_LORA_RANK, N_HEADS * QK_NOPE_DIM), BF16),
                   jax.ShapeDtypeStruct((N_HEADS * V_HEAD_DIM, KV_LORA_RANK), BF16)],
        compiler_params=pltpu.CompilerParams(vmem_limit_bytes=VMEM_LIMIT),
        name="prep_up_weights",
    )(w_q_b, w_kv_b)


def kernel(x, positions, pre_norm_g, w_in, q_a_norm_g, w_q_b, kv_a_norm_g, w_kv_b, conv_w,
           w_o_mla, w_o_conv, w_out, post_norm_g):
    b, s, d = x.shape
    assert b == 1 and d == D_MODEL
    x2 = x[0]

    w_in_t = w_in.T
    wqt, wk, wvt = _prep_up_weights(w_q_b, w_kv_b)

    inv_freq = ROPE_THETA ** (-jnp.arange(0, ROPE_DIM, 2, dtype=F32) / ROPE_DIM)
    ang = inv_freq[:, None] * positions[0].astype(F32)[None, :]
    cos, sin = jnp.cos(ang), jnp.sin(ang)
    zero_rows = jnp.zeros((LANES - ROPE_DIM, s), F32)
    cos_rows = jnp.concatenate([cos, cos, zero_rows], axis=0)
    sin_rows = jnp.concatenate([-sin, sin, zero_rows], axis=0)

    h, qt, k, vt = _latent(x2, pre_norm_g.reshape(1, d), w_in_t, q_a_norm_g.reshape(1, -1),
                           kv_a_norm_g.reshape(1, -1), wqt, wk, wvt, cos_rows, sin_rows)
    gates, womla, woconv, wout = _gates(h, w_in_t, (w_o_mla, w_o_conv, w_out))
    co = _conv(h, w_in_t, conv_w)
    attn = _attention(qt, k, vt).reshape(s, MLA_WIDTH)
    out = _output(x2, attn, gates, co, womla, woconv, wout, post_norm_g.reshape(1, d))
    return out[None]
```

```python
import functools
import math

import jax
import jax.numpy as jnp
from jax import lax
from jax.experimental import pallas as pl
from jax.experimental.pallas import tpu as pltpu

D_MODEL = 2048
N_HEADS = 16
QK_NOPE_DIM = 128
ROPE_DIM = 64
V_HEAD_DIM = 128
Q_LORA_RANK = 512
KV_LORA_RANK = 512
MLA_WIDTH = N_HEADS * V_HEAD_DIM
CONV_WIDTH = D_MODEL
CONV_K = 3
ROPE_THETA = 10000.0
RMS_EPS = 1e-6

LANES = 128
SUBLANES = 8
QK_PAD = QK_NOPE_DIM + LANES
BF16_SUBLANES = 16
VT_PAD = V_HEAD_DIM + BF16_SUBLANES
ATTN_TK = 512
ATTN_TQ = 2 * ATTN_TK
ATTN_HEADS = 2
HEAD_GROUP = 4
LAT_WIDTH = Q_LORA_RANK + KV_LORA_RANK + LANES
LAT_IN = Q_LORA_RANK + KV_LORA_RANK + ROPE_DIM
IN_Z_MLA = LAT_IN
IN_C_IN = IN_Z_MLA + MLA_WIDTH
IN_B_GATE = IN_C_IN + CONV_WIDTH
IN_C_GATE = IN_B_GATE + CONV_WIDTH
IN_Z_CONV = IN_C_GATE + CONV_WIDTH
IN_G_MLA = IN_Z_CONV + CONV_WIDTH
IN_G_CONV = IN_G_MLA + D_MODEL
VMEM_LIMIT = 56 * 1024 * 1024

BF16 = jnp.bfloat16
F32 = jnp.float32


def _params(semantics):
    return pltpu.CompilerParams(dimension_semantics=semantics, vmem_limit_bytes=VMEM_LIMIT)


def _resident(shape):
    return pl.BlockSpec(shape, lambda *_: (0,) * len(shape), pipeline_mode=pl.Buffered(1))


def _rms(xf, g):
    r = lax.rsqrt(jnp.mean(xf * xf, axis=-1, keepdims=True) + RMS_EPS)
    return xf * r * g


def _sigmoid(x):
    return 0.5 * jnp.tanh(0.5 * x) + 0.5


def _dot(a, b):
    return jnp.dot(a, b, preferred_element_type=F32)


def _dot_nt(a, b):
    return lax.dot_general(a, b, (((1,), (1,)), ((), ())), preferred_element_type=F32)


def _transpose_cast(dst_sc, src_ref, n_rows):
    for c in range(0, n_rows, LANES):
        dst_sc[:, c:c + LANES] = src_ref[c:c + LANES, :].T.astype(BF16)


def _wt_rows(tn, d, first_row):
    return pl.BlockSpec((pl.Element(tn), pl.Element(d)),
                        lambda j, i: (pl.multiple_of(first_row(j), SUBLANES), 0))


def _rope128(g2, cos_t, sin_t):
    return g2 * cos_t + pltpu.roll(g2, ROPE_DIM, axis=1) * sin_t


def _latent_kernel(x_ref, gpre_ref, win_ref, gq_ref, gkv_ref, wqt_ref, wk_ref, wvt_ref, cos_ref, sin_ref,
                   h_ref, qt_ref, k_ref, vt_ref, wlat_sc, *, scale):
    @pl.when(pl.program_id(0) == 0)
    def _():
        half = ROPE_DIM // 2
        _transpose_cast(wlat_sc, win_ref, LAT_IN - ROPE_DIM)
        rope_rows = jnp.concatenate([win_ref[LAT_IN - ROPE_DIM:LAT_IN, :], win_ref[LAT_IN - half:LAT_IN, :],
                                     win_ref[LAT_IN - ROPE_DIM:LAT_IN - half, :]], axis=0)
        wlat_sc[:, LAT_IN - ROPE_DIM:] = rope_rows.T.astype(BF16)

    h = _rms(x_ref[...], gpre_ref[...]).astype(BF16)
    h_ref[...] = h
    lat = _dot(h, wlat_sc[...])
    qa = _rms(lat[:, :Q_LORA_RANK], gq_ref[...]).astype(BF16)
    ckv = _rms(lat[:, Q_LORA_RANK:Q_LORA_RANK + KV_LORA_RANK], gkv_ref[...]).astype(BF16)
    cos_rows = cos_ref[...]
    sin_rows = sin_ref[...]
    k_rope = _rope128(lat[:, Q_LORA_RANK + KV_LORA_RANK:], cos_rows.T, sin_rows.T).astype(BF16)
    tm = qa.shape[0]
    vt = _dot_nt(wvt_ref[...], ckv)
    ones_rows = jnp.ones((VT_PAD - V_HEAD_DIM, tm), BF16)
    for grp in range(N_HEADS // HEAD_GROUP):
        qt = _dot_nt(wqt_ref[grp * HEAD_GROUP * QK_PAD:(grp + 1) * HEAD_GROUP * QK_PAD, :], qa) * scale
        kn = _dot(ckv, wk_ref[:, grp * HEAD_GROUP * QK_NOPE_DIM:(grp + 1) * HEAD_GROUP * QK_NOPE_DIM])
        for sub in range(HEAD_GROUP):
            hd = grp * HEAD_GROUP + sub
            base = sub * QK_PAD
            qt_ref[hd, :QK_NOPE_DIM, :] = qt[base:base + QK_NOPE_DIM, :].astype(BF16)
            g2 = qt[base + QK_NOPE_DIM:base + QK_PAD, :]
            g2_swapped = jnp.concatenate([g2[ROPE_DIM:], g2[:ROPE_DIM]], axis=0)
            qt_ref[hd, QK_NOPE_DIM:, :] = (g2 * cos_rows + g2_swapped * sin_rows).astype(BF16)
            k_ref[hd, :, :QK_NOPE_DIM] = kn[:, sub * QK_NOPE_DIM:(sub + 1) * QK_NOPE_DIM].astype(BF16)
            k_ref[hd, :, QK_NOPE_DIM:] = k_rope
            vt_ref[hd, 0, :V_HEAD_DIM, :] = vt[hd * V_HEAD_DIM:(hd + 1) * V_HEAD_DIM, :].astype(BF16)
            vt_ref[hd, 0, V_HEAD_DIM:, :] = ones_rows


def _latent(x, gpre, w_in_t, gq, gkv, wqt, wk, wvt, cos_rows, sin_rows, tm=256):
    s, d = x.shape
    per_blk = ATTN_TK // tm
    win_spec = pl.BlockSpec((pl.Element(LAT_IN), pl.Element(d)), lambda i: (0, 0),
                            pipeline_mode=pl.Buffered(1))
    scale = math.log2(math.e) / math.sqrt(QK_NOPE_DIM + ROPE_DIM)
    return pl.pallas_call(
        functools.partial(_latent_kernel, scale=scale),
        grid=(s // tm,),
        in_specs=[
            pl.BlockSpec((tm, d), lambda i: (i, 0)), _resident(gpre.shape),
            win_spec, _resident(gq.shape), _resident(gkv.shape),
            _resident(wqt.shape), _resident(wk.shape), _resident(wvt.shape),
            pl.BlockSpec((LANES, tm), lambda i: (0, i)),
            pl.BlockSpec((LANES, tm), lambda i: (0, i)),
        ],
        out_specs=[
            pl.BlockSpec((tm, d), lambda i: (i, 0)),
            pl.BlockSpec((N_HEADS, QK_PAD, tm), lambda i: (0, 0, i)),
            pl.BlockSpec((N_HEADS, tm, QK_PAD), lambda i: (0, i, 0)),
            pl.BlockSpec((N_HEADS, 1, VT_PAD, tm), lambda i: (0, i // per_blk, 0, i % per_blk)),
        ],
        out_shape=[
            jax.ShapeDtypeStruct((s, d), BF16),
            jax.ShapeDtypeStruct((N_HEADS, QK_PAD, s), BF16),
            jax.ShapeDtypeStruct((N_HEADS, s, QK_PAD), BF16),
            jax.ShapeDtypeStruct((N_HEADS, s // ATTN_TK, VT_PAD, ATTN_TK), BF16),
        ],
        scratch_shapes=[pltpu.VMEM((d, LAT_WIDTH), BF16)],
        compiler_params=_params(("arbitrary",)),
        name="latent",
    )(x, gpre, w_in_t, gq, gkv, wqt, wk, wvt, cos_rows, sin_rows)


def _gates_kernel(h_ref, w_ref, f32a_ref, f32b_ref, f32c_ref, o_ref, bf16a_ref, bf16b_ref, bf16c_ref, w_sc,
                  *, n_silu_blocks):
    @pl.when(pl.program_id(1) == 0)
    def _():
        w_sc[...] = w_ref[...].astype(BF16)

    bf16a_ref[...] = f32a_ref[...].astype(BF16)
    bf16b_ref[...] = f32b_ref[...].astype(BF16)
    bf16c_ref[...] = f32c_ref[...].astype(BF16)

    r = _dot_nt(h_ref[...], w_sc[...])
    sg = _sigmoid(r)
    is_silu = pl.program_id(0) < n_silu_blocks
    o_ref[...] = jnp.where(is_silu, r * sg, sg).astype(BF16)


def _gates(h, w_in_t, side_casts, tm=1024, tn=1024):
    s, d = h.shape
    n_silu_blocks = MLA_WIDTH // tn
    gap = IN_G_MLA - IN_Z_MLA - MLA_WIDTH
    n_j, n_i = 3 * D_MODEL // tn, s // tm
    cast_rows = next(r for r in range(BF16_SUBLANES, d + 1, BF16_SUBLANES)
                     if d % r == 0 and d // r <= n_j * n_i)
    last_cast_block = d // cast_rows - 1
    cast_spec = pl.BlockSpec((cast_rows, d), lambda j, i: (jnp.minimum(j * n_i + i, last_cast_block), 0))
    return pl.pallas_call(
        functools.partial(_gates_kernel, n_silu_blocks=n_silu_blocks),
        grid=(n_j, n_i),
        in_specs=[pl.BlockSpec((tm, d), lambda j, i: (i, 0)),
                  _wt_rows(tn, d, lambda j: IN_Z_MLA + j * tn + jnp.where(j < n_silu_blocks, 0, gap)),
                  cast_spec, cast_spec, cast_spec],
        out_specs=[pl.BlockSpec((tm, tn), lambda j, i: (i, j)), cast_spec, cast_spec, cast_spec],
        out_shape=[jax.ShapeDtypeStruct((s, 3 * D_MODEL), BF16)]
        + [jax.ShapeDtypeStruct(w.shape, BF16) for w in side_casts],
        scratch_shapes=[pltpu.VMEM((tn, d), BF16)],
        compiler_params=_params(("arbitrary", "arbitrary")),
        name="gates",
    )(h, w_in_t, *side_casts)


def _conv_kernel(h_ref, wci_ref, wbg_ref, wcg_ref, wzc_ref, cw_ref, o_ref,
                 wci_sc, wbg_sc, wcg_sc, wzc_sc, carry_ref):
    i = pl.program_id(1)

    @pl.when(i == 0)
    def _():
        carry_ref[...] = jnp.zeros_like(carry_ref)
        wci_sc[...] = wci_ref[...].astype(BF16)
        wbg_sc[...] = wbg_ref[...].astype(BF16)
        wcg_sc[...] = wcg_ref[...].astype(BF16)
        wzc_sc[...] = wzc_ref[...].astype(BF16)

    h = h_ref[...]
    u = _dot_nt(h, wcg_sc[...]) * _dot_nt(h, wci_sc[...])
    b_gate = _dot_nt(h, wbg_sc[...])
    prev = carry_ref[...]
    row = lax.broadcasted_iota(jnp.int32, u.shape, 0)
    u1 = jnp.where(row == 0, prev[SUBLANES - 1:SUBLANES, :], pltpu.roll(u, 1, axis=0))
    u2 = jnp.where(row == 0, prev[SUBLANES - 2:SUBLANES - 1, :],
                   jnp.where(row == 1, prev[SUBLANES - 1:SUBLANES, :], pltpu.roll(u, 2, axis=0)))
    cw = cw_ref[...]
    conv = cw[0:1, :] * u2 + cw[1:2, :] * u1 + cw[2:3, :] * u
    gated = b_gate * conv
    half_rows = h.shape[0] // 2
    for part in range(2):
        rows = slice(part * half_rows, (part + 1) * half_rows)
        z_conv = _dot_nt(h_ref[rows, :], wzc_sc[...])
        o_ref[rows, :] = (gated[rows, :] * (z_conv * _sigmoid(z_conv))).astype(BF16)
    carry_ref[...] = u[u.shape[0] - SUBLANES:, :]


def _conv(h, w_in_t, conv_w, tm=2048, tn=256):
    s, d = h.shape
    seg_specs = [_wt_rows(tn, d, lambda j, start=start: start + j * tn)
                 for start in (IN_C_IN, IN_B_GATE, IN_C_GATE, IN_Z_CONV)]
    return pl.pallas_call(
        _conv_kernel,
        grid=(CONV_WIDTH // tn, s // tm),
        in_specs=[pl.BlockSpec((tm, d), lambda j, i: (i, 0))] + seg_specs
        + [pl.BlockSpec((CONV_K, tn), lambda j, i: (0, j))],
        out_specs=pl.BlockSpec((tm, tn), lambda j, i: (i, j)),
        out_shape=jax.ShapeDtypeStruct((s, CONV_WIDTH), BF16),
        scratch_shapes=[pltpu.VMEM((tn, d), BF16)] * 4 + [pltpu.VMEM((SUBLANES, tn), F32)],
        compiler_params=_params(("arbitrary", "arbitrary")),
        name="conv",
    )(h, *([w_in_t] * 4), conv_w)


def _attn_kernel(qta_ref, qtb_ref, k_ref, vt_ref, o_ref, s0_ref, s1_ref, m_ref, acc_ref):
    i = pl.program_id(1)
    tk, tq = ATTN_TK, ATTN_TQ
    all_q = slice(0, tq)
    early_q = slice(0, tk)
    late_q = slice(tk, tq)
    heads = range(ATTN_HEADS)

    def qk(qt_ref, hd, j, s_ref, qs):
        k = k_ref[hd, pl.ds(pl.multiple_of(j * tk, tk), tk), :]
        s_ref[hd, :, qs] = _dot(k, qt_ref[hd, :, qs])

    def softmax_pv(slot, hd, j, s_ref, qs, keep=None):
        s = s_ref[hd, :, qs]
        if keep is not None:
            s = jnp.where(keep, s, -jnp.inf)
        m_prev = m_ref[slot, hd, :, qs]
        m_new = jnp.maximum(m_prev, jnp.max(s, axis=0, keepdims=True))
        alpha = jnp.exp2(m_prev - m_new)
        p = jnp.exp2(s - m_new).astype(BF16)
        acc_ref[slot, hd, :, qs] = alpha * acc_ref[slot, hd, :, qs] + _dot(vt_ref[hd, j], p)
        m_ref[slot, hd, :, qs] = m_new

    def first_scores(qt_ref):
        for hd in heads:
            qk(qt_ref, hd, 0, s0_ref, all_q)

    def pair(slot, qt_ref, j0):
        for hd in heads:
            qk(qt_ref, hd, j0 + 1, s1_ref, all_q)
            softmax_pv(slot, hd, j0, s0_ref, all_q)
        for hd in heads:
            qk(qt_ref, hd, j0 + 2, s0_ref, all_q)
            softmax_pv(slot, hd, j0 + 1, s1_ref, all_q)

    def paired_trips(slot, qt_ref, qi):
        def trip(jj, carry):
            pair(slot, qt_ref, 4 * jj)
            pair(slot, qt_ref, 4 * jj + 2)
            return carry

        lax.fori_loop(0, qi // 2, trip, 0)

    def finish(slot, qt_ref, qi, odd, then=None):
        if odd:
            pair(slot, qt_ref, 2 * qi - 2)
        diagonal_blocks(slot, qt_ref, qi)
        if then is not None:
            then()

    def diagonal_blocks(slot, qt_ref, qi):
        keep = (lax.broadcasted_iota(jnp.int32, (tk, tk), 0) <= lax.broadcasted_iota(jnp.int32, (tk, tk), 1))
        for hd in heads:
            qk(qt_ref, hd, 2 * qi + 1, s1_ref, late_q)
            softmax_pv(slot, hd, 2 * qi, s0_ref, early_q, keep)
            softmax_pv(slot, hd, 2 * qi, s0_ref, late_q)
        for hd in heads:
            softmax_pv(slot, hd, 2 * qi + 1, s1_ref, late_q, keep)
        for hd in heads:
            acc = acc_ref[slot, hd]
            out_t = acc[:V_HEAD_DIM, :] / acc[V_HEAD_DIM:V_HEAD_DIM + 1, :]
            o_ref[slot, :, hd * V_HEAD_DIM:(hd + 1) * V_HEAD_DIM] = out_t.T.astype(BF16)

    m_ref[...] = jnp.full_like(m_ref, -jnp.inf)
    acc_ref[...] = jnp.zeros_like(acc_ref)
    n_half = pl.num_programs(1)
    qi_a, qi_b = i, i + n_half
    is_odd = i % 2 == 1
    first_scores(qta_ref)
    paired_trips(0, qta_ref, qi_a)
    for odd in (True, False):
        @pl.when(is_odd == odd)
        def _(odd=odd):
            finish(0, qta_ref, qi_a, odd, then=lambda: first_scores(qtb_ref))
    paired_trips(1, qtb_ref, qi_b)
    for odd in (True, False):
        @pl.when(is_odd == odd)
        def _(odd=odd):
            finish(1, qtb_ref, qi_b, odd)


def _attention(qt, k, vt):
    nh, s, _ = k.shape
    tk, tq, hpb = ATTN_TK, ATTN_TQ, ATTN_HEADS
    n_half = s // tq // 2
    return pl.pallas_call(
        _attn_kernel,
        grid=(nh // hpb, n_half),
        in_specs=[pl.BlockSpec((hpb, QK_PAD, tq), lambda h, i: (h, 0, i)),
                  pl.BlockSpec((hpb, QK_PAD, tq), lambda h, i: (h, 0, i + n_half)),
                  pl.BlockSpec((hpb, s, QK_PAD), lambda h, i: (h, 0, 0)),
                  pl.BlockSpec((hpb, s // tk, VT_PAD, tk), lambda h, i: (h, 0, 0, 0))],
        out_specs=pl.BlockSpec((2, tq, hpb * V_HEAD_DIM), lambda h, i: (0, i, h)),
        out_shape=jax.ShapeDtypeStruct((2, s // 2, nh * V_HEAD_DIM), BF16),
        scratch_shapes=[pltpu.VMEM((hpb, tk, tq), F32), pltpu.VMEM((hpb, tk, tq), F32),
                        pltpu.VMEM((2, hpb, 1, tq), F32), pltpu.VMEM((2, hpb, VT_PAD, tq), F32)],
        compiler_params=_params(("parallel", "arbitrary")),
        name="attention",
    )(qt, qt, k, vt)


def _output_kernel(x_ref, attn_ref, sz_ref, sgm_ref, co_ref, sgc_ref,
                   womla_ref, woconv_ref, wout_ref, g_ref, o_ref):
    a = (attn_ref[...].astype(F32) * sz_ref[...].astype(F32)).astype(BF16)
    y_mla = _dot(a, womla_ref[...])
    y_conv = _dot(co_ref[...], woconv_ref[...])
    merged = sgm_ref[...].astype(F32) * y_mla + sgc_ref[...].astype(F32) * y_conv
    out = _dot(merged.astype(BF16), wout_ref[...])
    o_ref[...] = x_ref[...] + _rms(out, g_ref[...])


def _output(x, attn, gates, co, womla, woconv, wout, g, tm=256):
    s, d = x.shape
    row = lambda c: pl.BlockSpec((tm, d), lambda i: (i, c))
    return pl.pallas_call(
        _output_kernel,
        grid=(s // tm,),
        in_specs=[row(0), row(0), row(0), row(1), row(0), row(2),
                  _resident(womla.shape), _resident(woconv.shape), _resident(wout.shape),
                  _resident(g.shape)],
        out_specs=row(0),
        out_shape=jax.ShapeDtypeStruct((s, d), F32),
        compiler_params=_params(("parallel",)),
        name="output",
    )(x, attn, gates, gates, co, gates, womla, woconv, wout, g)


def _prep_up_weights_kernel(wq_ref, wkv_ref, wqt_ref, wk_ref, wvt_ref):
    wq_t = wq_ref[...].T
    half = ROPE_DIM // 2
    for hd in range(N_HEADS):
        src = hd * (QK_NOPE_DIM + ROPE_DIM)
        dst = hd * QK_PAD
        rope = wq_t[src + QK_NOPE_DIM:src + QK_NOPE_DIM + ROPE_DIM, :].astype(BF16)
        wqt_ref[dst:dst + QK_NOPE_DIM, :] = wq_t[src:src + QK_NOPE_DIM, :].astype(BF16)
        wqt_ref[dst + QK_NOPE_DIM:dst + QK_NOPE_DIM + ROPE_DIM, :] = rope
        wqt_ref[dst + QK_NOPE_DIM + ROPE_DIM:dst + QK_NOPE_DIM + ROPE_DIM + half, :] = rope[half:, :]
        wqt_ref[dst + QK_NOPE_DIM + ROPE_DIM + half:dst + QK_PAD, :] = rope[:half, :]
        kv0 = hd * (QK_NOPE_DIM + V_HEAD_DIM)
        wk_ref[:, hd * QK_NOPE_DIM:(hd + 1) * QK_NOPE_DIM] = wkv_ref[:, kv0:kv0 + QK_NOPE_DIM].astype(BF16)
        wvt_ref[hd * V_HEAD_DIM:(hd + 1) * V_HEAD_DIM, :] = (
            wkv_ref[:, kv0 + QK_NOPE_DIM:kv0 + QK_NOPE_DIM + V_HEAD_DIM].T.astype(BF16))


def _prep_up_weights(w_q_b, w_kv_b):
    return pl.pallas_call(
        _prep_up_weights_kernel,
        out_shape=[jax.ShapeDtypeStruct((N_HEADS * QK_PAD, Q_LORA_RANK), BF16),
                   jax.ShapeDtypeStruct((KV_LORA_RANK, N_HEADS * QK_NOPE_DIM), BF16),
                   jax.ShapeDtypeStruct((N_HEADS * V_HEAD_DIM, KV_LORA_RANK), BF16)],
        compiler_params=pltpu.CompilerParams(vmem_limit_bytes=VMEM_LIMIT),
        name="prep_up_weights",
    )(w_q_b, w_kv_b)


def kernel(x, positions, pre_norm_g, w_in, q_a_norm_g, w_q_b, kv_a_norm_g, w_kv_b, conv_w,
           w_o_mla, w_o_conv, w_out, post_norm_g):
    b, s, d = x.shape
    assert b == 1 and d == D_MODEL
    x2 = x[0]

    w_in_t = w_in.T
    wqt, wk, wvt = _prep_up_weights(w_q_b, w_kv_b)

    inv_freq = ROPE_THETA ** (-jnp.arange(0, ROPE_DIM, 2, dtype=F32) / ROPE_DIM)
    ang = inv_freq[:, None] * positions[0].astype(F32)[None, :]
    cos, sin = jnp.cos(ang), jnp.sin(ang)
    zero_rows = jnp.zeros((LANES - ROPE_DIM, s), F32)
    cos_rows = jnp.concatenate([cos, cos, zero_rows], axis=0)
    sin_rows = jnp.concatenate([-sin, sin, zero_rows], axis=0)

    h, qt, k, vt = _latent(x2, pre_norm_g.reshape(1, d), w_in_t, q_a_norm_g.reshape(1, -1),
                           kv_a_norm_g.reshape(1, -1), wqt, wk, wvt, cos_rows, sin_rows)
    gates, womla, woconv, wout = _gates(h, w_in_t, (w_o_mla, w_o_conv, w_out))
    co = _conv(h, w_in_t, conv_w)
    attn = _attention(qt, k, vt).reshape(s, MLA_WIDTH)
    out = _output(x2, attn, gates, co, womla, woconv, wout, post_norm_g.reshape(1, d))
    return out[None]
```

```python
import functools
import math

import jax
import jax.numpy as jnp
from jax import lax
from jax.experimental import pallas as pl
from jax.experimental.pallas import tpu as pltpu

D_MODEL = 2048
N_HEADS = 16
QK_NOPE_DIM = 128
ROPE_DIM = 64
V_HEAD_DIM = 128
Q_LORA_RANK = 512
KV_LORA_RANK = 512
MLA_WIDTH = N_HEADS * V_HEAD_DIM
CONV_WIDTH = D_MODEL
CONV_K = 3
ROPE_THETA = 10000.0
RMS_EPS = 1e-6

LANES = 128
SUBLANES = 8
QK_PAD = QK_NOPE_DIM + LANES
BF16_SUBLANES = 16
VT_PAD = V_HEAD_DIM + BF16_SUBLANES
ATTN_TK = 512
ATTN_TQ = 2 * ATTN_TK
ATTN_HEADS = 2
ATTN_QBLOCKS = 4
HEAD_GROUP = 4
LAT_WIDTH = Q_LORA_RANK + KV_LORA_RANK + LANES
LAT_IN = Q_LORA_RANK + KV_LORA_RANK + ROPE_DIM
IN_Z_MLA = LAT_IN
IN_C_IN = IN_Z_MLA + MLA_WIDTH
IN_B_GATE = IN_C_IN + CONV_WIDTH
IN_C_GATE = IN_B_GATE + CONV_WIDTH
IN_Z_CONV = IN_C_GATE + CONV_WIDTH
IN_G_MLA = IN_Z_CONV + CONV_WIDTH
IN_G_CONV = IN_G_MLA + D_MODEL
VMEM_LIMIT = 56 * 1024 * 1024

BF16 = jnp.bfloat16
F32 = jnp.float32


def _params(semantics):
    return pltpu.CompilerParams(dimension_semantics=semantics, vmem_limit_bytes=VMEM_LIMIT)


def _resident(shape):
    return pl.BlockSpec(shape, lambda *_: (0,) * len(shape), pipeline_mode=pl.Buffered(1))


def _rms(xf, g):
    r = lax.rsqrt(jnp.mean(xf * xf, axis=-1, keepdims=True) + RMS_EPS)
    return xf * r * g


def _sigmoid(x):
    return 0.5 * jnp.tanh(0.5 * x) + 0.5


def _dot(a, b):
    return jnp.dot(a, b, preferred_element_type=F32)


def _dot_nt(a, b):
    return lax.dot_general(a, b, (((1,), (1,)), ((), ())), preferred_element_type=F32)


def _transpose_cast(dst_sc, src_ref, n_rows):
    for c in range(0, n_rows, LANES):
        dst_sc[:, c:c + LANES] = src_ref[c:c + LANES, :].T.astype(BF16)


def _wt_rows(tn, d, first_row):
    return pl.BlockSpec((pl.Element(tn), pl.Element(d)),
                        lambda j, i: (pl.multiple_of(first_row(j), SUBLANES), 0))


def _rope128(g2, cos_t, sin_t):
    return g2 * cos_t + pltpu.roll(g2, ROPE_DIM, axis=1) * sin_t


def _latent_kernel(x_ref, gpre_ref, win_ref, gq_ref, gkv_ref, wqt_ref, wk_ref, wvt_ref, cos_ref, sin_ref,
                   h_ref, qt_ref, k_ref, vt_ref, wlat_sc, *, scale):
    @pl.when(pl.program_id(0) == 0)
    def _():
        half = ROPE_DIM // 2
        _transpose_cast(wlat_sc, win_ref, LAT_IN - ROPE_DIM)
        rope_rows = jnp.concatenate([win_ref[LAT_IN - ROPE_DIM:LAT_IN, :], win_ref[LAT_IN - half:LAT_IN, :],
                                     win_ref[LAT_IN - ROPE_DIM:LAT_IN - half, :]], axis=0)
        wlat_sc[:, LAT_IN - ROPE_DIM:] = rope_rows.T.astype(BF16)

    h = _rms(x_ref[...], gpre_ref[...]).astype(BF16)
    h_ref[...] = h
    lat = _dot(h, wlat_sc[...])
    qa = _rms(lat[:, :Q_LORA_RANK], gq_ref[...]).astype(BF16)
    ckv = _rms(lat[:, Q_LORA_RANK:Q_LORA_RANK + KV_LORA_RANK], gkv_ref[...]).astype(BF16)
    cos_rows = cos_ref[...]
    sin_rows = sin_ref[...]
    k_rope = _rope128(lat[:, Q_LORA_RANK + KV_LORA_RANK:], cos_rows.T, sin_rows.T).astype(BF16)
    tm = qa.shape[0]
    vt = _dot_nt(wvt_ref[...], ckv)
    ones_rows = jnp.ones((VT_PAD - V_HEAD_DIM, tm), BF16)
    for grp in range(N_HEADS // HEAD_GROUP):
        qt = _dot_nt(wqt_ref[grp * HEAD_GROUP * QK_PAD:(grp + 1) * HEAD_GROUP * QK_PAD, :], qa) * scale
        kn = _dot(ckv, wk_ref[:, grp * HEAD_GROUP * QK_NOPE_DIM:(grp + 1) * HEAD_GROUP * QK_NOPE_DIM])
        for sub in range(HEAD_GROUP):
            hd = grp * HEAD_GROUP + sub
            base = sub * QK_PAD
            qt_ref[hd, :QK_NOPE_DIM, :] = qt[base:base + QK_NOPE_DIM, :].astype(BF16)
            g2 = qt[base + QK_NOPE_DIM:base + QK_PAD, :]
            g2_swapped = jnp.concatenate([g2[ROPE_DIM:], g2[:ROPE_DIM]], axis=0)
            qt_ref[hd, QK_NOPE_DIM:, :] = (g2 * cos_rows + g2_swapped * sin_rows).astype(BF16)
            k_ref[hd, :, :QK_NOPE_DIM] = kn[:, sub * QK_NOPE_DIM:(sub + 1) * QK_NOPE_DIM].astype(BF16)
            k_ref[hd, :, QK_NOPE_DIM:] = k_rope
            vt_ref[hd, 0, :V_HEAD_DIM, :] = vt[hd * V_HEAD_DIM:(hd + 1) * V_HEAD_DIM, :].astype(BF16)
            vt_ref[hd, 0, V_HEAD_DIM:, :] = ones_rows


def _latent(x, gpre, w_in_t, gq, gkv, wqt, wk, wvt, cos_rows, sin_rows, tm=256):
    s, d = x.shape
    per_blk = ATTN_TK // tm
    win_spec = pl.BlockSpec((pl.Element(LAT_IN), pl.Element(d)), lambda i: (0, 0),
                            pipeline_mode=pl.Buffered(1))
    scale = math.log2(math.e) / math.sqrt(QK_NOPE_DIM + ROPE_DIM)
    return pl.pallas_call(
        functools.partial(_latent_kernel, scale=scale),
        grid=(s // tm,),
        in_specs=[
            pl.BlockSpec((tm, d), lambda i: (i, 0)), _resident(gpre.shape),
            win_spec, _resident(gq.shape), _resident(gkv.shape),
            _resident(wqt.shape), _resident(wk.shape), _resident(wvt.shape),
            pl.BlockSpec((LANES, tm), lambda i: (0, i)),
            pl.BlockSpec((LANES, tm), lambda i: (0, i)),
        ],
        out_specs=[
            pl.BlockSpec((tm, d), lambda i: (i, 0)),
            pl.BlockSpec((N_HEADS, QK_PAD, tm), lambda i: (0, 0, i)),
            pl.BlockSpec((N_HEADS, tm, QK_PAD), lambda i: (0, i, 0)),
            pl.BlockSpec((N_HEADS, 1, VT_PAD, tm), lambda i: (0, i // per_blk, 0, i % per_blk)),
        ],
        out_shape=[
            jax.ShapeDtypeStruct((s, d), BF16),
            jax.ShapeDtypeStruct((N_HEADS, QK_PAD, s), BF16),
            jax.ShapeDtypeStruct((N_HEADS, s, QK_PAD), BF16),
            jax.ShapeDtypeStruct((N_HEADS, s // ATTN_TK, VT_PAD, ATTN_TK), BF16),
        ],
        scratch_shapes=[pltpu.VMEM((d, LAT_WIDTH), BF16)],
        compiler_params=_params(("arbitrary",)),
        name="latent",
    )(x, gpre, w_in_t, gq, gkv, wqt, wk, wvt, cos_rows, sin_rows)


def _gates_kernel(h_ref, w_ref, f32a_ref, f32b_ref, f32c_ref, o_ref, bf16a_ref, bf16b_ref, bf16c_ref, w_sc,
                  *, n_silu_blocks):
    @pl.when(pl.program_id(1) == 0)
    def _():
        w_sc[...] = w_ref[...].astype(BF16)

    bf16a_ref[...] = f32a_ref[...].astype(BF16)
    bf16b_ref[...] = f32b_ref[...].astype(BF16)
    bf16c_ref[...] = f32c_ref[...].astype(BF16)

    r = _dot_nt(h_ref[...], w_sc[...])
    sg = _sigmoid(r)
    is_silu = pl.program_id(0) < n_silu_blocks
    o_ref[...] = jnp.where(is_silu, r * sg, sg).astype(BF16)


def _gates(h, w_in_t, side_casts, tm=1024, tn=1024):
    s, d = h.shape
    n_silu_blocks = MLA_WIDTH // tn
    gap = IN_G_MLA - IN_Z_MLA - MLA_WIDTH
    n_j, n_i = 3 * D_MODEL // tn, s // tm
    cast_rows = next(r for r in range(BF16_SUBLANES, d + 1, BF16_SUBLANES)
                     if d % r == 0 and d // r <= n_j * n_i)
    last_cast_block = d // cast_rows - 1
    cast_spec = pl.BlockSpec((cast_rows, d), lambda j, i: (jnp.minimum(j * n_i + i, last_cast_block), 0))
    return pl.pallas_call(
        functools.partial(_gates_kernel, n_silu_blocks=n_silu_blocks),
        grid=(n_j, n_i),
        in_specs=[pl.BlockSpec((tm, d), lambda j, i: (i, 0)),
                  _wt_rows(tn, d, lambda j: IN_Z_MLA + j * tn + jnp.where(j < n_silu_blocks, 0, gap)),
                  cast_spec, cast_spec, cast_spec],
        out_specs=[pl.BlockSpec((tm, tn), lambda j, i: (i, j)), cast_spec, cast_spec, cast_spec],
        out_shape=[jax.ShapeDtypeStruct((s, 3 * D_MODEL), BF16)]
        + [jax.ShapeDtypeStruct(w.shape, BF16) for w in side_casts],
        scratch_shapes=[pltpu.VMEM((tn, d), BF16)],
        compiler_params=_params(("arbitrary", "arbitrary")),
        name="gates",
    )(h, w_in_t, *side_casts)


def _conv_kernel(h_ref, wci_ref, wbg_ref, wcg_ref, wzc_ref, cw_ref, o_ref,
                 wci_sc, wbg_sc, wcg_sc, wzc_sc, carry_ref):
    i = pl.program_id(1)

    @pl.when(i == 0)
    def _():
        carry_ref[...] = jnp.zeros_like(carry_ref)
        wci_sc[...] = wci_ref[...].astype(BF16)
        wbg_sc[...] = wbg_ref[...].astype(BF16)
        wcg_sc[...] = wcg_ref[...].astype(BF16)
        wzc_sc[...] = wzc_ref[...].astype(BF16)

    h = h_ref[...]
    u = _dot_nt(h, wcg_sc[...]) * _dot_nt(h, wci_sc[...])
    b_gate = _dot_nt(h, wbg_sc[...])
    prev = carry_ref[...]
    row = lax.broadcasted_iota(jnp.int32, u.shape, 0)
    u1 = jnp.where(row == 0, prev[SUBLANES - 1:SUBLANES, :], pltpu.roll(u, 1, axis=0))
    u2 = jnp.where(row == 0, prev[SUBLANES - 2:SUBLANES - 1, :],
                   jnp.where(row == 1, prev[SUBLANES - 1:SUBLANES, :], pltpu.roll(u, 2, axis=0)))
    cw = cw_ref[...]
    conv = cw[0:1, :] * u2 + cw[1:2, :] * u1 + cw[2:3, :] * u
    gated = b_gate * conv
    half_rows = h.shape[0] // 2
    for part in range(2):
        rows = slice(part * half_rows, (part + 1) * half_rows)
        z_conv = _dot_nt(h_ref[rows, :], wzc_sc[...])
        o_ref[rows, :] = (gated[rows, :] * (z_conv * _sigmoid(z_conv))).astype(BF16)
    carry_ref[...] = u[u.shape[0] - SUBLANES:, :]


def _conv(h, w_in_t, conv_w, tm=2048, tn=256):
    s, d = h.shape
    seg_specs = [_wt_rows(tn, d, lambda j, start=start: start + j * tn)
                 for start in (IN_C_IN, IN_B_GATE, IN_C_GATE, IN_Z_CONV)]
    return pl.pallas_call(
        _conv_kernel,
        grid=(CONV_WIDTH // tn, s // tm),
        in_specs=[pl.BlockSpec((tm, d), lambda j, i: (i, 0))] + seg_specs
        + [pl.BlockSpec((CONV_K, tn), lambda j, i: (0, j))],
        out_specs=pl.BlockSpec((tm, tn), lambda j, i: (i, j)),
        out_shape=jax.ShapeDtypeStruct((s, CONV_WIDTH), BF16),
        scratch_shapes=[pltpu.VMEM((tn, d), BF16)] * 4 + [pltpu.VMEM((SUBLANES, tn), F32)],
        compiler_params=_params(("arbitrary", "arbitrary")),
        name="conv",
    )(h, *([w_in_t] * 4), conv_w)


def _attn_kernel(*refs):
    qt_refs = refs[:ATTN_QBLOCKS]
    k_ref, vt_ref, o_ref, s0_ref, s1_ref, m_ref, acc_ref = refs[ATTN_QBLOCKS:]
    i = pl.program_id(1)
    tk, tq = ATTN_TK, ATTN_TQ
    all_q = slice(0, tq)
    early_q = slice(0, tk)
    late_q = slice(tk, tq)
    heads = range(ATTN_HEADS)

    def qk(qt_ref, hd, j, s_ref, qs):
        k = k_ref[hd, pl.ds(pl.multiple_of(j * tk, tk), tk), :]
        s_ref[hd, :, qs] = _dot(k, qt_ref[hd, :, qs])

    def softmax_pv(slot, hd, j, s_ref, qs, keep=None):
        s = s_ref[hd, :, qs]
        if keep is not None:
            s = jnp.where(keep, s, -jnp.inf)
        m_prev = m_ref[slot, hd, :, qs]
        m_new = jnp.maximum(m_prev, jnp.max(s, axis=0, keepdims=True))
        alpha = jnp.exp2(m_prev - m_new)
        p = jnp.exp2(s - m_new).astype(BF16)
        acc_ref[slot, hd, :, qs] = alpha * acc_ref[slot, hd, :, qs] + _dot(vt_ref[hd, j], p)
        m_ref[slot, hd, :, qs] = m_new

    def first_scores(qt_ref):
        for hd in heads:
            qk(qt_ref, hd, 0, s0_ref, all_q)

    def pair(slot, qt_ref, j0):
        for hd in heads:
            qk(qt_ref, hd, j0 + 1, s1_ref, all_q)
            softmax_pv(slot, hd, j0, s0_ref, all_q)
        for hd in heads:
            qk(qt_ref, hd, j0 + 2, s0_ref, all_q)
            softmax_pv(slot, hd, j0 + 1, s1_ref, all_q)

    def paired_trips(slot, qt_ref, qi):
        def trip(jj, carry):
            pair(slot, qt_ref, 4 * jj)
            pair(slot, qt_ref, 4 * jj + 2)
            return carry

        lax.fori_loop(0, qi // 2, trip, 0)

    def finish(slot, out_idx, qt_ref, qi, odd, next_qt_ref):
        if odd:
            pair(slot, qt_ref, 2 * qi - 2)
        diagonal_blocks(slot, out_idx, qt_ref, qi)
        if next_qt_ref is not None:
            first_scores(next_qt_ref)

    def reset(slot):
        m_ref[slot] = jnp.full(m_ref.shape[1:], -jnp.inf, F32)
        acc_ref[slot] = jnp.zeros(acc_ref.shape[1:], F32)

    def diagonal_blocks(slot, out_idx, qt_ref, qi):
        keep = (lax.broadcasted_iota(jnp.int32, (tk, tk), 0) <= lax.broadcasted_iota(jnp.int32, (tk, tk), 1))
        for hd in heads:
            qk(qt_ref, hd, 2 * qi + 1, s1_ref, late_q)
            softmax_pv(slot, hd, 2 * qi, s0_ref, early_q, keep)
            softmax_pv(slot, hd, 2 * qi, s0_ref, late_q)
        for hd in heads:
            softmax_pv(slot, hd, 2 * qi + 1, s1_ref, late_q, keep)
        for hd in heads:
            acc = acc_ref[slot, hd]
            out_t = acc[:V_HEAD_DIM, :] / acc[V_HEAD_DIM:V_HEAD_DIM + 1, :]
            o_ref[out_idx, :, hd * V_HEAD_DIM:(hd + 1) * V_HEAD_DIM] = out_t.T.astype(BF16)

    n_steps = pl.num_programs(1)
    is_odd = i % 2 == 1
    reset(0)
    reset(1)
    first_scores(qt_refs[0])
    for c in range(ATTN_QBLOCKS):
        qi = i + c * n_steps
        slot = c % 2
        next_qt_ref = qt_refs[c + 1] if c + 1 < ATTN_QBLOCKS else None
        paired_trips(slot, qt_refs[c], qi)
        for odd in (True, False):
            @pl.when(is_odd == odd)
            def _(odd=odd, slot=slot, c=c, qi=qi, next_qt_ref=next_qt_ref):
                finish(slot, c, qt_refs[c], qi, odd, next_qt_ref)
        if c + 2 < ATTN_QBLOCKS:
            reset(slot)


def _attention(qt, k, vt):
    nh, s, _ = k.shape
    tk, tq, hpb, nqb = ATTN_TK, ATTN_TQ, ATTN_HEADS, ATTN_QBLOCKS
    n_steps = s // tq // nqb
    assert n_steps % 2 == 0
    qt_specs = [pl.BlockSpec((hpb, QK_PAD, tq), lambda h, i, c=c: (h, 0, i + c * n_steps)) for c in range(nqb)]
    return pl.pallas_call(
        _attn_kernel,
        grid=(nh // hpb, n_steps),
        in_specs=qt_specs + [pl.BlockSpec((hpb, s, QK_PAD), lambda h, i: (h, 0, 0)),
                             pl.BlockSpec((hpb, s // tk, VT_PAD, tk), lambda h, i: (h, 0, 0, 0))],
        out_specs=pl.BlockSpec((nqb, tq, hpb * V_HEAD_DIM), lambda h, i: (0, i, h)),
        out_shape=jax.ShapeDtypeStruct((nqb, s // nqb, nh * V_HEAD_DIM), BF16),
        scratch_shapes=[pltpu.VMEM((hpb, tk, tq), F32), pltpu.VMEM((hpb, tk, tq), F32),
                        pltpu.VMEM((2, hpb, 1, tq), F32), pltpu.VMEM((2, hpb, VT_PAD, tq), F32)],
        compiler_params=_params(("parallel", "arbitrary")),
        name="attention",
    )(*([qt] * nqb), k, vt)


def _output_kernel(x_ref, attn_ref, sz_ref, sgm_ref, co_ref, sgc_ref,
                   womla_ref, woconv_ref, wout_ref, g_ref, o_ref):
    a = (attn_ref[...].astype(F32) * sz_ref[...].astype(F32)).astype(BF16)
    y_mla = _dot(a, womla_ref[...])
    y_conv = _dot(co_ref[...], woconv_ref[...])
    merged = sgm_ref[...].astype(F32) * y_mla + sgc_ref[...].astype(F32) * y_conv
    out = _dot(merged.astype(BF16), wout_ref[...])
    o_ref[...] = x_ref[...] + _rms(out, g_ref[...])


def _output(x, attn, gates, co, womla, woconv, wout, g, tm=256):
    s, d = x.shape
    row = lambda c: pl.BlockSpec((tm, d), lambda i: (i, c))
    return pl.pallas_call(
        _output_kernel,
        grid=(s // tm,),
        in_specs=[row(0), row(0), row(0), row(1), row(0), row(2),
                  _resident(womla.shape), _resident(woconv.shape), _resident(wout.shape),
                  _resident(g.shape)],
        out_specs=row(0),
        out_shape=jax.ShapeDtypeStruct((s, d), F32),
        compiler_params=_params(("parallel",)),
        name="output",
    )(x, attn, gates, gates, co, gates, womla, woconv, wout, g)


def _prep_up_weights_kernel(wq_ref, wkv_ref, wqt_ref, wk_ref, wvt_ref):
    wq_t = wq_ref[...].T
    half = ROPE_DIM // 2
    for hd in range(N_HEADS):
        src = hd * (QK_NOPE_DIM + ROPE_DIM)
        dst = hd * QK_PAD
        rope = wq_t[src + QK_NOPE_DIM:src + QK_NOPE_DIM + ROPE_DIM, :].astype(BF16)
        wqt_ref[dst:dst + QK_NOPE_DIM, :] = wq_t[src:src + QK_NOPE_DIM, :].astype(BF16)
        wqt_ref[dst + QK_NOPE_DIM:dst + QK_NOPE_DIM + ROPE_DIM, :] = rope
        wqt_ref[dst + QK_NOPE_DIM + ROPE_DIM:dst + QK_NOPE_DIM + ROPE_DIM + half, :] = rope[half:, :]
        wqt_ref[dst + QK_NOPE_DIM + ROPE_DIM + half:dst + QK_PAD, :] = rope[:half, :]
        kv0 = hd * (QK_NOPE_DIM + V_HEAD_DIM)
        wk_ref[:, hd * QK_NOPE_DIM:(hd + 1) * QK_NOPE_DIM] = wkv_ref[:, kv0:kv0 + QK_NOPE_DIM].astype(BF16)
        wvt_ref[hd * V_HEAD_DIM:(hd + 1) * V_HEAD_DIM, :] = (
            wkv_ref[:, kv0 + QK_NOPE_DIM:kv0 + QK_NOPE_DIM + V_HEAD_DIM].T.astype(BF16))


def _prep_up_weights(w_q_b, w_kv_b):
    return pl.pallas_call(
        _prep_up_weights_kernel,
        out_shape=[jax.ShapeDtypeStruct((N_HEADS * QK_PAD, Q_LORA_RANK), BF16),
                   jax.ShapeDtypeStruct((KV_LORA_RANK, N_HEADS * QK_NOPE_DIM), BF16),
                   jax.ShapeDtypeStruct((N_HEADS * V_HEAD_DIM, KV_LORA_RANK), BF16)],
        compiler_params=pltpu.CompilerParams(vmem_limit_bytes=VMEM_LIMIT),
        name="prep_up_weights",
    )(w_q_b, w_kv_b)


def kernel(x, positions, pre_norm_g, w_in, q_a_norm_g, w_q_b, kv_a_norm_g, w_kv_b, conv_w,
           w_o_mla, w_o_conv, w_out, post_norm_g):
    b, s, d = x.shape
    assert b == 1 and d == D_MODEL
    x2 = x[0]

    w_in_t = w_in.T
    wqt, wk, wvt = _prep_up_weights(w_q_b, w_kv_b)

    inv_freq = ROPE_THETA ** (-jnp.arange(0, ROPE_DIM, 2, dtype=F32) / ROPE_DIM)
    ang = inv_freq[:, None] * positions[0].astype(F32)[None, :]
    cos, sin = jnp.cos(ang), jnp.sin(ang)
    zero_rows = jnp.zeros((LANES - ROPE_DIM, s), F32)
    cos_rows = jnp.concatenate([cos, cos, zero_rows], axis=0)
    sin_rows = jnp.concatenate([-sin, sin, zero_rows], axis=0)

    h, qt, k, vt = _latent(x2, pre_norm_g.reshape(1, d), w_in_t, q_a_norm_g.reshape(1, -1),
                           kv_a_norm_g.reshape(1, -1), wqt, wk, wvt, cos_rows, sin_rows)
    gates, womla, woconv, wout = _gates(h, w_in_t, (w_o_mla, w_o_conv, w_out))
    co = _conv(h, w_in_t, conv_w)
    attn = _attention(qt, k, vt).reshape(s, MLA_WIDTH)
    out = _output(x2, attn, gates, co, womla, woconv, wout, post_norm_g.reshape(1, d))
    return out[None]
```

```python
import functools
import math

import jax
import jax.numpy as jnp
from jax import lax
from jax.experimental import pallas as pl
from jax.experimental.pallas import tpu as pltpu

D_MODEL = 2048
N_HEADS = 16
QK_NOPE_DIM = 128
ROPE_DIM = 64
V_HEAD_DIM = 128
Q_LORA_RANK = 512
KV_LORA_RANK = 512
MLA_WIDTH = N_HEADS * V_HEAD_DIM
CONV_WIDTH = D_MODEL
CONV_K = 3
ROPE_THETA = 10000.0
RMS_EPS = 1e-6

LANES = 128
SUBLANES = 8
QK_PAD = QK_NOPE_DIM + LANES
BF16_SUBLANES = 16
VT_PAD = V_HEAD_DIM + BF16_SUBLANES
ATTN_TK = 512
ATTN_TQ = 2 * ATTN_TK
ATTN_HEADS = 2
HEAD_GROUP = 4
LAT_WIDTH = Q_LORA_RANK + KV_LORA_RANK + LANES
LAT_IN = Q_LORA_RANK + KV_LORA_RANK + ROPE_DIM
IN_Z_MLA = LAT_IN
IN_C_IN = IN_Z_MLA + MLA_WIDTH
IN_B_GATE = IN_C_IN + CONV_WIDTH
IN_C_GATE = IN_B_GATE + CONV_WIDTH
IN_Z_CONV = IN_C_GATE + CONV_WIDTH
IN_G_MLA = IN_Z_CONV + CONV_WIDTH
IN_G_CONV = IN_G_MLA + D_MODEL
VMEM_LIMIT = 56 * 1024 * 1024

BF16 = jnp.bfloat16
F32 = jnp.float32


def _params(semantics):
    return pltpu.CompilerParams(dimension_semantics=semantics, vmem_limit_bytes=VMEM_LIMIT)


def _resident(shape):
    return pl.BlockSpec(shape, lambda *_: (0,) * len(shape), pipeline_mode=pl.Buffered(1))


def _rms(xf, g):
    r = lax.rsqrt(jnp.mean(xf * xf, axis=-1, keepdims=True) + RMS_EPS)
    return xf * r * g


def _sigmoid(x):
    return 0.5 * jnp.tanh(0.5 * x) + 0.5


def _dot(a, b):
    return jnp.dot(a, b, preferred_element_type=F32)


def _dot_nt(a, b):
    return lax.dot_general(a, b, (((1,), (1,)), ((), ())), preferred_element_type=F32)


def _transpose_cast(dst_sc, src_ref, n_rows):
    for c in range(0, n_rows, LANES):
        dst_sc[:, c:c + LANES] = src_ref[c:c + LANES, :].T.astype(BF16)


def _wt_rows(tn, d, first_row):
    return pl.BlockSpec((pl.Element(tn), pl.Element(d)),
                        lambda j, i: (pl.multiple_of(first_row(j), SUBLANES), 0))


def _rope128(g2, cos_t, sin_t):
    return g2 * cos_t + pltpu.roll(g2, ROPE_DIM, axis=1) * sin_t


def _latent_kernel(x_ref, gpre_ref, win_ref, gq_ref, gkv_ref, wqt_ref, wk_ref, wvt_ref, cos_ref, sin_ref,
                   h_ref, qt_ref, k_ref, vt_ref, wlat_sc, *, scale):
    @pl.when(pl.program_id(0) == 0)
    def _():
        half = ROPE_DIM // 2
        _transpose_cast(wlat_sc, win_ref, LAT_IN - ROPE_DIM)
        rope_rows = jnp.concatenate([win_ref[LAT_IN - ROPE_DIM:LAT_IN, :], win_ref[LAT_IN - half:LAT_IN, :],
                                     win_ref[LAT_IN - ROPE_DIM:LAT_IN - half, :]], axis=0)
        wlat_sc[:, LAT_IN - ROPE_DIM:] = rope_rows.T.astype(BF16)

    h = _rms(x_ref[...], gpre_ref[...]).astype(BF16)
    h_ref[...] = h
    lat = _dot(h, wlat_sc[...])
    qa = _rms(lat[:, :Q_LORA_RANK], gq_ref[...]).astype(BF16)
    ckv = _rms(lat[:, Q_LORA_RANK:Q_LORA_RANK + KV_LORA_RANK], gkv_ref[...]).astype(BF16)
    cos_rows = cos_ref[...]
    sin_rows = sin_ref[...]
    k_rope = _rope128(lat[:, Q_LORA_RANK + KV_LORA_RANK:], cos_rows.T, sin_rows.T).astype(BF16)
    tm = qa.shape[0]
    vt = _dot_nt(wvt_ref[...], ckv)
    ones_rows = jnp.ones((VT_PAD - V_HEAD_DIM, tm), BF16)
    for grp in range(N_HEADS // HEAD_GROUP):
        qt = _dot_nt(wqt_ref[grp * HEAD_GROUP * QK_PAD:(grp + 1) * HEAD_GROUP * QK_PAD, :], qa) * scale
        kn = _dot(ckv, wk_ref[:, grp * HEAD_GROUP * QK_NOPE_DIM:(grp + 1) * HEAD_GROUP * QK_NOPE_DIM])
        for sub in range(HEAD_GROUP):
            hd = grp * HEAD_GROUP + sub
            base = sub * QK_PAD
            qt_ref[hd, :QK_NOPE_DIM, :] = qt[base:base + QK_NOPE_DIM, :].astype(BF16)
            g2 = qt[base + QK_NOPE_DIM:base + QK_PAD, :]
            g2_swapped = jnp.concatenate([g2[ROPE_DIM:], g2[:ROPE_DIM]], axis=0)
            qt_ref[hd, QK_NOPE_DIM:, :] = (g2 * cos_rows + g2_swapped * sin_rows).astype(BF16)
            k_ref[hd, :, :QK_NOPE_DIM] = kn[:, sub * QK_NOPE_DIM:(sub + 1) * QK_NOPE_DIM].astype(BF16)
            k_ref[hd, :, QK_NOPE_DIM:] = k_rope
            vt_ref[hd, 0, :V_HEAD_DIM, :] = vt[hd * V_HEAD_DIM:(hd + 1) * V_HEAD_DIM, :].astype(BF16)
            vt_ref[hd, 0, V_HEAD_DIM:, :] = ones_rows


def _latent(x, gpre, w_in_t, gq, gkv, wqt, wk, wvt, cos_rows, sin_rows, tm=256):
    s, d = x.shape
    per_blk = ATTN_TK // tm
    win_spec = pl.BlockSpec((pl.Element(LAT_IN), pl.Element(d)), lambda i: (0, 0),
                            pipeline_mode=pl.Buffered(1))
    scale = math.log2(math.e) / math.sqrt(QK_NOPE_DIM + ROPE_DIM)
    return pl.pallas_call(
        functools.partial(_latent_kernel, scale=scale),
        grid=(s // tm,),
        in_specs=[
            pl.BlockSpec((tm, d), lambda i: (i, 0)), _resident(gpre.shape),
            win_spec, _resident(gq.shape), _resident(gkv.shape),
            _resident(wqt.shape), _resident(wk.shape), _resident(wvt.shape),
            pl.BlockSpec((LANES, tm), lambda i: (0, i)),
            pl.BlockSpec((LANES, tm), lambda i: (0, i)),
        ],
        out_specs=[
            pl.BlockSpec((tm, d), lambda i: (i, 0)),
            pl.BlockSpec((N_HEADS, QK_PAD, tm), lambda i: (0, 0, i)),
            pl.BlockSpec((N_HEADS, tm, QK_PAD), lambda i: (0, i, 0)),
            pl.BlockSpec((N_HEADS, 1, VT_PAD, tm), lambda i: (0, i // per_blk, 0, i % per_blk)),
        ],
        out_shape=[
            jax.ShapeDtypeStruct((s, d), BF16),
            jax.ShapeDtypeStruct((N_HEADS, QK_PAD, s), BF16),
            jax.ShapeDtypeStruct((N_HEADS, s, QK_PAD), BF16),
            jax.ShapeDtypeStruct((N_HEADS, s // ATTN_TK, VT_PAD, ATTN_TK), BF16),
        ],
        scratch_shapes=[pltpu.VMEM((d, LAT_WIDTH), BF16)],
        compiler_params=_params(("arbitrary",)),
        name="latent",
    )(x, gpre, w_in_t, gq, gkv, wqt, wk, wvt, cos_rows, sin_rows)


def _gates_kernel(h_ref, w_ref, f32a_ref, f32b_ref, f32c_ref, o_ref, bf16a_ref, bf16b_ref, bf16c_ref, w_sc,
                  *, n_silu_blocks):
    @pl.when(pl.program_id(1) == 0)
    def _():
        w_sc[...] = w_ref[...].astype(BF16)

    bf16a_ref[...] = f32a_ref[...].astype(BF16)
    bf16b_ref[...] = f32b_ref[...].astype(BF16)
    bf16c_ref[...] = f32c_ref[...].astype(BF16)

    is_silu = pl.program_id(0) < n_silu_blocks
    half_rows = h_ref.shape[0] // 2
    for part in range(2):
        rows = slice(part * half_rows, (part + 1) * half_rows)
        r = _dot_nt(h_ref[rows, :], w_sc[...]).astype(BF16)
        sg = _sigmoid(r)
        o_ref[rows, :] = jnp.where(is_silu, r * sg, sg)


def _gates(h, w_in_t, side_casts, tm=1024, tn=1024):
    s, d = h.shape
    n_silu_blocks = MLA_WIDTH // tn
    gap = IN_G_MLA - IN_Z_MLA - MLA_WIDTH
    n_j, n_i = 3 * D_MODEL // tn, s // tm
    cast_rows = next(r for r in range(BF16_SUBLANES, d + 1, BF16_SUBLANES)
                     if d % r == 0 and d // r <= n_j * n_i)
    last_cast_block = d // cast_rows - 1
    cast_spec = pl.BlockSpec((cast_rows, d), lambda j, i: (jnp.minimum(j * n_i + i, last_cast_block), 0))
    return pl.pallas_call(
        functools.partial(_gates_kernel, n_silu_blocks=n_silu_blocks),
        grid=(n_j, n_i),
        in_specs=[pl.BlockSpec((tm, d), lambda j, i: (i, 0)),
                  _wt_rows(tn, d, lambda j: IN_Z_MLA + j * tn + jnp.where(j < n_silu_blocks, 0, gap)),
                  cast_spec, cast_spec, cast_spec],
        out_specs=[pl.BlockSpec((tm, tn), lambda j, i: (i, j)), cast_spec, cast_spec, cast_spec],
        out_shape=[jax.ShapeDtypeStruct((s, 3 * D_MODEL), BF16)]
        + [jax.ShapeDtypeStruct(w.shape, BF16) for w in side_casts],
        scratch_shapes=[pltpu.VMEM((tn, d), BF16)],
        compiler_params=_params(("arbitrary", "arbitrary")),
        name="gates",
    )(h, w_in_t, *side_casts)


def _conv_kernel(h_ref, wci_ref, wbg_ref, wcg_ref, wzc_ref, cw_ref, o_ref,
                 wci_sc, wbg_sc, wcg_sc, wzc_sc, carry_ref):
    i = pl.program_id(1)

    @pl.when(i == 0)
    def _():
        carry_ref[...] = jnp.zeros_like(carry_ref)
        wci_sc[...] = wci_ref[...].astype(BF16)
        wbg_sc[...] = wbg_ref[...].astype(BF16)
        wcg_sc[...] = wcg_ref[...].astype(BF16)
        wzc_sc[...] = wzc_ref[...].astype(BF16)

    h = h_ref[...]
    u = _dot_nt(h, wcg_sc[...]) * _dot_nt(h, wci_sc[...])
    b_gate = _dot_nt(h, wbg_sc[...])
    prev = carry_ref[...]
    row = lax.broadcasted_iota(jnp.int32, u.shape, 0)
    u1 = jnp.where(row == 0, prev[SUBLANES - 1:SUBLANES, :], pltpu.roll(u, 1, axis=0))
    u2 = jnp.where(row == 0, prev[SUBLANES - 2:SUBLANES - 1, :],
                   jnp.where(row == 1, prev[SUBLANES - 1:SUBLANES, :], pltpu.roll(u, 2, axis=0)))
    cw = cw_ref[...]
    conv = cw[0:1, :] * u2 + cw[1:2, :] * u1 + cw[2:3, :] * u
    gated = b_gate * conv
    half_rows = h.shape[0] // 2
    for part in range(2):
        rows = slice(part * half_rows, (part + 1) * half_rows)
        z_conv = _dot_nt(h_ref[rows, :], wzc_sc[...])
        o_ref[rows, :] = (gated[rows, :] * (z_conv * _sigmoid(z_conv))).astype(BF16)
    carry_ref[...] = u[u.shape[0] - SUBLANES:, :]


def _conv(h, w_in_t, conv_w, tm=2048, tn=256):
    s, d = h.shape
    seg_specs = [_wt_rows(tn, d, lambda j, start=start: start + j * tn)
                 for start in (IN_C_IN, IN_B_GATE, IN_C_GATE, IN_Z_CONV)]
    return pl.pallas_call(
        _conv_kernel,
        grid=(CONV_WIDTH // tn, s // tm),
        in_specs=[pl.BlockSpec((tm, d), lambda j, i: (i, 0))] + seg_specs
        + [pl.BlockSpec((CONV_K, tn), lambda j, i: (0, j))],
        out_specs=pl.BlockSpec((tm, tn), lambda j, i: (i, j)),
        out_shape=jax.ShapeDtypeStruct((s, CONV_WIDTH), BF16),
        scratch_shapes=[pltpu.VMEM((tn, d), BF16)] * 4 + [pltpu.VMEM((SUBLANES, tn), F32)],
        compiler_params=_params(("arbitrary", "arbitrary")),
        name="conv",
    )(h, *([w_in_t] * 4), conv_w)


def _attn_kernel(qta_ref, qtb_ref, k_ref, vt_ref, o_ref, s0_ref, s1_ref, m_ref, acc_ref):
    i = pl.program_id(1)
    tk, tq = ATTN_TK, ATTN_TQ
    all_q = slice(0, tq)
    early_q = slice(0, tk)
    late_q = slice(tk, tq)
    heads = range(ATTN_HEADS)

    def qk(qt_ref, hd, j, s_ref, qs):
        k = k_ref[hd, pl.ds(pl.multiple_of(j * tk, tk), tk), :]
        s_ref[hd, :, qs] = _dot(k, qt_ref[hd, :, qs])

    def softmax_pv(slot, hd, j, s_ref, qs, keep=None):
        s = s_ref[hd, :, qs]
        if keep is not None:
            s = jnp.where(keep, s, -jnp.inf)
        m_prev = m_ref[slot, hd, :, qs]
        m_new = jnp.maximum(m_prev, jnp.max(s, axis=0, keepdims=True))
        alpha = jnp.exp2(m_prev - m_new)
        p = jnp.exp2(s - m_new).astype(BF16)
        acc_ref[slot, hd, :, qs] = alpha * acc_ref[slot, hd, :, qs] + _dot(vt_ref[hd, j], p)
        m_ref[slot, hd, :, qs] = m_new

    def first_scores(qt_ref):
        for hd in heads:
            qk(qt_ref, hd, 0, s0_ref, all_q)

    def pair(slot, qt_ref, j0):
        for hd in heads:
            qk(qt_ref, hd, j0 + 1, s1_ref, all_q)
            softmax_pv(slot, hd, j0, s0_ref, all_q)
        for hd in heads:
            qk(qt_ref, hd, j0 + 2, s0_ref, all_q)
            softmax_pv(slot, hd, j0 + 1, s1_ref, all_q)

    def paired_trips(slot, qt_ref, qi):
        def trip(jj, carry):
            pair(slot, qt_ref, 4 * jj)
            pair(slot, qt_ref, 4 * jj + 2)
            return carry

        lax.fori_loop(0, qi // 2, trip, 0)

    def finish(slot, qt_ref, qi, odd, then=None):
        if odd:
            pair(slot, qt_ref, 2 * qi - 2)
        diagonal_blocks(slot, qt_ref, qi)
        if then is not None:
            then()

    def diagonal_blocks(slot, qt_ref, qi):
        keep = (lax.broadcasted_iota(jnp.int32, (tk, tk), 0) <= lax.broadcasted_iota(jnp.int32, (tk, tk), 1))
        for hd in heads:
            qk(qt_ref, hd, 2 * qi + 1, s1_ref, late_q)
            softmax_pv(slot, hd, 2 * qi, s0_ref, early_q, keep)
            softmax_pv(slot, hd, 2 * qi, s0_ref, late_q)
        for hd in heads:
            softmax_pv(slot, hd, 2 * qi + 1, s1_ref, late_q, keep)
        for hd in heads:
            acc = acc_ref[slot, hd]
            out_t = acc[:V_HEAD_DIM, :] / acc[V_HEAD_DIM:V_HEAD_DIM + 1, :]
            o_ref[slot, :, hd * V_HEAD_DIM:(hd + 1) * V_HEAD_DIM] = out_t.T.astype(BF16)

    m_ref[...] = jnp.full_like(m_ref, -jnp.inf)
    acc_ref[...] = jnp.zeros_like(acc_ref)
    n_half = pl.num_programs(1)
    qi_a, qi_b = i, i + n_half
    is_odd = i % 2 == 1
    first_scores(qta_ref)
    paired_trips(0, qta_ref, qi_a)
    for odd in (True, False):
        @pl.when(is_odd == odd)
        def _(odd=odd):
            finish(0, qta_ref, qi_a, odd, then=lambda: first_scores(qtb_ref))
    paired_trips(1, qtb_ref, qi_b)
    for odd in (True, False):
        @pl.when(is_odd == odd)
        def _(odd=odd):
            finish(1, qtb_ref, qi_b, odd)


def _attention(qt, k, vt):
    nh, s, _ = k.shape
    tk, tq, hpb = ATTN_TK, ATTN_TQ, ATTN_HEADS
    n_half = s // tq // 2
    return pl.pallas_call(
        _attn_kernel,
        grid=(nh // hpb, n_half),
        in_specs=[pl.BlockSpec((hpb, QK_PAD, tq), lambda h, i: (h, 0, i)),
                  pl.BlockSpec((hpb, QK_PAD, tq), lambda h, i: (h, 0, i + n_half)),
                  pl.BlockSpec((hpb, s, QK_PAD), lambda h, i: (h, 0, 0)),
                  pl.BlockSpec((hpb, s // tk, VT_PAD, tk), lambda h, i: (h, 0, 0, 0))],
        out_specs=pl.BlockSpec((2, tq, hpb * V_HEAD_DIM), lambda h, i: (0, i, h)),
        out_shape=jax.ShapeDtypeStruct((2, s // 2, nh * V_HEAD_DIM), BF16),
        scratch_shapes=[pltpu.VMEM((hpb, tk, tq), F32), pltpu.VMEM((hpb, tk, tq), F32),
                        pltpu.VMEM((2, hpb, 1, tq), F32), pltpu.VMEM((2, hpb, VT_PAD, tq), F32)],
        compiler_params=_params(("parallel", "arbitrary")),
        name="attention",
    )(qt, qt, k, vt)


def _output_kernel(x_ref, attn_ref, sz_ref, sgm_ref, co_ref, sgc_ref,
                   womla_ref, woconv_ref, wout_ref, g_ref, o_ref):
    a = (attn_ref[...].astype(F32) * sz_ref[...].astype(F32)).astype(BF16)
    y_mla = _dot(a, womla_ref[...])
    y_conv = _dot(co_ref[...], woconv_ref[...])
    merged = sgm_ref[...].astype(F32) * y_mla + sgc_ref[...].astype(F32) * y_conv
    out = _dot(merged.astype(BF16), wout_ref[...])
    o_ref[...] = x_ref[...] + _rms(out, g_ref[...])


def _output(x, attn, gates, co, womla, woconv, wout, g, tm=256):
    s, d = x.shape
    row = lambda c: pl.BlockSpec((tm, d), lambda i: (i, c))
    return pl.pallas_call(
        _output_kernel,
        grid=(s // tm,),
        in_specs=[row(0), row(0), row(0), row(1), row(0), row(2),
                  _resident(womla.shape), _resident(woconv.shape), _resident(wout.shape),
                  _resident(g.shape)],
        out_specs=row(0),
        out_shape=jax.ShapeDtypeStruct((s, d), F32),
        compiler_params=_params(("parallel",)),
        name="output",
    )(x, attn, gates, gates, co, gates, womla, woconv, wout, g)


def _prep_up_weights_kernel(wq_ref, wkv_ref, wqt_ref, wk_ref, wvt_ref):
    wq_t = wq_ref[...].T
    half = ROPE_DIM // 2
    for hd in range(N_HEADS):
        src = hd * (QK_NOPE_DIM + ROPE_DIM)
        dst = hd * QK_PAD
        rope = wq_t[src + QK_NOPE_DIM:src + QK_NOPE_DIM + ROPE_DIM, :].astype(BF16)
        wqt_ref[dst:dst + QK_NOPE_DIM, :] = wq_t[src:src + QK_NOPE_DIM, :].astype(BF16)
        wqt_ref[dst + QK_NOPE_DIM:dst + QK_NOPE_DIM + ROPE_DIM, :] = rope
        wqt_ref[dst + QK_NOPE_DIM + ROPE_DIM:dst + QK_NOPE_DIM + ROPE_DIM + half, :] = rope[half:, :]
        wqt_ref[dst + QK_NOPE_DIM + ROPE_DIM + half:dst + QK_PAD, :] = rope[:half, :]
        kv0 = hd * (QK_NOPE_DIM + V_HEAD_DIM)
        wk_ref[:, hd * QK_NOPE_DIM:(hd + 1) * QK_NOPE_DIM] = wkv_ref[:, kv0:kv0 + QK_NOPE_DIM].astype(BF16)
        wvt_ref[hd * V_HEAD_DIM:(hd + 1) * V_HEAD_DIM, :] = (
            wkv_ref[:, kv0 + QK_NOPE_DIM:kv0 + QK_NOPE_DIM + V_HEAD_DIM].T.astype(BF16))


def _prep_up_weights(w_q_b, w_kv_b):
    return pl.pallas_call(
        _prep_up_weights_kernel,
        out_shape=[jax.ShapeDtypeStruct((N_HEADS * QK_PAD, Q_LORA_RANK), BF16),
                   jax.ShapeDtypeStruct((KV_LORA_RANK, N_HEADS * QK_NOPE_DIM), BF16),
                   jax.ShapeDtypeStruct((N_HEADS * V_HEAD_DIM, KV_LORA_RANK), BF16)],
        compiler_params=pltpu.CompilerParams(vmem_limit_bytes=VMEM_LIMIT),
        name="prep_up_weights",
    )(w_q_b, w_kv_b)


def kernel(x, positions, pre_norm_g, w_in, q_a_norm_g, w_q_b, kv_a_norm_g, w_kv_b, conv_w,
           w_o_mla, w_o_conv, w_out, post_norm_g):
    b, s, d = x.shape
    assert b == 1 and d == D_MODEL
    x2 = x[0]

    w_in_t = w_in.T
    wqt, wk, wvt = _prep_up_weights(w_q_b, w_kv_b)

    inv_freq = ROPE_THETA ** (-jnp.arange(0, ROPE_DIM, 2, dtype=F32) / ROPE_DIM)
    ang = inv_freq[:, None] * positions[0].astype(F32)[None, :]
    cos, sin = jnp.cos(ang), jnp.sin(ang)
    zero_rows = jnp.zeros((LANES - ROPE_DIM, s), F32)
    cos_rows = jnp.concatenate([cos, cos, zero_rows], axis=0)
    sin_rows = jnp.concatenate([-sin, sin, zero_rows], axis=0)

    h, qt, k, vt = _latent(x2, pre_norm_g.reshape(1, d), w_in_t, q_a_norm_g.reshape(1, -1),
                           kv_a_norm_g.reshape(1, -1), wqt, wk, wvt, cos_rows, sin_rows)
    gates, womla, woconv, wout = _gates(h, w_in_t, (w_o_mla, w_o_conv, w_out))
    co = _conv(h, w_in_t, conv_w)
    attn = _attention(qt, k, vt).reshape(s, MLA_WIDTH)
    out = _output(x2, attn, gates, co, womla, woconv, wout, post_norm_g.reshape(1, d))
    return out[None]
```

```python
import functools
import math

import jax
import jax.numpy as jnp
from jax import lax
from jax.experimental import pallas as pl
from jax.experimental.pallas import tpu as pltpu

D_MODEL = 2048
N_HEADS = 16
QK_NOPE_DIM = 128
ROPE_DIM = 64
V_HEAD_DIM = 128
Q_LORA_RANK = 512
KV_LORA_RANK = 512
MLA_WIDTH = N_HEADS * V_HEAD_DIM
CONV_WIDTH = D_MODEL
CONV_K = 3
ROPE_THETA = 10000.0
RMS_EPS = 1e-6

LANES = 128
SUBLANES = 8
QK_PAD = QK_NOPE_DIM + LANES
BF16_SUBLANES = 16
VT_PAD = V_HEAD_DIM + BF16_SUBLANES
ATTN_TK = 512
ATTN_TQ = 2 * ATTN_TK
ATTN_HEADS = 2
HEAD_GROUP = 4
LAT_WIDTH = Q_LORA_RANK + KV_LORA_RANK + LANES
LAT_IN = Q_LORA_RANK + KV_LORA_RANK + ROPE_DIM
IN_Z_MLA = LAT_IN
IN_C_IN = IN_Z_MLA + MLA_WIDTH
IN_B_GATE = IN_C_IN + CONV_WIDTH
IN_C_GATE = IN_B_GATE + CONV_WIDTH
IN_Z_CONV = IN_C_GATE + CONV_WIDTH
IN_G_MLA = IN_Z_CONV + CONV_WIDTH
IN_G_CONV = IN_G_MLA + D_MODEL
VMEM_LIMIT = 56 * 1024 * 1024

BF16 = jnp.bfloat16
F32 = jnp.float32


def _params(semantics):
    return pltpu.CompilerParams(dimension_semantics=semantics, vmem_limit_bytes=VMEM_LIMIT)


def _resident(shape):
    return pl.BlockSpec(shape, lambda *_: (0,) * len(shape), pipeline_mode=pl.Buffered(1))


def _rms(xf, g):
    r = lax.rsqrt(jnp.mean(xf * xf, axis=-1, keepdims=True) + RMS_EPS)
    return xf * r * g


def _sigmoid(x):
    return 0.5 * jnp.tanh(0.5 * x) + 0.5


def _dot(a, b):
    return jnp.dot(a, b, preferred_element_type=F32)


def _dot_nt(a, b):
    return lax.dot_general(a, b, (((1,), (1,)), ((), ())), preferred_element_type=F32)


def _transpose_cast(dst_sc, src_ref, n_rows):
    for c in range(0, n_rows, LANES):
        dst_sc[:, c:c + LANES] = src_ref[c:c + LANES, :].T.astype(BF16)


def _wt_rows(tn, d, first_row):
    return pl.BlockSpec((pl.Element(tn), pl.Element(d)),
                        lambda j, i: (pl.multiple_of(first_row(j), SUBLANES), 0))


def _rope128(g2, cos_t, sin_t):
    return g2 * cos_t + pltpu.roll(g2, ROPE_DIM, axis=1) * sin_t


def _latent_kernel(x_ref, gpre_ref, wlat_ref, gq_ref, gkv_ref, wqt_ref, wk_ref, wvt_ref, cos_ref, sin_ref,
                   h_ref, qt_ref, k_ref, vt_ref, *, scale):
    h = _rms(x_ref[...], gpre_ref[...]).astype(BF16)
    h_ref[...] = h
    lat = _dot(h, wlat_ref[...])
    qa = _rms(lat[:, :Q_LORA_RANK], gq_ref[...]).astype(BF16)
    ckv = _rms(lat[:, Q_LORA_RANK:Q_LORA_RANK + KV_LORA_RANK], gkv_ref[...]).astype(BF16)
    cos_rows = cos_ref[...]
    sin_rows = sin_ref[...]
    k_rope = _rope128(lat[:, Q_LORA_RANK + KV_LORA_RANK:], cos_rows.T, sin_rows.T).astype(BF16)
    tm = qa.shape[0]
    vt = _dot_nt(wvt_ref[...], ckv)
    ones_rows = jnp.ones((VT_PAD - V_HEAD_DIM, tm), BF16)
    for grp in range(N_HEADS // HEAD_GROUP):
        qt = _dot_nt(wqt_ref[grp * HEAD_GROUP * QK_PAD:(grp + 1) * HEAD_GROUP * QK_PAD, :], qa) * scale
        kn = _dot(ckv, wk_ref[:, grp * HEAD_GROUP * QK_NOPE_DIM:(grp + 1) * HEAD_GROUP * QK_NOPE_DIM])
        for sub in range(HEAD_GROUP):
            hd = grp * HEAD_GROUP + sub
            base = sub * QK_PAD
            qt_ref[hd, :QK_NOPE_DIM, :] = qt[base:base + QK_NOPE_DIM, :].astype(BF16)
            g2 = qt[base + QK_NOPE_DIM:base + QK_PAD, :]
            g2_swapped = jnp.concatenate([g2[ROPE_DIM:], g2[:ROPE_DIM]], axis=0)
            qt_ref[hd, QK_NOPE_DIM:, :] = (g2 * cos_rows + g2_swapped * sin_rows).astype(BF16)
            k_ref[hd, :, :QK_NOPE_DIM] = kn[:, sub * QK_NOPE_DIM:(sub + 1) * QK_NOPE_DIM].astype(BF16)
            k_ref[hd, :, QK_NOPE_DIM:] = k_rope
            vt_ref[hd, 0, :V_HEAD_DIM, :] = vt[hd * V_HEAD_DIM:(hd + 1) * V_HEAD_DIM, :].astype(BF16)
            vt_ref[hd, 0, V_HEAD_DIM:, :] = ones_rows


def _latent(x, gpre, wlat, gq, gkv, wqt, wk, wvt, cos_rows, sin_rows, tm=512):
    s, d = x.shape
    per_blk = ATTN_TK // tm
    scale = math.log2(math.e) / math.sqrt(QK_NOPE_DIM + ROPE_DIM)
    return pl.pallas_call(
        functools.partial(_latent_kernel, scale=scale),
        grid=(s // tm,),
        in_specs=[
            pl.BlockSpec((tm, d), lambda i: (i, 0)), _resident(gpre.shape),
            _resident(wlat.shape), _resident(gq.shape), _resident(gkv.shape),
            _resident(wqt.shape), _resident(wk.shape), _resident(wvt.shape),
            pl.BlockSpec((LANES, tm), lambda i: (0, i)),
            pl.BlockSpec((LANES, tm), lambda i: (0, i)),
        ],
        out_specs=[
            pl.BlockSpec((tm, d), lambda i: (i, 0)),
            pl.BlockSpec((N_HEADS, QK_PAD, tm), lambda i: (0, 0, i)),
            pl.BlockSpec((N_HEADS, tm, QK_PAD), lambda i: (0, i, 0)),
            pl.BlockSpec((N_HEADS, 1, VT_PAD, tm), lambda i: (0, i // per_blk, 0, i % per_blk)),
        ],
        out_shape=[
            jax.ShapeDtypeStruct((s, d), BF16),
            jax.ShapeDtypeStruct((N_HEADS, QK_PAD, s), BF16),
            jax.ShapeDtypeStruct((N_HEADS, s, QK_PAD), BF16),
            jax.ShapeDtypeStruct((N_HEADS, s // ATTN_TK, VT_PAD, ATTN_TK), BF16),
        ],
        compiler_params=_params(("parallel",)),
        name="latent",
    )(x, gpre, wlat, gq, gkv, wqt, wk, wvt, cos_rows, sin_rows)


def _gates_kernel(h_ref, w_ref, f32a_ref, f32b_ref, f32c_ref, o_ref, bf16a_ref, bf16b_ref, bf16c_ref, w_sc,
                  *, n_silu_blocks):
    @pl.when(pl.program_id(1) == 0)
    def _():
        w_sc[...] = w_ref[...].astype(BF16)

    bf16a_ref[...] = f32a_ref[...].astype(BF16)
    bf16b_ref[...] = f32b_ref[...].astype(BF16)
    bf16c_ref[...] = f32c_ref[...].astype(BF16)

    is_silu = pl.program_id(0) < n_silu_blocks
    half_rows = h_ref.shape[0] // 2
    for part in range(2):
        rows = slice(part * half_rows, (part + 1) * half_rows)
        r = _dot_nt(h_ref[rows, :], w_sc[...]).astype(BF16)
        sg = _sigmoid(r)
        o_ref[rows, :] = jnp.where(is_silu, r * sg, sg)


def _gates(h, w_in_t, side_casts, tm=1024, tn=1024):
    s, d = h.shape
    n_silu_blocks = MLA_WIDTH // tn
    gap = IN_G_MLA - IN_Z_MLA - MLA_WIDTH
    n_j, n_i = 3 * D_MODEL // tn, s // tm
    cast_rows = next(r for r in range(BF16_SUBLANES, d + 1, BF16_SUBLANES)
                     if d % r == 0 and d // r <= n_j * n_i)
    last_cast_block = d // cast_rows - 1
    cast_spec = pl.BlockSpec((cast_rows, d), lambda j, i: (jnp.minimum(j * n_i + i, last_cast_block), 0))
    return pl.pallas_call(
        functools.partial(_gates_kernel, n_silu_blocks=n_silu_blocks),
        grid=(n_j, n_i),
        in_specs=[pl.BlockSpec((tm, d), lambda j, i: (i, 0)),
                  _wt_rows(tn, d, lambda j: IN_Z_MLA + j * tn + jnp.where(j < n_silu_blocks, 0, gap)),
                  cast_spec, cast_spec, cast_spec],
        out_specs=[pl.BlockSpec((tm, tn), lambda j, i: (i, j)), cast_spec, cast_spec, cast_spec],
        out_shape=[jax.ShapeDtypeStruct((s, 3 * D_MODEL), BF16)]
        + [jax.ShapeDtypeStruct(w.shape, BF16) for w in side_casts],
        scratch_shapes=[pltpu.VMEM((tn, d), BF16)],
        compiler_params=_params(("arbitrary", "arbitrary")),
        name="gates",
    )(h, w_in_t, *side_casts)


def _conv_kernel(h_ref, wci_ref, wbg_ref, wcg_ref, wzc_ref, cw_ref, o_ref,
                 wci_sc, wbg_sc, wcg_sc, wzc_sc, carry_ref):
    i = pl.program_id(1)

    @pl.when(i == 0)
    def _():
        carry_ref[...] = jnp.zeros_like(carry_ref)
        wci_sc[...] = wci_ref[...].astype(BF16)
        wbg_sc[...] = wbg_ref[...].astype(BF16)
        wcg_sc[...] = wcg_ref[...].astype(BF16)
        wzc_sc[...] = wzc_ref[...].astype(BF16)

    h = h_ref[...]
    u = _dot_nt(h, wcg_sc[...]) * _dot_nt(h, wci_sc[...])
    b_gate = _dot_nt(h, wbg_sc[...])
    prev = carry_ref[...]
    row = lax.broadcasted_iota(jnp.int32, u.shape, 0)
    u1 = jnp.where(row == 0, prev[SUBLANES - 1:SUBLANES, :], pltpu.roll(u, 1, axis=0))
    u2 = jnp.where(row == 0, prev[SUBLANES - 2:SUBLANES - 1, :],
                   jnp.where(row == 1, prev[SUBLANES - 1:SUBLANES, :], pltpu.roll(u, 2, axis=0)))
    cw = cw_ref[...]
    conv = cw[0:1, :] * u2 + cw[1:2, :] * u1 + cw[2:3, :] * u
    gated = b_gate * conv
    half_rows = h.shape[0] // 2
    for part in range(2):
        rows = slice(part * half_rows, (part + 1) * half_rows)
        z_conv = _dot_nt(h_ref[rows, :], wzc_sc[...])
        o_ref[rows, :] = (gated[rows, :] * (z_conv * _sigmoid(z_conv))).astype(BF16)
    carry_ref[...] = u[u.shape[0] - SUBLANES:, :]


def _conv(h, w_in_t, conv_w, tm=2048, tn=256):
    s, d = h.shape
    seg_specs = [_wt_rows(tn, d, lambda j, start=start: start + j * tn)
                 for start in (IN_C_IN, IN_B_GATE, IN_C_GATE, IN_Z_CONV)]
    return pl.pallas_call(
        _conv_kernel,
        grid=(CONV_WIDTH // tn, s // tm),
        in_specs=[pl.BlockSpec((tm, d), lambda j, i: (i, 0))] + seg_specs
        + [pl.BlockSpec((CONV_K, tn), lambda j, i: (0, j))],
        out_specs=pl.BlockSpec((tm, tn), lambda j, i: (i, j)),
        out_shape=jax.ShapeDtypeStruct((s, CONV_WIDTH), BF16),
        scratch_shapes=[pltpu.VMEM((tn, d), BF16)] * 4 + [pltpu.VMEM((SUBLANES, tn), F32)],
        compiler_params=_params(("arbitrary", "arbitrary")),
        name="conv",
    )(h, *([w_in_t] * 4), conv_w)


def _attn_kernel(qta_ref, qtb_ref, k_ref, vt_ref, o_ref, s0_ref, s1_ref, m_ref, acc_ref):
    i = pl.program_id(1)
    tk, tq = ATTN_TK, ATTN_TQ
    all_q = slice(0, tq)
    early_q = slice(0, tk)
    late_q = slice(tk, tq)
    heads = range(ATTN_HEADS)

    def qk(qt_ref, hd, j, s_ref, qs):
        k = k_ref[hd, pl.ds(pl.multiple_of(j * tk, tk), tk), :]
        s_ref[hd, :, qs] = _dot(k, qt_ref[hd, :, qs])

    def softmax_pv(slot, hd, j, s_ref, qs, keep=None):
        s = s_ref[hd, :, qs]
        if keep is not None:
            s = jnp.where(keep, s, -jnp.inf)
        m_prev = m_ref[slot, hd, :, qs]
        m_new = jnp.maximum(m_prev, jnp.max(s, axis=0, keepdims=True))
        alpha = jnp.exp2(m_prev - m_new)
        p = jnp.exp2(s - m_new).astype(BF16)
        acc_ref[slot, hd, :, qs] = alpha * acc_ref[slot, hd, :, qs] + _dot(vt_ref[hd, j], p)
        m_ref[slot, hd, :, qs] = m_new

    def first_scores(qt_ref):
        for hd in heads:
            qk(qt_ref, hd, 0, s0_ref, all_q)

    def pair(slot, qt_ref, j0):
        for hd in heads:
            qk(qt_ref, hd, j0 + 1, s1_ref, all_q)
            softmax_pv(slot, hd, j0, s0_ref, all_q)
        for hd in heads:
            qk(qt_ref, hd, j0 + 2, s0_ref, all_q)
            softmax_pv(slot, hd, j0 + 1, s1_ref, all_q)

    def paired_trips(slot, qt_ref, qi):
        def trip(jj, carry):
            pair(slot, qt_ref, 4 * jj)
            pair(slot, qt_ref, 4 * jj + 2)
            return carry

        lax.fori_loop(0, qi // 2, trip, 0)

    def finish(slot, qt_ref, qi, odd, then=None):
        if odd:
            pair(slot, qt_ref, 2 * qi - 2)
        diagonal_blocks(slot, qt_ref, qi)
        if then is not None:
            then()

    def diagonal_blocks(slot, qt_ref, qi):
        keep = (lax.broadcasted_iota(jnp.int32, (tk, tk), 0) <= lax.broadcasted_iota(jnp.int32, (tk, tk), 1))
        for hd in heads:
            qk(qt_ref, hd, 2 * qi + 1, s1_ref, late_q)
            softmax_pv(slot, hd, 2 * qi, s0_ref, early_q, keep)
            softmax_pv(slot, hd, 2 * qi, s0_ref, late_q)
        for hd in heads:
            softmax_pv(slot, hd, 2 * qi + 1, s1_ref, late_q, keep)
        for hd in heads:
            acc = acc_ref[slot, hd]
            out_t = acc[:V_HEAD_DIM, :] / acc[V_HEAD_DIM:V_HEAD_DIM + 1, :]
            o_ref[slot, :, hd * V_HEAD_DIM:(hd + 1) * V_HEAD_DIM] = out_t.T.astype(BF16)

    m_ref[...] = jnp.full_like(m_ref, -jnp.inf)
    acc_ref[...] = jnp.zeros_like(acc_ref)
    n_half = pl.num_programs(1)
    qi_a, qi_b = i, i + n_half
    is_odd = i % 2 == 1
    first_scores(qta_ref)
    paired_trips(0, qta_ref, qi_a)
    for odd in (True, False):
        @pl.when(is_odd == odd)
        def _(odd=odd):
            finish(0, qta_ref, qi_a, odd, then=lambda: first_scores(qtb_ref))
    paired_trips(1, qtb_ref, qi_b)
    for odd in (True, False):
        @pl.when(is_odd == odd)
        def _(odd=odd):
            finish(1, qtb_ref, qi_b, odd)


def _attention(qt, k, vt):
    nh, s, _ = k.shape
    tk, tq, hpb = ATTN_TK, ATTN_TQ, ATTN_HEADS
    n_half = s // tq // 2
    return pl.pallas_call(
        _attn_kernel,
        grid=(nh // hpb, n_half),
        in_specs=[pl.BlockSpec((hpb, QK_PAD, tq), lambda h, i: (h, 0, i)),
                  pl.BlockSpec((hpb, QK_PAD, tq), lambda h, i: (h, 0, i + n_half)),
                  pl.BlockSpec((hpb, s, QK_PAD), lambda h, i: (h, 0, 0)),
                  pl.BlockSpec((hpb, s // tk, VT_PAD, tk), lambda h, i: (h, 0, 0, 0))],
        out_specs=pl.BlockSpec((2, tq, hpb * V_HEAD_DIM), lambda h, i: (0, i, h)),
        out_shape=jax.ShapeDtypeStruct((2, s // 2, nh * V_HEAD_DIM), BF16),
        scratch_shapes=[pltpu.VMEM((hpb, tk, tq), F32), pltpu.VMEM((hpb, tk, tq), F32),
                        pltpu.VMEM((2, hpb, 1, tq), F32), pltpu.VMEM((2, hpb, VT_PAD, tq), F32)],
        compiler_params=_params(("parallel", "arbitrary")),
        name="attention",
    )(qt, qt, k, vt)


def _output_kernel(x_ref, attn_ref, sz_ref, sgm_ref, co_ref, sgc_ref,
                   womla_ref, woconv_ref, wout_ref, g_ref, o_ref):
    a = (attn_ref[...].astype(F32) * sz_ref[...].astype(F32)).astype(BF16)
    y_mla = _dot(a, womla_ref[...])
    y_conv = _dot(co_ref[...], woconv_ref[...])
    merged = sgm_ref[...].astype(F32) * y_mla + sgc_ref[...].astype(F32) * y_conv
    out = _dot(merged.astype(BF16), wout_ref[...])
    o_ref[...] = x_ref[...] + _rms(out, g_ref[...])


def _output(x, attn, gates, co, womla, woconv, wout, g, tm=256):
    s, d = x.shape
    row = lambda c: pl.BlockSpec((tm, d), lambda i: (i, c))
    return pl.pallas_call(
        _output_kernel,
        grid=(s // tm,),
        in_specs=[row(0), row(0), row(0), row(1), row(0), row(2),
                  _resident(womla.shape), _resident(woconv.shape), _resident(wout.shape),
                  _resident(g.shape)],
        out_specs=row(0),
        out_shape=jax.ShapeDtypeStruct((s, d), F32),
        compiler_params=_params(("parallel",)),
        name="output",
    )(x, attn, gates, gates, co, gates, womla, woconv, wout, g)


def _prep_up_weights_kernel(wq_ref, wkv_ref, win_ref, wqt_ref, wk_ref, wvt_ref, wlat_ref):
    half = ROPE_DIM // 2
    _transpose_cast(wlat_ref, win_ref, LAT_IN - ROPE_DIM)
    rope_rows = jnp.concatenate([win_ref[LAT_IN - ROPE_DIM:LAT_IN, :], win_ref[LAT_IN - half:LAT_IN, :],
                                 win_ref[LAT_IN - ROPE_DIM:LAT_IN - half, :]], axis=0)
    wlat_ref[:, LAT_IN - ROPE_DIM:] = rope_rows.T.astype(BF16)

    wq_t = wq_ref[...].T
    for hd in range(N_HEADS):
        src = hd * (QK_NOPE_DIM + ROPE_DIM)
        dst = hd * QK_PAD
        rope = wq_t[src + QK_NOPE_DIM:src + QK_NOPE_DIM + ROPE_DIM, :].astype(BF16)
        wqt_ref[dst:dst + QK_NOPE_DIM, :] = wq_t[src:src + QK_NOPE_DIM, :].astype(BF16)
        wqt_ref[dst + QK_NOPE_DIM:dst + QK_NOPE_DIM + ROPE_DIM, :] = rope
        wqt_ref[dst + QK_NOPE_DIM + ROPE_DIM:dst + QK_NOPE_DIM + ROPE_DIM + half, :] = rope[half:, :]
        wqt_ref[dst + QK_NOPE_DIM + ROPE_DIM + half:dst + QK_PAD, :] = rope[:half, :]
        kv0 = hd * (QK_NOPE_DIM + V_HEAD_DIM)
        wk_ref[:, hd * QK_NOPE_DIM:(hd + 1) * QK_NOPE_DIM] = wkv_ref[:, kv0:kv0 + QK_NOPE_DIM].astype(BF16)
        wvt_ref[hd * V_HEAD_DIM:(hd + 1) * V_HEAD_DIM, :] = (
            wkv_ref[:, kv0 + QK_NOPE_DIM:kv0 + QK_NOPE_DIM + V_HEAD_DIM].T.astype(BF16))


def _prep_up_weights(w_q_b, w_kv_b, w_in_t):
    d = w_in_t.shape[1]
    whole = lambda shape: pl.BlockSpec(shape, lambda i: (0,) * len(shape))
    out_shapes = [(N_HEADS * QK_PAD, Q_LORA_RANK), (KV_LORA_RANK, N_HEADS * QK_NOPE_DIM),
                  (N_HEADS * V_HEAD_DIM, KV_LORA_RANK), (d, LAT_WIDTH)]
    return pl.pallas_call(
        _prep_up_weights_kernel,
        grid=(1,),
        in_specs=[whole(w_q_b.shape), whole(w_kv_b.shape),
                  pl.BlockSpec((pl.Element(LAT_IN), pl.Element(d)), lambda i: (0, 0))],
        out_specs=[whole(shape) for shape in out_shapes],
        out_shape=[jax.ShapeDtypeStruct(shape, BF16) for shape in out_shapes],
        compiler_params=_params(("arbitrary",)),
        name="prep_up_weights",
    )(w_q_b, w_kv_b, w_in_t)


def kernel(x, positions, pre_norm_g, w_in, q_a_norm_g, w_q_b, kv_a_norm_g, w_kv_b, conv_w,
           w_o_mla, w_o_conv, w_out, post_norm_g):
    b, s, d = x.shape
    assert b == 1 and d == D_MODEL
    x2 = x[0]

    w_in_t = w_in.T
    wqt, wk, wvt, wlat = _prep_up_weights(w_q_b, w_kv_b, w_in_t)

    inv_freq = ROPE_THETA ** (-jnp.arange(0, ROPE_DIM, 2, dtype=F32) / ROPE_DIM)
    ang = inv_freq[:, None] * positions[0].astype(F32)[None, :]
    cos, sin = jnp.cos(ang), jnp.sin(ang)
    zero_rows = jnp.zeros((LANES - ROPE_DIM, s), F32)
    cos_rows = jnp.concatenate([cos, cos, zero_rows], axis=0)
    sin_rows = jnp.concatenate([-sin, sin, zero_rows], axis=0)

    h, qt, k, vt = _latent(x2, pre_norm_g.reshape(1, d), wlat, q_a_norm_g.reshape(1, -1),
                           kv_a_norm_g.reshape(1, -1), wqt, wk, wvt, cos_rows, sin_rows)
    gates, womla, woconv, wout = _gates(h, w_in_t, (w_o_mla, w_o_conv, w_out))
    co = _conv(h, w_in_t, conv_w)
    attn = _attention(qt, k, vt).reshape(s, MLA_WIDTH)
    out = _output(x2, attn, gates, co, womla, woconv, wout, post_norm_g.reshape(1, d))
    return out[None]
```

```python
import functools
import math

import jax
import jax.numpy as jnp
from jax import lax
from jax.experimental import pallas as pl
from jax.experimental.pallas import tpu as pltpu

D_MODEL = 2048
N_HEADS = 16
QK_NOPE_DIM = 128
ROPE_DIM = 64
V_HEAD_DIM = 128
Q_LORA_RANK = 512
KV_LORA_RANK = 512
MLA_WIDTH = N_HEADS * V_HEAD_DIM
CONV_WIDTH = D_MODEL
CONV_K = 3
ROPE_THETA = 10000.0
RMS_EPS = 1e-6

LANES = 128
SUBLANES = 8
QK_PAD = QK_NOPE_DIM + LANES
BF16_SUBLANES = 16
VT_PAD = V_HEAD_DIM + BF16_SUBLANES
ATTN_TK = 512
ATTN_TQ = 2 * ATTN_TK
ATTN_HEADS = 2
HEAD_GROUP = 4
LAT_WIDTH = Q_LORA_RANK + KV_LORA_RANK + LANES
LAT_IN = Q_LORA_RANK + KV_LORA_RANK + ROPE_DIM
IN_Z_MLA = LAT_IN
IN_C_IN = IN_Z_MLA + MLA_WIDTH
IN_B_GATE = IN_C_IN + CONV_WIDTH
IN_C_GATE = IN_B_GATE + CONV_WIDTH
IN_Z_CONV = IN_C_GATE + CONV_WIDTH
IN_G_MLA = IN_Z_CONV + CONV_WIDTH
IN_G_CONV = IN_G_MLA + D_MODEL
VMEM_LIMIT = 56 * 1024 * 1024

BF16 = jnp.bfloat16
F32 = jnp.float32


def _params(semantics):
    return pltpu.CompilerParams(dimension_semantics=semantics, vmem_limit_bytes=VMEM_LIMIT)


def _resident(shape):
    return pl.BlockSpec(shape, lambda *_: (0,) * len(shape), pipeline_mode=pl.Buffered(1))


def _rms(xf, g):
    r = lax.rsqrt(jnp.mean(xf * xf, axis=-1, keepdims=True) + RMS_EPS)
    return xf * r * g


def _sigmoid(x):
    return 0.5 * jnp.tanh(0.5 * x) + 0.5


def _dot(a, b):
    return jnp.dot(a, b, preferred_element_type=F32)


def _dot_nt(a, b):
    return lax.dot_general(a, b, (((1,), (1,)), ((), ())), preferred_element_type=F32)


def _transpose_cast(dst_ref, src_ref, n_rows):
    for c in range(0, n_rows, LANES):
        dst_ref[:, c:c + LANES] = src_ref[c:c + LANES, :].T.astype(BF16)


def _wt_rows(tn, d, first_row):
    return pl.BlockSpec((pl.Element(tn), pl.Element(d)),
                        lambda j, i: (pl.multiple_of(first_row(j), SUBLANES), 0))


def _rope128(g2, cos_t, sin_t):
    return g2 * cos_t + pltpu.roll(g2, ROPE_DIM, axis=1) * sin_t


def _latent_kernel(x_ref, gpre_ref, wlat_ref, gq_ref, gkv_ref, wqt_ref, wk_ref, wvt_ref, cos_ref, sin_ref,
                   h_ref, qt_ref, k_ref, vt_ref, *, scale):
    h = _rms(x_ref[...], gpre_ref[...]).astype(BF16)
    h_ref[...] = h
    lat = _dot(h, wlat_ref[...])
    qa = _rms(lat[:, :Q_LORA_RANK], gq_ref[...]).astype(BF16)
    ckv = _rms(lat[:, Q_LORA_RANK:Q_LORA_RANK + KV_LORA_RANK], gkv_ref[...]).astype(BF16)
    cos_rows = cos_ref[...]
    sin_rows = sin_ref[...]
    k_rope = _rope128(lat[:, Q_LORA_RANK + KV_LORA_RANK:], cos_rows.T, sin_rows.T).astype(BF16)
    tm = qa.shape[0]
    vt = _dot_nt(wvt_ref[...], ckv)
    ones_rows = jnp.ones((VT_PAD - V_HEAD_DIM, tm), BF16)
    for grp in range(N_HEADS // HEAD_GROUP):
        qt = _dot_nt(wqt_ref[grp * HEAD_GROUP * QK_PAD:(grp + 1) * HEAD_GROUP * QK_PAD, :], qa) * scale
        kn = _dot(ckv, wk_ref[:, grp * HEAD_GROUP * QK_NOPE_DIM:(grp + 1) * HEAD_GROUP * QK_NOPE_DIM])
        for sub in range(HEAD_GROUP):
            hd = grp * HEAD_GROUP + sub
            base = sub * QK_PAD
            qt_ref[hd, :QK_NOPE_DIM, :] = qt[base:base + QK_NOPE_DIM, :].astype(BF16)
            g2 = qt[base + QK_NOPE_DIM:base + QK_PAD, :]
            g2_swapped = jnp.concatenate([g2[ROPE_DIM:], g2[:ROPE_DIM]], axis=0)
            qt_ref[hd, QK_NOPE_DIM:, :] = (g2 * cos_rows + g2_swapped * sin_rows).astype(BF16)
            k_ref[hd, :, :QK_NOPE_DIM] = kn[:, sub * QK_NOPE_DIM:(sub + 1) * QK_NOPE_DIM].astype(BF16)
            k_ref[hd, :, QK_NOPE_DIM:] = k_rope
            vt_ref[hd, 0, :V_HEAD_DIM, :] = vt[hd * V_HEAD_DIM:(hd + 1) * V_HEAD_DIM, :].astype(BF16)
            vt_ref[hd, 0, V_HEAD_DIM:, :] = ones_rows


def _latent(x, gpre, wlat, gq, gkv, wqt, wk, wvt, cos_rows, sin_rows, tm=512):
    s, d = x.shape
    per_blk = ATTN_TK // tm
    scale = math.log2(math.e) / math.sqrt(QK_NOPE_DIM + ROPE_DIM)
    return pl.pallas_call(
        functools.partial(_latent_kernel, scale=scale),
        grid=(s // tm,),
        in_specs=[
            pl.BlockSpec((tm, d), lambda i: (i, 0)), _resident(gpre.shape),
            _resident(wlat.shape), _resident(gq.shape), _resident(gkv.shape),
            _resident(wqt.shape), _resident(wk.shape), _resident(wvt.shape),
            pl.BlockSpec((LANES, tm), lambda i: (0, i)),
            pl.BlockSpec((LANES, tm), lambda i: (0, i)),
        ],
        out_specs=[
            pl.BlockSpec((tm, d), lambda i: (i, 0)),
            pl.BlockSpec((N_HEADS, QK_PAD, tm), lambda i: (0, 0, i)),
            pl.BlockSpec((N_HEADS, tm, QK_PAD), lambda i: (0, i, 0)),
            pl.BlockSpec((N_HEADS, 1, VT_PAD, tm), lambda i: (0, i // per_blk, 0, i % per_blk)),
        ],
        out_shape=[
            jax.ShapeDtypeStruct((s, d), BF16),
            jax.ShapeDtypeStruct((N_HEADS, QK_PAD, s), BF16),
            jax.ShapeDtypeStruct((N_HEADS, s, QK_PAD), BF16),
            jax.ShapeDtypeStruct((N_HEADS, s // ATTN_TK, VT_PAD, ATTN_TK), BF16),
        ],
        compiler_params=_params(("parallel",)),
        name="latent",
    )(x, gpre, wlat, gq, gkv, wqt, wk, wvt, cos_rows, sin_rows)


def _gates_kernel(h_ref, w_ref, f32a_ref, f32b_ref, f32c_ref, o_ref, bf16a_ref, bf16b_ref, bf16c_ref, w_sc,
                  *, n_silu_blocks):
    @pl.when(pl.program_id(1) == 0)
    def _():
        w_sc[...] = w_ref[...].astype(BF16)

    bf16a_ref[...] = f32a_ref[...].astype(BF16)
    bf16b_ref[...] = f32b_ref[...].astype(BF16)
    bf16c_ref[...] = f32c_ref[...].astype(BF16)

    is_silu = pl.program_id(0) < n_silu_blocks
    half_rows = h_ref.shape[0] // 2
    for part in range(2):
        rows = slice(part * half_rows, (part + 1) * half_rows)
        r = _dot_nt(h_ref[rows, :], w_sc[...]).astype(BF16)
        sg = _sigmoid(r)
        o_ref[rows, :] = jnp.where(is_silu, r * sg, sg)


def _gates(h, w_in_t, side_casts, tm=1024, tn=1024):
    s, d = h.shape
    n_silu_blocks = MLA_WIDTH // tn
    gap = IN_G_MLA - IN_Z_MLA - MLA_WIDTH
    n_j, n_i = 3 * D_MODEL // tn, s // tm
    cast_rows = next(r for r in range(BF16_SUBLANES, d + 1, BF16_SUBLANES)
                     if d % r == 0 and d // r <= n_j * n_i)
    last_cast_block = d // cast_rows - 1
    cast_spec = pl.BlockSpec((cast_rows, d), lambda j, i: (jnp.minimum(j * n_i + i, last_cast_block), 0))
    return pl.pallas_call(
        functools.partial(_gates_kernel, n_silu_blocks=n_silu_blocks),
        grid=(n_j, n_i),
        in_specs=[pl.BlockSpec((tm, d), lambda j, i: (i, 0)),
                  _wt_rows(tn, d, lambda j: IN_Z_MLA + j * tn + jnp.where(j < n_silu_blocks, 0, gap)),
                  cast_spec, cast_spec, cast_spec],
        out_specs=[pl.BlockSpec((tm, tn), lambda j, i: (i, j)), cast_spec, cast_spec, cast_spec],
        out_shape=[jax.ShapeDtypeStruct((s, 3 * D_MODEL), BF16)]
        + [jax.ShapeDtypeStruct(w.shape, BF16) for w in side_casts],
        scratch_shapes=[pltpu.VMEM((tn, d), BF16)],
        compiler_params=_params(("arbitrary", "arbitrary")),
        name="gates",
    )(h, w_in_t, *side_casts)


def _conv_kernel(h_ref, wci_ref, wbg_ref, wcg_ref, wzc_ref, cw_ref, o_ref,
                 wci_sc, wbg_sc, wcg_sc, wzc_sc, carry_ref):
    i = pl.program_id(1)

    @pl.when(i == 0)
    def _():
        carry_ref[...] = jnp.zeros_like(carry_ref)
        wci_sc[...] = wci_ref[...].astype(BF16)
        wbg_sc[...] = wbg_ref[...].astype(BF16)
        wcg_sc[...] = wcg_ref[...].astype(BF16)
        wzc_sc[...] = wzc_ref[...].astype(BF16)

    h = h_ref[...]
    u = _dot_nt(h, wcg_sc[...]) * _dot_nt(h, wci_sc[...])
    b_gate = _dot_nt(h, wbg_sc[...])
    prev = carry_ref[...]
    row = lax.broadcasted_iota(jnp.int32, u.shape, 0)
    u1 = jnp.where(row == 0, prev[SUBLANES - 1:SUBLANES, :], pltpu.roll(u, 1, axis=0))
    u2 = jnp.where(row == 0, prev[SUBLANES - 2:SUBLANES - 1, :],
                   jnp.where(row == 1, prev[SUBLANES - 1:SUBLANES, :], pltpu.roll(u, 2, axis=0)))
    cw = cw_ref[...]
    conv = cw[0:1, :] * u2 + cw[1:2, :] * u1 + cw[2:3, :] * u
    gated = b_gate * conv
    half_rows = h.shape[0] // 2
    for part in range(2):
        rows = slice(part * half_rows, (part + 1) * half_rows)
        z_conv = _dot_nt(h_ref[rows, :], wzc_sc[...])
        o_ref[rows, :] = (gated[rows, :] * (z_conv * _sigmoid(z_conv))).astype(BF16)
    carry_ref[...] = u[u.shape[0] - SUBLANES:, :]


def _conv(h, w_in_t, conv_w, tm=2048, tn=256):
    s, d = h.shape
    seg_specs = [_wt_rows(tn, d, lambda j, start=start: start + j * tn)
                 for start in (IN_C_IN, IN_B_GATE, IN_C_GATE, IN_Z_CONV)]
    return pl.pallas_call(
        _conv_kernel,
        grid=(CONV_WIDTH // tn, s // tm),
        in_specs=[pl.BlockSpec((tm, d), lambda j, i: (i, 0))] + seg_specs
        + [pl.BlockSpec((CONV_K, tn), lambda j, i: (0, j))],
        out_specs=pl.BlockSpec((tm, tn), lambda j, i: (i, j)),
        out_shape=jax.ShapeDtypeStruct((s, CONV_WIDTH), BF16),
        scratch_shapes=[pltpu.VMEM((tn, d), BF16)] * 4 + [pltpu.VMEM((SUBLANES, tn), F32)],
        compiler_params=_params(("arbitrary", "arbitrary")),
        name="conv",
    )(h, *([w_in_t] * 4), conv_w)


def _attn_kernel(qta_ref, qtb_ref, k_ref, vt_ref, o_ref, s0_ref, s1_ref, m_ref, acc_ref):
    i = pl.program_id(1)
    tk, tq = ATTN_TK, ATTN_TQ
    all_q = slice(0, tq)
    early_q = slice(0, tk)
    late_q = slice(tk, tq)
    heads = range(ATTN_HEADS)

    def qk(qt_ref, hd, j, s_ref, qs):
        k = k_ref[hd, pl.ds(pl.multiple_of(j * tk, tk), tk), :]
        s_ref[hd, :, qs] = _dot(k, qt_ref[hd, :, qs])

    def softmax_pv(slot, hd, j, s_ref, qs, keep=None):
        s = s_ref[hd, :, qs]
        if keep is not None:
            s = jnp.where(keep, s, -jnp.inf)
        m_prev = m_ref[slot, hd, :, qs]
        m_new = jnp.maximum(m_prev, jnp.max(s, axis=0, keepdims=True))
        alpha = jnp.exp2(m_prev - m_new)
        p = jnp.exp2(s - m_new).astype(BF16)
        acc_ref[slot, hd, :, qs] = alpha * acc_ref[slot, hd, :, qs] + _dot(vt_ref[hd, j], p)
        m_ref[slot, hd, :, qs] = m_new

    def first_scores(qt_ref):
        for hd in heads:
            qk(qt_ref, hd, 0, s0_ref, all_q)

    def pair(slot, qt_ref, j0):
        for hd in heads:
            qk(qt_ref, hd, j0 + 1, s1_ref, all_q)
            softmax_pv(slot, hd, j0, s0_ref, all_q)
        for hd in heads:
            qk(qt_ref, hd, j0 + 2, s0_ref, all_q)
            softmax_pv(slot, hd, j0 + 1, s1_ref, all_q)

    def paired_trips(slot, qt_ref, qi):
        def trip(jj, carry):
            pair(slot, qt_ref, 4 * jj)
            pair(slot, qt_ref, 4 * jj + 2)
            return carry

        lax.fori_loop(0, qi // 2, trip, 0)

    def finish(slot, qt_ref, qi, odd, then=None):
        if odd:
            pair(slot, qt_ref, 2 * qi - 2)
        diagonal_blocks(slot, qt_ref, qi)
        if then is not None:
            then()

    def diagonal_blocks(slot, qt_ref, qi):
        keep = (lax.broadcasted_iota(jnp.int32, (tk, tk), 0) <= lax.broadcasted_iota(jnp.int32, (tk, tk), 1))
        for hd in heads:
            qk(qt_ref, hd, 2 * qi + 1, s1_ref, late_q)
            softmax_pv(slot, hd, 2 * qi, s0_ref, early_q, keep)
            softmax_pv(slot, hd, 2 * qi, s0_ref, late_q)
        for hd in heads:
            softmax_pv(slot, hd, 2 * qi + 1, s1_ref, late_q, keep)
        for hd in heads:
            acc = acc_ref[slot, hd]
            out_t = acc[:V_HEAD_DIM, :] / acc[V_HEAD_DIM:V_HEAD_DIM + 1, :]
            o_ref[slot, :, hd * V_HEAD_DIM:(hd + 1) * V_HEAD_DIM] = out_t.T.astype(BF16)

    m_ref[...] = jnp.full_like(m_ref, -jnp.inf)
    acc_ref[...] = jnp.zeros_like(acc_ref)
    n_half = pl.num_programs(1)
    qi_a, qi_b = i, i + n_half
    is_odd = i % 2 == 1
    first_scores(qta_ref)
    paired_trips(0, qta_ref, qi_a)
    for odd in (True, False):
        @pl.when(is_odd == odd)
        def _(odd=odd):
            finish(0, qta_ref, qi_a, odd, then=lambda: first_scores(qtb_ref))
    paired_trips(1, qtb_ref, qi_b)
    for odd in (True, False):
        @pl.when(is_odd == odd)
        def _(odd=odd):
            finish(1, qtb_ref, qi_b, odd)


def _attention(qt, k, vt):
    nh, s, _ = k.shape
    tk, tq, hpb = ATTN_TK, ATTN_TQ, ATTN_HEADS
    n_half = s // tq // 2
    return pl.pallas_call(
        _attn_kernel,
        grid=(nh // hpb, n_half),
        in_specs=[pl.BlockSpec((hpb, QK_PAD, tq), lambda h, i: (h, 0, i)),
                  pl.BlockSpec((hpb, QK_PAD, tq), lambda h, i: (h, 0, i + n_half)),
                  pl.BlockSpec((hpb, s, QK_PAD), lambda h, i: (h, 0, 0)),
                  pl.BlockSpec((hpb, s // tk, VT_PAD, tk), lambda h, i: (h, 0, 0, 0))],
        out_specs=pl.BlockSpec((2, tq, hpb * V_HEAD_DIM), lambda h, i: (0, i, h)),
        out_shape=jax.ShapeDtypeStruct((2, s // 2, nh * V_HEAD_DIM), BF16),
        scratch_shapes=[pltpu.VMEM((hpb, tk, tq), F32), pltpu.VMEM((hpb, tk, tq), F32),
                        pltpu.VMEM((2, hpb, 1, tq), F32), pltpu.VMEM((2, hpb, VT_PAD, tq), F32)],
        compiler_params=_params(("parallel", "arbitrary")),
        name="attention",
    )(qt, qt, k, vt)


def _output_kernel(x_ref, attn_ref, sz_ref, sgm_ref, co_ref, sgc_ref,
                   womla_ref, woconv_ref, wout_ref, g_ref, o_ref):
    a = (attn_ref[...].astype(F32) * sz_ref[...].astype(F32)).astype(BF16)
    y_mla = _dot(a, womla_ref[...])
    y_conv = _dot(co_ref[...], woconv_ref[...])
    merged = sgm_ref[...].astype(F32) * y_mla + sgc_ref[...].astype(F32) * y_conv
    out = _dot(merged.astype(BF16), wout_ref[...])
    o_ref[...] = x_ref[...] + _rms(out, g_ref[...])


def _output(x, attn, gates, co, womla, woconv, wout, g, tm=256):
    s, d = x.shape
    row = lambda c: pl.BlockSpec((tm, d), lambda i: (i, c))
    return pl.pallas_call(
        _output_kernel,
        grid=(s // tm,),
        in_specs=[row(0), row(0), row(0), row(1), row(0), row(2),
                  _resident(womla.shape), _resident(woconv.shape), _resident(wout.shape),
                  _resident(g.shape)],
        out_specs=row(0),
        out_shape=jax.ShapeDtypeStruct((s, d), F32),
        compiler_params=_params(("parallel",)),
        name="output",
    )(x, attn, gates, gates, co, gates, womla, woconv, wout, g)


def _prep_up_weights_kernel(wq_ref, wkv_ref, win_ref, wqt_ref, wk_ref, wvt_ref, wlat_ref):
    half = ROPE_DIM // 2
    _transpose_cast(wlat_ref, win_ref, LAT_IN - ROPE_DIM)
    rope_rows = jnp.concatenate([win_ref[LAT_IN - ROPE_DIM:LAT_IN, :], win_ref[LAT_IN - half:LAT_IN, :],
                                 win_ref[LAT_IN - ROPE_DIM:LAT_IN - half, :]], axis=0)
    wlat_ref[:, LAT_IN - ROPE_DIM:] = rope_rows.T.astype(BF16)

    wq_t = wq_ref[...].T
    for hd in range(N_HEADS):
        src = hd * (QK_NOPE_DIM + ROPE_DIM)
        dst = hd * QK_PAD
        rope = wq_t[src + QK_NOPE_DIM:src + QK_NOPE_DIM + ROPE_DIM, :].astype(BF16)
        wqt_ref[dst:dst + QK_NOPE_DIM, :] = wq_t[src:src + QK_NOPE_DIM, :].astype(BF16)
        wqt_ref[dst + QK_NOPE_DIM:dst + QK_NOPE_DIM + ROPE_DIM, :] = rope
        wqt_ref[dst + QK_NOPE_DIM + ROPE_DIM:dst + QK_NOPE_DIM + ROPE_DIM + half, :] = rope[half:, :]
        wqt_ref[dst + QK_NOPE_DIM + ROPE_DIM + half:dst + QK_PAD, :] = rope[:half, :]
        kv0 = hd * (QK_NOPE_DIM + V_HEAD_DIM)
        wk_ref[:, hd * QK_NOPE_DIM:(hd + 1) * QK_NOPE_DIM] = wkv_ref[:, kv0:kv0 + QK_NOPE_DIM].astype(BF16)
        wvt_ref[hd * V_HEAD_DIM:(hd + 1) * V_HEAD_DIM, :] = (
            wkv_ref[:, kv0 + QK_NOPE_DIM:kv0 + QK_NOPE_DIM + V_HEAD_DIM].T.astype(BF16))


def _prep_up_weights(w_q_b, w_kv_b, w_in_t):
    d = w_in_t.shape[1]
    whole = lambda shape: pl.BlockSpec(shape, lambda i: (0,) * len(shape))
    out_shapes = [(N_HEADS * QK_PAD, Q_LORA_RANK), (KV_LORA_RANK, N_HEADS * QK_NOPE_DIM),
                  (N_HEADS * V_HEAD_DIM, KV_LORA_RANK), (d, LAT_WIDTH)]
    return pl.pallas_call(
        _prep_up_weights_kernel,
        grid=(1,),
        in_specs=[whole(w_q_b.shape), whole(w_kv_b.shape),
                  pl.BlockSpec((pl.Element(LAT_IN), pl.Element(d)), lambda i: (0, 0))],
        out_specs=[whole(shape) for shape in out_shapes],
        out_shape=[jax.ShapeDtypeStruct(shape, BF16) for shape in out_shapes],
        compiler_params=_params(("arbitrary",)),
        name="prep_up_weights",
    )(w_q_b, w_kv_b, w_in_t)


def kernel(x, positions, pre_norm_g, w_in, q_a_norm_g, w_q_b, kv_a_norm_g, w_kv_b, conv_w,
           w_o_mla, w_o_conv, w_out, post_norm_g):
    b, s, d = x.shape
    assert b == 1 and d == D_MODEL
    x2 = x[0]

    w_in_t = w_in.T
    wqt, wk, wvt, wlat = _prep_up_weights(w_q_b, w_kv_b, w_in_t)

    inv_freq = ROPE_THETA ** (-jnp.arange(0, ROPE_DIM, 2, dtype=F32) / ROPE_DIM)
    ang = inv_freq[:, None] * positions[0].astype(F32)[None, :]
    cos, sin = jnp.cos(ang), jnp.sin(ang)
    zero_rows = jnp.zeros((LANES - ROPE_DIM, s), F32)
    cos_rows = jnp.concatenate([cos, cos, zero_rows], axis=0)
    sin_rows = jnp.concatenate([-sin, sin, zero_rows], axis=0)

    h, qt, k, vt = _latent(x2, pre_norm_g.reshape(1, d), wlat, q_a_norm_g.reshape(1, -1),
                           kv_a_norm_g.reshape(1, -1), wqt, wk, wvt, cos_rows, sin_rows)
    gates, womla, woconv, wout = _gates(h, w_in_t, (w_o_mla, w_o_conv, w_out))
    co = _conv(h, w_in_t, conv_w)
    attn = _attention(qt, k, vt).reshape(s, MLA_WIDTH)
    out = _output(x2, attn, gates, co, womla, woconv, wout, post_norm_g.reshape(1, d))
    return out[None]
```

```python
import functools
import math

import jax
import jax.numpy as jnp
from jax import lax
from jax.experimental import pallas as pl
from jax.experimental.pallas import tpu as pltpu

D_MODEL = 2048
N_HEADS = 16
QK_NOPE_DIM = 128
ROPE_DIM = 64
V_HEAD_DIM = 128
Q_LORA_RANK = 512
KV_LORA_RANK = 512
MLA_WIDTH = N_HEADS * V_HEAD_DIM
CONV_WIDTH = D_MODEL
CONV_K = 3
ROPE_THETA = 10000.0
RMS_EPS = 1e-6

LANES = 128
SUBLANES = 8
QK_PAD = QK_NOPE_DIM + LANES
BF16_SUBLANES = 16
VT_PAD = V_HEAD_DIM + BF16_SUBLANES
ATTN_TK = 512
ATTN_TQ = 2 * ATTN_TK
ATTN_HEADS = 2
HEAD_GROUP = 4
PREP_HEADS = 2
LAT_WIDTH = Q_LORA_RANK + KV_LORA_RANK + LANES
LAT_IN = Q_LORA_RANK + KV_LORA_RANK + ROPE_DIM
IN_Z_MLA = LAT_IN
IN_C_IN = IN_Z_MLA + MLA_WIDTH
IN_B_GATE = IN_C_IN + CONV_WIDTH
IN_C_GATE = IN_B_GATE + CONV_WIDTH
IN_Z_CONV = IN_C_GATE + CONV_WIDTH
IN_G_MLA = IN_Z_CONV + CONV_WIDTH
IN_G_CONV = IN_G_MLA + D_MODEL
VMEM_LIMIT = 56 * 1024 * 1024

BF16 = jnp.bfloat16
F32 = jnp.float32


def _params(semantics):
    return pltpu.CompilerParams(dimension_semantics=semantics, vmem_limit_bytes=VMEM_LIMIT)


def _resident(shape):
    return pl.BlockSpec(shape, lambda *_: (0,) * len(shape), pipeline_mode=pl.Buffered(1))


def _rms(xf, g):
    r = lax.rsqrt(jnp.mean(xf * xf, axis=-1, keepdims=True) + RMS_EPS)
    return xf * r * g


def _sigmoid(x):
    return 0.5 * jnp.tanh(0.5 * x) + 0.5


def _dot(a, b):
    return jnp.dot(a, b, preferred_element_type=F32)


def _dot_nt(a, b):
    return lax.dot_general(a, b, (((1,), (1,)), ((), ())), preferred_element_type=F32)


def _transpose_cast(dst_ref, src_ref, n_rows):
    for c in range(0, n_rows, LANES):
        dst_ref[:, c:c + LANES] = src_ref[c:c + LANES, :].T.astype(BF16)


def _wt_rows(tn, d, first_row):
    return pl.BlockSpec((pl.Element(tn), pl.Element(d)),
                        lambda j, i: (pl.multiple_of(first_row(j), SUBLANES), 0))


def _rope128(g2, cos_t, sin_t):
    return g2 * cos_t + pltpu.roll(g2, ROPE_DIM, axis=1) * sin_t


def _latent_kernel(x_ref, gpre_ref, wlat_ref, gq_ref, gkv_ref, wqt_ref, wk_ref, wvt_ref, cos_ref, sin_ref,
                   h_ref, qt_ref, k_ref, vt_ref, *, scale):
    h = _rms(x_ref[...], gpre_ref[...]).astype(BF16)
    h_ref[...] = h
    lat = _dot(h, wlat_ref[...])
    qa = _rms(lat[:, :Q_LORA_RANK], gq_ref[...]).astype(BF16)
    ckv = _rms(lat[:, Q_LORA_RANK:Q_LORA_RANK + KV_LORA_RANK], gkv_ref[...]).astype(BF16)
    cos_rows = cos_ref[...]
    sin_rows = sin_ref[...]
    k_rope = _rope128(lat[:, Q_LORA_RANK + KV_LORA_RANK:], cos_rows.T, sin_rows.T).astype(BF16)
    tm = qa.shape[0]
    vt = _dot_nt(wvt_ref[...], ckv)
    ones_rows = jnp.ones((VT_PAD - V_HEAD_DIM, tm), BF16)
    for grp in range(N_HEADS // HEAD_GROUP):
        qt = _dot_nt(wqt_ref[grp * HEAD_GROUP * QK_PAD:(grp + 1) * HEAD_GROUP * QK_PAD, :], qa) * scale
        kn = _dot(ckv, wk_ref[:, grp * HEAD_GROUP * QK_NOPE_DIM:(grp + 1) * HEAD_GROUP * QK_NOPE_DIM])
        for sub in range(HEAD_GROUP):
            hd = grp * HEAD_GROUP + sub
            base = sub * QK_PAD
            qt_ref[hd, :QK_NOPE_DIM, :] = qt[base:base + QK_NOPE_DIM, :].astype(BF16)
            g2 = qt[base + QK_NOPE_DIM:base + QK_PAD, :]
            g2_swapped = jnp.concatenate([g2[ROPE_DIM:], g2[:ROPE_DIM]], axis=0)
            qt_ref[hd, QK_NOPE_DIM:, :] = (g2 * cos_rows + g2_swapped * sin_rows).astype(BF16)
            k_ref[hd, :, :QK_NOPE_DIM] = kn[:, sub * QK_NOPE_DIM:(sub + 1) * QK_NOPE_DIM].astype(BF16)
            k_ref[hd, :, QK_NOPE_DIM:] = k_rope
            vt_ref[hd, 0, :V_HEAD_DIM, :] = vt[hd * V_HEAD_DIM:(hd + 1) * V_HEAD_DIM, :].astype(BF16)
            vt_ref[hd, 0, V_HEAD_DIM:, :] = ones_rows


def _latent(x, gpre, wlat, gq, gkv, wqt, wk, wvt, cos_rows, sin_rows, tm=512):
    s, d = x.shape
    per_blk = ATTN_TK // tm
    scale = math.log2(math.e) / math.sqrt(QK_NOPE_DIM + ROPE_DIM)
    return pl.pallas_call(
        functools.partial(_latent_kernel, scale=scale),
        grid=(s // tm,),
        in_specs=[
            pl.BlockSpec((tm, d), lambda i: (i, 0)), _resident(gpre.shape),
            _resident(wlat.shape), _resident(gq.shape), _resident(gkv.shape),
            _resident(wqt.shape), _resident(wk.shape), _resident(wvt.shape),
            pl.BlockSpec((LANES, tm), lambda i: (0, i)),
            pl.BlockSpec((LANES, tm), lambda i: (0, i)),
        ],
        out_specs=[
            pl.BlockSpec((tm, d), lambda i: (i, 0)),
            pl.BlockSpec((N_HEADS, QK_PAD, tm), lambda i: (0, 0, i)),
            pl.BlockSpec((N_HEADS, tm, QK_PAD), lambda i: (0, i, 0)),
            pl.BlockSpec((N_HEADS, 1, VT_PAD, tm), lambda i: (0, i // per_blk, 0, i % per_blk)),
        ],
        out_shape=[
            jax.ShapeDtypeStruct((s, d), BF16),
            jax.ShapeDtypeStruct((N_HEADS, QK_PAD, s), BF16),
            jax.ShapeDtypeStruct((N_HEADS, s, QK_PAD), BF16),
            jax.ShapeDtypeStruct((N_HEADS, s // ATTN_TK, VT_PAD, ATTN_TK), BF16),
        ],
        compiler_params=_params(("parallel",)),
        name="latent",
    )(x, gpre, wlat, gq, gkv, wqt, wk, wvt, cos_rows, sin_rows)


def _gates_kernel(h_ref, w_ref, f32a_ref, f32b_ref, f32c_ref, o_ref, bf16a_ref, bf16b_ref, bf16c_ref, w_sc,
                  *, n_silu_blocks):
    @pl.when(pl.program_id(1) == 0)
    def _():
        w_sc[...] = w_ref[...].astype(BF16)

    bf16a_ref[...] = f32a_ref[...].astype(BF16)
    bf16b_ref[...] = f32b_ref[...].astype(BF16)
    bf16c_ref[...] = f32c_ref[...].astype(BF16)

    is_silu = pl.program_id(0) < n_silu_blocks
    half_rows = h_ref.shape[0] // 2
    for part in range(2):
        rows = slice(part * half_rows, (part + 1) * half_rows)
        r = _dot_nt(h_ref[rows, :], w_sc[...]).astype(BF16)
        sg = _sigmoid(r)
        o_ref[rows, :] = jnp.where(is_silu, r * sg, sg)


def _gates(h, w_in_t, side_casts, tm=1024, tn=1024):
    s, d = h.shape
    n_silu_blocks = MLA_WIDTH // tn
    gap = IN_G_MLA - IN_Z_MLA - MLA_WIDTH
    n_j, n_i = 3 * D_MODEL // tn, s // tm
    cast_rows = next(r for r in range(BF16_SUBLANES, d + 1, BF16_SUBLANES)
                     if d % r == 0 and d // r <= n_j * n_i)
    last_cast_block = d // cast_rows - 1
    cast_spec = pl.BlockSpec((cast_rows, d), lambda j, i: (jnp.minimum(j * n_i + i, last_cast_block), 0))
    return pl.pallas_call(
        functools.partial(_gates_kernel, n_silu_blocks=n_silu_blocks),
        grid=(n_j, n_i),
        in_specs=[pl.BlockSpec((tm, d), lambda j, i: (i, 0)),
                  _wt_rows(tn, d, lambda j: IN_Z_MLA + j * tn + jnp.where(j < n_silu_blocks, 0, gap)),
                  cast_spec, cast_spec, cast_spec],
        out_specs=[pl.BlockSpec((tm, tn), lambda j, i: (i, j)), cast_spec, cast_spec, cast_spec],
        out_shape=[jax.ShapeDtypeStruct((s, 3 * D_MODEL), BF16)]
        + [jax.ShapeDtypeStruct(w.shape, BF16) for w in side_casts],
        scratch_shapes=[pltpu.VMEM((tn, d), BF16)],
        compiler_params=_params(("arbitrary", "arbitrary")),
        name="gates",
    )(h, w_in_t, *side_casts)


def _conv_kernel(h_ref, wci_ref, wbg_ref, wcg_ref, wzc_ref, cw_ref, o_ref,
                 wci_sc, wbg_sc, wcg_sc, wzc_sc, carry_ref):
    i = pl.program_id(1)

    @pl.when(i == 0)
    def _():
        carry_ref[...] = jnp.zeros_like(carry_ref)
        wci_sc[...] = wci_ref[...].astype(BF16)
        wbg_sc[...] = wbg_ref[...].astype(BF16)
        wcg_sc[...] = wcg_ref[...].astype(BF16)
        wzc_sc[...] = wzc_ref[...].astype(BF16)

    h = h_ref[...]
    u = _dot_nt(h, wcg_sc[...]) * _dot_nt(h, wci_sc[...])
    b_gate = _dot_nt(h, wbg_sc[...])
    prev = carry_ref[...]
    row = lax.broadcasted_iota(jnp.int32, u.shape, 0)
    u1 = jnp.where(row == 0, prev[SUBLANES - 1:SUBLANES, :], pltpu.roll(u, 1, axis=0))
    u2 = jnp.where(row == 0, prev[SUBLANES - 2:SUBLANES - 1, :],
                   jnp.where(row == 1, prev[SUBLANES - 1:SUBLANES, :], pltpu.roll(u, 2, axis=0)))
    cw = cw_ref[...]
    conv = cw[0:1, :] * u2 + cw[1:2, :] * u1 + cw[2:3, :] * u
    gated = b_gate * conv
    half_rows = h.shape[0] // 2
    for part in range(2):
        rows = slice(part * half_rows, (part + 1) * half_rows)
        z_conv = _dot_nt(h_ref[rows, :], wzc_sc[...])
        o_ref[rows, :] = (gated[rows, :] * (z_conv * _sigmoid(z_conv))).astype(BF16)
    carry_ref[...] = u[u.shape[0] - SUBLANES:, :]


def _conv(h, w_in_t, conv_w, tm=2048, tn=256):
    s, d = h.shape
    seg_specs = [_wt_rows(tn, d, lambda j, start=start: start + j * tn)
                 for start in (IN_C_IN, IN_B_GATE, IN_C_GATE, IN_Z_CONV)]
    return pl.pallas_call(
        _conv_kernel,
        grid=(CONV_WIDTH // tn, s // tm),
        in_specs=[pl.BlockSpec((tm, d), lambda j, i: (i, 0))] + seg_specs
        + [pl.BlockSpec((CONV_K, tn), lambda j, i: (0, j))],
        out_specs=pl.BlockSpec((tm, tn), lambda j, i: (i, j)),
        out_shape=jax.ShapeDtypeStruct((s, CONV_WIDTH), BF16),
        scratch_shapes=[pltpu.VMEM((tn, d), BF16)] * 4 + [pltpu.VMEM((SUBLANES, tn), F32)],
        compiler_params=_params(("arbitrary", "arbitrary")),
        name="conv",
    )(h, *([w_in_t] * 4), conv_w)


def _attn_kernel(qta_ref, qtb_ref, k_ref, vt_ref, sza_ref, szb_ref, o_ref, s0_ref, s1_ref, m_ref, acc_ref):
    i = pl.program_id(1)
    tk, tq = ATTN_TK, ATTN_TQ
    all_q = slice(0, tq)
    early_q = slice(0, tk)
    late_q = slice(tk, tq)
    heads = range(ATTN_HEADS)

    def qk(qt_ref, hd, j, s_ref, qs):
        k = k_ref[hd, pl.ds(pl.multiple_of(j * tk, tk), tk), :]
        s_ref[hd, :, qs] = _dot(k, qt_ref[hd, :, qs])

    def softmax_pv(slot, hd, j, s_ref, qs, keep=None):
        s = s_ref[hd, :, qs]
        if keep is not None:
            s = jnp.where(keep, s, -jnp.inf)
        m_prev = m_ref[slot, hd, :, qs]
        m_new = jnp.maximum(m_prev, jnp.max(s, axis=0, keepdims=True))
        alpha = jnp.exp2(m_prev - m_new)
        p = jnp.exp2(s - m_new).astype(BF16)
        acc_ref[slot, hd, :, qs] = alpha * acc_ref[slot, hd, :, qs] + _dot(vt_ref[hd, j], p)
        m_ref[slot, hd, :, qs] = m_new

    def first_scores(qt_ref):
        for hd in heads:
            qk(qt_ref, hd, 0, s0_ref, all_q)

    def pair(slot, qt_ref, j0):
        for hd in heads:
            qk(qt_ref, hd, j0 + 1, s1_ref, all_q)
            softmax_pv(slot, hd, j0, s0_ref, all_q)
        for hd in heads:
            qk(qt_ref, hd, j0 + 2, s0_ref, all_q)
            softmax_pv(slot, hd, j0 + 1, s1_ref, all_q)

    def paired_trips(slot, qt_ref, qi):
        def trip(jj, carry):
            pair(slot, qt_ref, 4 * jj)
            pair(slot, qt_ref, 4 * jj + 2)
            return carry

        lax.fori_loop(0, qi // 2, trip, 0)

    def finish(slot, qt_ref, qi, odd, then=None):
        if odd:
            pair(slot, qt_ref, 2 * qi - 2)
        diagonal_blocks(slot, qt_ref, qi)
        if then is not None:
            then()

    def diagonal_blocks(slot, qt_ref, qi):
        keep = (lax.broadcasted_iota(jnp.int32, (tk, tk), 0) <= lax.broadcasted_iota(jnp.int32, (tk, tk), 1))
        for hd in heads:
            qk(qt_ref, hd, 2 * qi + 1, s1_ref, late_q)
            softmax_pv(slot, hd, 2 * qi, s0_ref, early_q, keep)
            softmax_pv(slot, hd, 2 * qi, s0_ref, late_q)
        for hd in heads:
            softmax_pv(slot, hd, 2 * qi + 1, s1_ref, late_q, keep)
        for hd in heads:
            acc = acc_ref[slot, hd]
            out_t = acc[:V_HEAD_DIM, :] / acc[V_HEAD_DIM:V_HEAD_DIM + 1, :]
            cols = slice(hd * V_HEAD_DIM, (hd + 1) * V_HEAD_DIM)
            sz = (sza_ref, szb_ref)[slot][:, cols].astype(F32)
            o_ref[slot, :, cols] = (out_t.T.astype(BF16).astype(F32) * sz).astype(BF16)

    m_ref[...] = jnp.full_like(m_ref, -jnp.inf)
    acc_ref[...] = jnp.zeros_like(acc_ref)
    n_half = pl.num_programs(1)
    qi_a, qi_b = i, i + n_half
    is_odd = i % 2 == 1
    first_scores(qta_ref)
    paired_trips(0, qta_ref, qi_a)
    for odd in (True, False):
        @pl.when(is_odd == odd)
        def _(odd=odd):
            finish(0, qta_ref, qi_a, odd, then=lambda: first_scores(qtb_ref))
    paired_trips(1, qtb_ref, qi_b)
    for odd in (True, False):
        @pl.when(is_odd == odd)
        def _(odd=odd):
            finish(1, qtb_ref, qi_b, odd)


def _attention(qt, k, vt, gates):
    nh, s, _ = k.shape
    tk, tq, hpb = ATTN_TK, ATTN_TQ, ATTN_HEADS
    n_half = s // tq // 2
    return pl.pallas_call(
        _attn_kernel,
        grid=(nh // hpb, n_half),
        in_specs=[pl.BlockSpec((hpb, QK_PAD, tq), lambda h, i: (h, 0, i)),
                  pl.BlockSpec((hpb, QK_PAD, tq), lambda h, i: (h, 0, i + n_half)),
                  pl.BlockSpec((hpb, s, QK_PAD), lambda h, i: (h, 0, 0)),
                  pl.BlockSpec((hpb, s // tk, VT_PAD, tk), lambda h, i: (h, 0, 0, 0)),
                  pl.BlockSpec((tq, hpb * V_HEAD_DIM), lambda h, i: (i, h)),
                  pl.BlockSpec((tq, hpb * V_HEAD_DIM), lambda h, i: (i + n_half, h))],
        out_specs=pl.BlockSpec((2, tq, hpb * V_HEAD_DIM), lambda h, i: (0, i, h)),
        out_shape=jax.ShapeDtypeStruct((2, s // 2, nh * V_HEAD_DIM), BF16),
        scratch_shapes=[pltpu.VMEM((hpb, tk, tq), F32), pltpu.VMEM((hpb, tk, tq), F32),
                        pltpu.VMEM((2, hpb, 1, tq), F32), pltpu.VMEM((2, hpb, VT_PAD, tq), F32)],
        compiler_params=_params(("parallel", "arbitrary")),
        name="attention",
    )(qt, qt, k, vt, gates, gates)


def _output_kernel(x_ref, attn_ref, sgm_ref, co_ref, sgc_ref,
                   womla_ref, woconv_ref, wout_ref, g_ref, o_ref):
    y_mla = _dot(attn_ref[...], womla_ref[...])
    y_conv = _dot(co_ref[...], woconv_ref[...])
    merged = sgm_ref[...].astype(F32) * y_mla + sgc_ref[...].astype(F32) * y_conv
    out = _dot(merged.astype(BF16), wout_ref[...])
    o_ref[...] = x_ref[...] + _rms(out, g_ref[...])


def _output(x, attn, gates, co, womla, woconv, wout, g, tm=256):
    s, d = x.shape
    row = lambda c: pl.BlockSpec((tm, d), lambda i: (i, c))
    return pl.pallas_call(
        _output_kernel,
        grid=(s // tm,),
        in_specs=[row(0), row(0), row(1), row(0), row(2),
                  _resident(womla.shape), _resident(woconv.shape), _resident(wout.shape),
                  _resident(g.shape)],
        out_specs=row(0),
        out_shape=jax.ShapeDtypeStruct((s, d), F32),
        compiler_params=_params(("parallel",)),
        name="output",
    )(x, attn, gates, co, gates, womla, woconv, wout, g)


def _prep_up_weights_kernel(wq_ref, wkv_ref, win_ref, wqt_ref, wk_ref, wvt_ref, wlat_ref):
    half = ROPE_DIM // 2
    rows = win_ref[...]
    rope_rows = jnp.concatenate([rows[:ROPE_DIM], rows[half:ROPE_DIM], rows[:half]], axis=0)
    is_rope_chunk = pl.program_id(0) == pl.num_programs(0) - 1
    wlat_ref[...] = jnp.where(is_rope_chunk, rope_rows, rows).T.astype(BF16)

    wq_t = wq_ref[...].T
    for hd in range(PREP_HEADS):
        src = hd * (QK_NOPE_DIM + ROPE_DIM)
        dst = hd * QK_PAD
        rope = wq_t[src + QK_NOPE_DIM:src + QK_NOPE_DIM + ROPE_DIM, :].astype(BF16)
        wqt_ref[dst:dst + QK_NOPE_DIM, :] = wq_t[src:src + QK_NOPE_DIM, :].astype(BF16)
        wqt_ref[dst + QK_NOPE_DIM:dst + QK_NOPE_DIM + ROPE_DIM, :] = rope
        wqt_ref[dst + QK_NOPE_DIM + ROPE_DIM:dst + QK_NOPE_DIM + ROPE_DIM + half, :] = rope[half:, :]
        wqt_ref[dst + QK_NOPE_DIM + ROPE_DIM + half:dst + QK_PAD, :] = rope[:half, :]
        kv0 = hd * (QK_NOPE_DIM + V_HEAD_DIM)
        wk_ref[:, hd * QK_NOPE_DIM:(hd + 1) * QK_NOPE_DIM] = wkv_ref[:, kv0:kv0 + QK_NOPE_DIM].astype(BF16)
        wvt_ref[hd * V_HEAD_DIM:(hd + 1) * V_HEAD_DIM, :] = (
            wkv_ref[:, kv0 + QK_NOPE_DIM:kv0 + QK_NOPE_DIM + V_HEAD_DIM].T.astype(BF16))


def _prep_up_weights(w_q_b, w_kv_b, w_in_t):
    d = w_in_t.shape[1]
    out_shapes = [(N_HEADS * QK_PAD, Q_LORA_RANK), (KV_LORA_RANK, N_HEADS * QK_NOPE_DIM),
                  (N_HEADS * V_HEAD_DIM, KV_LORA_RANK), (d, LAT_WIDTH)]
    n_chunks = LAT_WIDTH // LANES
    assert n_chunks == N_HEADS // PREP_HEADS + 1
    grp = lambda i: jnp.minimum(i, n_chunks - 2)
    rank = Q_LORA_RANK
    return pl.pallas_call(
        _prep_up_weights_kernel,
        grid=(n_chunks,),
        in_specs=[pl.BlockSpec((rank, PREP_HEADS * (QK_NOPE_DIM + ROPE_DIM)), lambda i: (0, grp(i))),
                  pl.BlockSpec((rank, PREP_HEADS * (QK_NOPE_DIM + V_HEAD_DIM)), lambda i: (0, grp(i))),
                  pl.BlockSpec((pl.Element(LANES), pl.Element(d)),
                               lambda i: (pl.multiple_of(i * LANES, SUBLANES), 0))],
        out_specs=[pl.BlockSpec((PREP_HEADS * QK_PAD, rank), lambda i: (grp(i), 0)),
                   pl.BlockSpec((rank, PREP_HEADS * QK_NOPE_DIM), lambda i: (0, grp(i))),
                   pl.BlockSpec((PREP_HEADS * V_HEAD_DIM, rank), lambda i: (grp(i), 0)),
                   pl.BlockSpec((d, LANES), lambda i: (0, i))],
        out_shape=[jax.ShapeDtypeStruct(shape, BF16) for shape in out_shapes],
        compiler_params=_params(("arbitrary",)),
        name="prep_up_weights",
    )(w_q_b, w_kv_b, w_in_t)


def kernel(x, positions, pre_norm_g, w_in, q_a_norm_g, w_q_b, kv_a_norm_g, w_kv_b, conv_w,
           w_o_mla, w_o_conv, w_out, post_norm_g):
    b, s, d = x.shape
    assert b == 1 and d == D_MODEL
    x2 = x[0]

    w_in_t = w_in.T
    wqt, wk, wvt, wlat = _prep_up_weights(w_q_b, w_kv_b, w_in_t)

    inv_freq = ROPE_THETA ** (-jnp.arange(0, ROPE_DIM, 2, dtype=F32) / ROPE_DIM)
    ang = inv_freq[:, None] * positions[0].astype(F32)[None, :]
    cos, sin = jnp.cos(ang), jnp.sin(ang)
    zero_rows = jnp.zeros((LANES - ROPE_DIM, s), F32)
    cos_rows = jnp.concatenate([cos, cos, zero_rows], axis=0)
    sin_rows = jnp.concatenate([-sin, sin, zero_rows], axis=0)

    h, qt, k, vt = _latent(x2, pre_norm_g.reshape(1, d), wlat, q_a_norm_g.reshape(1, -1),
                           kv_a_norm_g.reshape(1, -1), wqt, wk, wvt, cos_rows, sin_rows)
    gates, womla, woconv, wout = _gates(h, w_in_t, (w_o_mla, w_o_conv, w_out))
    co = _conv(h, w_in_t, conv_w)
    attn = _attention(qt, k, vt, gates).reshape(s, MLA_WIDTH)
    out = _output(x2, attn, gates, co, womla, woconv, wout, post_norm_g.reshape(1, d))
    return out[None]
```
